```python
import jax
import jax.numpy as jnp
from jax import lax
import numpy as np

D_MODEL = 1024
BATCH = 8
SEQ = 4096
DEPTH = 2

GRID_W = 64
CTX_LEN = 256
RMS_EPS = 1e-6

FN_GROUPS = 4
FN_GROUP_DIM = 64
FN_WIDTH = FN_GROUPS * FN_GROUP_DIM
NA_HEADS = 8
NA_HEAD_DIM = 64
NA_WIDTH = NA_HEADS * NA_HEAD_DIM
NA_WIN_ROWS = 8
NA_WIN_COLS = 16
HG_HEADS = 4
HG_HEAD_DIM = 64
HG_WIDTH = HG_HEADS * HG_HEAD_DIM
HG_CHUNK = 64
N_BRANCHES = 3
N_EXPERTS = 256
TOP_K = 8
EXPERT_DIM = 256
SHARED_DIM = 256
ROUTE_SCALE = 2.5
MOE_BLOCK = 128

IN_SPLITS = (FN_WIDTH, NA_WIDTH, NA_WIDTH, NA_WIDTH, HG_WIDTH, HG_WIDTH, HG_WIDTH, HG_WIDTH, HG_WIDTH, N_BRANCHES * D_MODEL)
IN_COLS = FN_WIDTH + 3 * NA_WIDTH + 5 * HG_WIDTH + N_BRANCHES * D_MODEL

kernel_name = 'hybrid_fnet_natten_hgrn2_moe_dit'


def _rms_norm(x, gain):
    xf = x.astype(jnp.float32)
    y = xf * lax.rsqrt(jnp.mean(xf * xf, axis=-1, keepdims=True) + RMS_EPS)
    return (y * gain.astype(jnp.float32)).astype(x.dtype)


def _modulate(x, gain, shift, scale):
    return _rms_norm(x, gain) * (1 + scale) + shift


def _split_cols(z):
    offs = [int(o) for o in np.cumsum(IN_SPLITS)[:-1]]
    return jnp.split(z, offs, axis=-1)


def _heads(t, n_heads):
    return t.reshape(t.shape[:-1] + (n_heads, t.shape[-1] // n_heads))


def _fourier_mix(u):
    b, n, _ = u.shape
    ug = u.reshape(b, n, FN_GROUPS, FN_GROUP_DIM).astype(jnp.float32)
    f = jnp.fft.fft2(ug, axes=(1, 3), norm='ortho')
    return jnp.real(f).reshape(b, n, FN_WIDTH).astype(u.dtype)


def _na_latent(q, k, v, k_ctx, v_ctx, rpb):
    b, n, h, dh = q.shape
    rows = n // GRID_W
    wr = min(NA_WIN_ROWS, rows)
    wc = NA_WIN_COLS
    r = jnp.arange(rows)
    key_rows = jnp.clip(r - wr // 2, 0, rows - wr)[:, None] + jnp.arange(wr)[None, :]
    col = jnp.arange(GRID_W)
    col_start = jnp.clip(col - wc // 2, 0, GRID_W - wc)
    in_win = (col[None, :] >= col_start[:, None]) & (col[None, :] < col_start[:, None] + wc)
    dr_idx = key_rows - r[:, None] + (NA_WIN_ROWS - 1)
    dc_idx = jnp.clip(col[None, :] - col[:, None], 1 - wc, wc - 1) + (wc - 1)
    bias = rpb.astype(jnp.float32)[:, dr_idx[:, :, None, None], dc_idx[None, None, :, :]]
    bias = jnp.where(in_win[None, None, None], bias, -jnp.inf)
    bias = bias.transpose(0, 1, 3, 2, 4).reshape(h, rows, GRID_W, wr * GRID_W)
    scale = dh ** -0.5
    n_loc = wr * GRID_W
    qg = q.reshape(b, rows, GRID_W, h, dh)
    kg = k.reshape(b, rows, GRID_W, h, dh)[:, key_rows].reshape(b, rows, n_loc, h, dh)
    vg = v.reshape(b, rows, GRID_W, h, dh)[:, key_rows].reshape(b, rows, n_loc, h, dh)
    s_loc = jnp.einsum('brqhd,brkhd->bhrqk', qg, kg).astype(jnp.float32) * scale + bias
    s_ctx = jnp.einsum('brqhd,bchd->bhrqc', qg, k_ctx).astype(jnp.float32) * scale
    p = jax.nn.softmax(jnp.concatenate([s_loc, s_ctx], axis=-1), axis=-1).astype(v.dtype)
    o = (jnp.einsum('bhrqk,brkhd->brqhd', p[..., :n_loc], vg)
         + jnp.einsum('bhrqc,bchd->brqhd', p[..., n_loc:], v_ctx))
    return o.reshape(b, n, h * dh)


def _ctx_attention(q, k, v):
    b, n, h, dh = q.shape
    s = jnp.einsum('bqhd,bkhd->bhqk', q, k).astype(jnp.float32) * dh ** -0.5
    p = jax.nn.softmax(s, axis=-1).astype(v.dtype)
    return jnp.einsum('bhqk,bkhd->bqhd', p, v).reshape(b, n, h * dh)


def _hgrn_gates(z, lb):
    zf = z.astype(jnp.float32)
    log_f = jnp.logaddexp(jnp.log(lb), jnp.log1p(-lb) + jax.nn.log_sigmoid(zf))
    key = (1.0 - lb) * jax.nn.sigmoid(-zf)
    return _heads(log_f, HG_HEADS), _heads(key, HG_HEADS)


def _hgrn_inputs(hq, hi, hf, hb, lb_f, lb_b):
    q = _heads(jax.nn.silu(hq.astype(jnp.float32)), HG_HEADS)
    v = _heads(hi.astype(jnp.float32), HG_HEADS)
    lf_f, k_f = _hgrn_gates(hf, lb_f)
    lf_b, k_b = _hgrn_gates(hb, lb_b)
    return q, v, lf_f, k_f, lf_b, k_b


def _hgrn_scan(q, k, v, log_f, s0):
    b, n, h, _ = q.shape
    nc = n // HG_CHUNK

    def to_chunks(t):
        return t.reshape(b, nc, HG_CHUNK, h, t.shape[-1]).transpose(1, 0, 3, 2, 4).astype(jnp.float32)

    causal = jnp.tril(jnp.ones((HG_CHUNK, HG_CHUNK), dtype=bool))[None, None, :, :, None]

    def step(s, chunk):
        qc, kc, vc, gc = chunk
        bcum = jnp.cumsum(gc, axis=2)
        o_inter = jnp.einsum('bhtk,bhkv->bhtv', qc * jnp.exp(bcum), s)
        diff = bcum[:, :, :, None, :] - bcum[:, :, None, :, :]
        decay = jnp.exp(jnp.where(causal, diff, -jnp.inf))
        a = jnp.einsum('bhtk,bhsk,bhtsk->bhts', qc, kc, decay)
        o_intra = jnp.einsum('bhts,bhsv->bhtv', a, vc)
        b_last = bcum[:, :, -1:, :]
        s_new = (jnp.exp(b_last[:, :, 0, :, None]) * s
                 + jnp.einsum('bhsk,bhsv->bhkv', kc * jnp.exp(b_last - bcum), vc))
        return s_new, o_inter + o_intra

    s_fin, o = lax.scan(step, s0, (to_chunks(q), to_chunks(k), to_chunks(v), to_chunks(log_f)))
    o = o.transpose(1, 0, 3, 2, 4).reshape(b, n, h, -1)
    return o, s_fin


def _hgrn_bidir(q, v, lf_f, k_f, lf_b, k_b, s_f0, s_b0):
    o_f, s_f = _hgrn_scan(q, k_f, v, lf_f, s_f0)
    o_b, s_b = _hgrn_scan(q[:, ::-1], k_b[:, ::-1], v[:, ::-1], lf_b[:, ::-1], s_b0)
    return o_f + o_b[:, ::-1], s_f, s_b


def _hgrn_readout(o, gate_logits, gain, dtype):
    of = o * lax.rsqrt(jnp.mean(o * o, axis=-1, keepdims=True) + RMS_EPS) * gain.astype(jnp.float32)
    y = of * jax.nn.silu(_heads(gate_logits, HG_HEADS).astype(jnp.float32))
    return y.reshape(y.shape[:2] + (HG_WIDTH,)).astype(dtype)


def _merge(y_fn, y_na, y_hg, gate_logits, w_out_l):
    d = y_fn.shape[-1]
    g = jax.nn.sigmoid(gate_logits)
    m = g[..., :d] * y_fn + g[..., d:2 * d] * y_na + g[..., 2 * d:] * y_hg
    return m @ w_out_l


def _moe(t, w_router, b_router, w_gate, w_up, w_down, ws_gate, ws_up, ws_down):
    n_tok, d = t.shape
    scores = jax.nn.sigmoid((t @ w_router).astype(jnp.float32))
    _, idx = lax.top_k(scores + b_router.astype(jnp.float32), TOP_K)
    gate = jnp.take_along_axis(scores, idx, axis=-1)
    gate = ROUTE_SCALE * gate / jnp.sum(gate, axis=-1, keepdims=True)
    n_assign = n_tok * TOP_K
    e_flat = idx.reshape(-1)
    order = jnp.argsort(e_flat)
    e_sorted = e_flat[order]
    tok_sorted = (order // TOP_K).astype(jnp.int32)
    gate_sorted = gate.reshape(-1)[order]
    counts = jnp.bincount(e_flat, length=N_EXPERTS)
    starts = jnp.cumsum(counts) - counts
    padded = (counts + MOE_BLOCK - 1) // MOE_BLOCK * MOE_BLOCK
    pad_ends = jnp.cumsum(padded)
    pad_starts = pad_ends - padded
    dest = pad_starts[e_sorted] + jnp.arange(n_assign) - starts[e_sorted]
    n_blocks = -(-n_assign // MOE_BLOCK) + N_EXPERTS
    n_slots = n_blocks * MOE_BLOCK
    slot_tok = jnp.full((n_slots,), n_tok, dtype=jnp.int32).at[dest].set(tok_sorted)
    slot_gate = jnp.zeros((n_slots,), dtype=jnp.float32).at[dest].set(gate_sorted)
    block_expert = jnp.minimum(jnp.searchsorted(pad_ends, jnp.arange(n_blocks) * MOE_BLOCK, side='right'), N_EXPERTS - 1)
    t_pad = jnp.concatenate([t, jnp.zeros((1, d), t.dtype)], axis=0)

    def expert_block(acc, blk):
        tok, g, e = blk
        xb = t_pad[tok]
        hb = jax.nn.silu(xb @ w_gate[e]) * (xb @ w_up[e])
        return acc.at[tok].add((hb @ w_down[e]) * g[:, None].astype(t.dtype)), None

    routed, _ = lax.scan(expert_block, jnp.zeros_like(t_pad),
                         (slot_tok.reshape(n_blocks, MOE_BLOCK), slot_gate.reshape(n_blocks, MOE_BLOCK), block_expert))
    shared = (jax.nn.silu(t @ ws_gate) * (t @ ws_up)) @ ws_down
    return routed[:n_tok] + shared


def setup_inputs(seed: int = 0) -> dict:
    key = jax.random.key(seed)
    ks = jax.random.split(key, 26)
    d = D_MODEL

    def nrm(k, shape, s):
        return jax.random.normal(k, shape, jnp.float32) * s

    return {
        'x': nrm(ks[0], (BATCH, SEQ, d), 1.0),
        'c': nrm(ks[1], (BATCH, d), 1.0),
        'ctx': nrm(ks[2], (BATCH, CTX_LEN, d), 1.0),
        'c_ctx': nrm(ks[3], (d,), 1.0),
        'w_ada': nrm(ks[4], (DEPTH, d, 6 * d), 0.5 * d ** -0.5),
        'b_ada': nrm(ks[5], (DEPTH, 6 * d), 0.02),
        'norm1_gain': 1.0 + nrm(ks[6], (DEPTH, d), 0.1),
        'norm2_gain': 1.0 + nrm(ks[7], (DEPTH, d), 0.1),
        'w_in': nrm(ks[8], (DEPTH, d, IN_COLS), d ** -0.5),
        'na_rpb': nrm(ks[9], (DEPTH, NA_HEADS, 2 * NA_WIN_ROWS - 1, 2 * NA_WIN_COLS - 1), 0.1),
        'hg_lower_bounds': nrm(ks[10], (DEPTH, 2, HG_WIDTH), 0.5),
        'hg_norm_gain': 1.0 + nrm(ks[11], (DEPTH, HG_HEAD_DIM), 0.1),
        'w_fn_branch': nrm(ks[12], (DEPTH, FN_WIDTH, d), FN_WIDTH ** -0.5),
        'w_na_branch': nrm(ks[13], (DEPTH, NA_WIDTH, d), NA_WIDTH ** -0.5),
        'w_hg_branch': nrm(ks[14], (DEPTH, HG_WIDTH, d), HG_WIDTH ** -0.5),
        'w_out': nrm(ks[15], (DEPTH, d, d), d ** -0.5),
        'w_router': nrm(ks[16], (DEPTH, d, N_EXPERTS), d ** -0.5),
        'b_router': nrm(ks[17], (DEPTH, N_EXPERTS), 0.01),
        'w_exp_gate': nrm(ks[18], (DEPTH, N_EXPERTS, d, EXPERT_DIM), d ** -0.5),
        'w_exp_up': nrm(ks[19], (DEPTH, N_EXPERTS, d, EXPERT_DIM), d ** -0.5),
        'w_exp_down': nrm(ks[20], (DEPTH, N_EXPERTS, EXPERT_DIM, d), EXPERT_DIM ** -0.5),
        'w_sh_gate': nrm(ks[21], (DEPTH, d, SHARED_DIM), d ** -0.5),
        'w_sh_up': nrm(ks[22], (DEPTH, d, SHARED_DIM), d ** -0.5),
        'w_sh_down': nrm(ks[23], (DEPTH, SHARED_DIM, d), SHARED_DIM ** -0.5),
        'final_gain': 1.0 + nrm(ks[24], (d,), 0.1),
    }


def reference(x, c, ctx, c_ctx, w_ada, b_ada, norm1_gain, norm2_gain, w_in, na_rpb, hg_lower_bounds,
              hg_norm_gain, w_fn_branch, w_na_branch, w_hg_branch, w_out, w_router, b_router,
              w_exp_gate, w_exp_up, w_exp_down, w_sh_gate, w_sh_up, w_sh_down, final_gain):
    b, n, d = x.shape
    nc = ctx.shape[1]
    sm = jax.nn.softmax(hg_lower_bounds.astype(jnp.float32), axis=0)
    cs = jnp.cumsum(sm, axis=0)
    lower = cs - cs[0]
    silu_c = jax.nn.silu(c)
    silu_cc = jax.nn.silu(c_ctx)
    xc = ctx
    zero_state = jnp.zeros((b, HG_HEADS, HG_HEAD_DIM, HG_HEAD_DIM), jnp.float32)
    for l in range(DEPTH):
        last = l == DEPTH - 1
        mod = (silu_c @ w_ada[l] + b_ada[l])[:, None, :]
        mod_c = (silu_cc @ w_ada[l] + b_ada[l])[None, None, :]
        sh1, sc1, g1, sh2, sc2, g2 = jnp.split(mod, 6, axis=-1)
        sh1c, sc1c, g1c, sh2c, sc2c, g2c = jnp.split(mod_c, 6, axis=-1)

        h = _modulate(x, norm1_gain[l], sh1, sc1)
        hc = _modulate(xc, norm1_gain[l], sh1c, sc1c)
        fn, q, k, v, hq, hi, hf, hb, hg, bg = _split_cols(h @ w_in[l])
        fn_c, q_c, k_c, v_c, hq_c, hi_c, hf_c, hb_c, hg_c, bg_c = _split_cols(hc @ w_in[l])
        k_ch = _heads(k_c, NA_HEADS)
        v_ch = _heads(v_c, NA_HEADS)

        o_hc, s_f, s_b = _hgrn_bidir(*_hgrn_inputs(hq_c, hi_c, hf_c, hb_c, lower[l, 0], lower[l, 1]),
                                     zero_state, zero_state)
        o_hl, _, _ = _hgrn_bidir(*_hgrn_inputs(hq, hi, hf, hb, lower[l, 0], lower[l, 1]), s_f, s_b)

        y_fn = _fourier_mix(fn) @ w_fn_branch[l]
        y_na = _na_latent(_heads(q, NA_HEADS), _heads(k, NA_HEADS), _heads(v, NA_HEADS), k_ch, v_ch, na_rpb[l]) @ w_na_branch[l]
        y_hg = _hgrn_readout(o_hl, hg, hg_norm_gain[l], x.dtype) @ w_hg_branch[l]
        x = x + g1 * _merge(y_fn, y_na, y_hg, bg, w_out[l])
        if not last:
            y_fn_c = _fourier_mix(fn_c) @ w_fn_branch[l]
            y_na_c = _ctx_attention(_heads(q_c, NA_HEADS), k_ch, v_ch) @ w_na_branch[l]
            y_hg_c = _hgrn_readout(o_hc, hg_c, hg_norm_gain[l], xc.dtype) @ w_hg_branch[l]
            xc = xc + g1c * _merge(y_fn_c, y_na_c, y_hg_c, bg_c, w_out[l])

        h2 = _modulate(x, norm2_gain[l], sh2, sc2).reshape(b * n, d)
        if last:
            tokens = h2
        else:
            h2c = _modulate(xc, norm2_gain[l], sh2c, sc2c).reshape(b * nc, d)
            tokens = jnp.concatenate([h2, h2c], axis=0)
        ffn = _moe(tokens, w_router[l], b_router[l], w_exp_gate[l], w_exp_up[l], w_exp_down[l],
                   w_sh_gate[l], w_sh_up[l], w_sh_down[l])
        x = x + g2 * ffn[:b * n].reshape(b, n, d)
        if not last:
            xc = xc + g2c * ffn[b * n:].reshape(b, nc, d)
    return _rms_norm(x, final_gain)
```

```python
import functools

import numpy as np
import jax
import jax.numpy as jnp
from jax import lax
from jax.experimental import pallas as pl
from jax.experimental.pallas import tpu as pltpu

D_MODEL = 1024
GRID_W = 64
RMS_EPS = 1e-6
FN_GROUPS = 4
FN_GROUP_DIM = 64
FN_WIDTH = 256
NA_HEADS = 8
NA_HEAD_DIM = 64
NA_WIDTH = 512
NA_WIN_ROWS = 8
NA_WIN_COLS = 16
HG_HEADS = 4
HG_HEAD_DIM = 64
HG_WIDTH = 256
N_EXPERTS = 256
TOP_K = 8
EXPERT_DIM = 256
ROUTE_SCALE = 2.5

LANES = 128
SUBLANES = 8

F32 = jnp.float32
BF16 = jnp.bfloat16
VMEM_LIMIT = 56 * 1024 * 1024


def _cparams(*sem):
    return pltpu.CompilerParams(dimension_semantics=sem, vmem_limit_bytes=VMEM_LIMIT)


def _dot(a, b):
    return jnp.dot(a, b, preferred_element_type=F32)


def _dot_nt(a, b):
    return lax.dot_general(a, b, (((1,), (1,)), ((), ())), preferred_element_type=F32)


def _sigmoid(z):
    e = jnp.exp(-jnp.abs(z))
    r = 1.0 / (1.0 + e)
    return jnp.where(z >= 0, r, e * r)


def _silu(z):
    return z * _sigmoid(z)


def _norm_mod(x, gain, shift, scale):
    y = x * lax.rsqrt(jnp.mean(x * x, axis=-1, keepdims=True) + RMS_EPS)
    return (y * gain) * (1.0 + scale) + shift


def _ada_kernel(c_ref, w_ref, b_ref, o_ref):
    s = _silu(c_ref[...]).astype(BF16)
    o_ref[0] = _dot(s, w_ref[0].astype(BF16)) + b_ref[0]


def _ada_mod(cin, w_ada, b_ada):
    depth, d, n = w_ada.shape
    r = cin.shape[0]
    tn = 1536
    return pl.pallas_call(
        _ada_kernel,
        grid=(depth, n // tn),
        in_specs=[
            pl.BlockSpec((r, d), lambda l, j: (0, 0)),
            pl.BlockSpec((1, d, tn), lambda l, j: (l, 0, j)),
            pl.BlockSpec((1, 1, tn), lambda l, j: (l, 0, j)),
        ],
        out_specs=pl.BlockSpec((1, r, tn), lambda l, j: (l, 0, j)),
        out_shape=jax.ShapeDtypeStruct((depth, r, n), F32),
        compiler_params=_cparams("parallel", "parallel"),
        name="ada_mod",
    )(cin, w_ada, b_ada.reshape(depth, 1, n))


def _proj_in_kernel(x_ref, mod_ref, gain_ref, wfn_ref, wqkv_ref, whg_ref, fn_ref, qkv_ref, hg_ref):
    h = _norm_mod(x_ref[...], gain_ref[...], mod_ref[0, 0:1, :], mod_ref[0, 1:2, :]).astype(BF16)
    fn_ref[...] = _dot(h, wfn_ref[...])
    qkv_ref[...] = _dot(h, wqkv_ref[...]).astype(BF16)
    hg_ref[...] = _dot(h, whg_ref[...])


def _proj_in(x_all, mod, gain, w_fn, w_qkv, w_hg, n_lat, seq, tm=512):
    t, d = x_all.shape
    n_lat_tiles = n_lat // tm
    per_batch = seq // tm
    ctx_row = mod.shape[0] - 1

    def mod_map(i):
        return (jnp.where(i < n_lat_tiles, i // per_batch, ctx_row), 0, 0)

    const = lambda i: (0, 0)
    row = lambda i: (i, 0)
    return pl.pallas_call(
        _proj_in_kernel,
        grid=(t // tm,),
        in_specs=[
            pl.BlockSpec((tm, d), row),
            pl.BlockSpec((1, 6, d), mod_map),
            pl.BlockSpec((1, d), const),
            pl.BlockSpec(w_fn.shape, const),
            pl.BlockSpec(w_qkv.shape, const),
            pl.BlockSpec(w_hg.shape, const),
        ],
        out_specs=[
            pl.BlockSpec((tm, w_fn.shape[1]), row),
            pl.BlockSpec((tm, w_qkv.shape[1]), row),
            pl.BlockSpec((tm, w_hg.shape[1]), row),
        ],
        out_shape=[
            jax.ShapeDtypeStruct((t, w_fn.shape[1]), F32),
            jax.ShapeDtypeStruct((t, w_qkv.shape[1]), BF16),
            jax.ShapeDtypeStruct((t, w_hg.shape[1]), F32),
        ],
        compiler_params=_cparams("parallel"),
        name="proj_in",
    )(x_all, mod, gain.reshape(1, d), w_fn, w_qkv, w_hg)


def _dft_cos_sin(n):
    k = np.arange(n)
    ang = 2.0 * np.pi * ((k[:, None] * k[None, :]) % n) / n
    return np.cos(ang), np.sin(ang)


def _channel_dft_mats(scale):
    c, s = _dft_cos_sin(FN_GROUP_DIM)
    eye = np.eye(FN_GROUPS)
    return (jnp.asarray(np.kron(eye, c) * scale, BF16), jnp.asarray(np.kron(eye, -s) * scale, BF16))


FFT_R = 64


def _store_halves(ref, lead, rows, val):
    ref[lead + (0, rows, slice(None))] = val[:, :LANES]
    ref[lead + (1, rows, slice(None))] = val[:, LANES:]


def _load_halves(ref, lead, rows):
    return jnp.concatenate([ref[lead + (0, rows, slice(None))], ref[lead + (1, rows, slice(None))]], axis=1)


def _fourier_lat_kernel(u_ref, cc_ref, sc_ref, ga_ref, gb_ref, tc_ref, ts_ref, o_ref, u_s, z_s):
    ub = u_ref[...].astype(BF16)
    every = slice(None)
    _store_halves(u_s, (0,), every, _dot(ub, cc_ref[...]))
    _store_halves(u_s, (1,), every, _dot(ub, sc_ref[...]))

    def stage_a(n2, carry):
        rows = pl.ds(n2, FFT_R, stride=FFT_R)
        x = jnp.concatenate([_load_halves(u_s, (0,), rows), _load_halves(u_s, (1,), rows)], axis=0).astype(BF16)
        a = _dot(ga_ref[...], x)
        ar, ai = a[:FFT_R], a[FFT_R:]
        tc = jnp.concatenate([tc_ref[n2], tc_ref[n2]], axis=1)
        ts = jnp.concatenate([ts_ref[n2], ts_ref[n2]], axis=1)
        dst = pl.ds(pl.multiple_of(n2 * FFT_R, FFT_R), FFT_R)
        _store_halves(z_s, (0,), dst, ar * tc + ai * ts)
        _store_halves(z_s, (1,), dst, ai * tc - ar * ts)
        return carry

    lax.fori_loop(0, FFT_R, stage_a, 0)

    def stage_b(k1, carry):
        rows = pl.ds(k1, FFT_R, stride=FFT_R)
        z = jnp.concatenate([_load_halves(z_s, (0,), rows), _load_halves(z_s, (1,), rows)], axis=0).astype(BF16)
        _store_halves(o_ref, (), rows, _dot(gb_ref[...], z))
        return carry

    lax.fori_loop(0, FFT_R, stage_b, 0)


def _fourier_lat(z_fn, n_batch, seq):
    assert seq == FFT_R * FFT_R
    c, s = _dft_cos_sin(FFT_R)
    cc, sc = _channel_dft_mats((seq * FN_GROUP_DIM) ** -0.5)
    ga = jnp.asarray(np.block([[c, s], [-s, c]]), BF16)
    gb = jnp.asarray(np.concatenate([c, s], axis=1), BF16)
    k = np.arange(FFT_R)
    ang = 2.0 * np.pi * (k[:, None] * k[None, :]) / seq
    tc = jnp.asarray(np.broadcast_to(np.cos(ang)[:, :, None], (FFT_R, FFT_R, 128)), F32)
    ts = jnp.asarray(np.broadcast_to(np.sin(ang)[:, :, None], (FFT_R, FFT_R, 128)), F32)
    const2 = lambda b: (0, 0)
    const3 = lambda b: (0, 0, 0)
    return pl.pallas_call(
        _fourier_lat_kernel,
        grid=(n_batch,),
        in_specs=[
            pl.BlockSpec((seq, FN_WIDTH), lambda b: (b, 0)),
            pl.BlockSpec(cc.shape, const2),
            pl.BlockSpec(sc.shape, const2),
            pl.BlockSpec(ga.shape, const2),
            pl.BlockSpec(gb.shape, const2),
            pl.BlockSpec(tc.shape, const3),
            pl.BlockSpec(ts.shape, const3),
        ],
        out_specs=pl.BlockSpec((2, seq, LANES), lambda b: (0, b, 0)),
        out_shape=jax.ShapeDtypeStruct((2, n_batch * seq, LANES), F32),
        scratch_shapes=[pltpu.VMEM((2, 2, seq, LANES), F32), pltpu.VMEM((2, 2, seq, LANES), F32)],
        compiler_params=_cparams("parallel"),
        name="fourier_lat",
    )(z_fn, cc, sc, ga, gb, tc, ts)


def _fourier_ctx_kernel(u_ref, cc_ref, sc_ref, g_ref, o_ref):
    ub = u_ref[...].astype(BF16)
    x = jnp.concatenate([_dot(ub, cc_ref[...]), _dot(ub, sc_ref[...])], axis=0).astype(BF16)
    _store_halves(o_ref, (), slice(None), _dot(g_ref[...], x))


def _fourier_ctx(z_fn, n_batch, n_ctx, row0):
    blk0 = row0 // n_ctx
    c, s = _dft_cos_sin(n_ctx)
    cc, sc = _channel_dft_mats((n_ctx * FN_GROUP_DIM) ** -0.5)
    g = jnp.asarray(np.concatenate([c, s], axis=1), BF16)
    const2 = lambda b: (0, 0)
    return pl.pallas_call(
        _fourier_ctx_kernel,
        grid=(n_batch,),
        in_specs=[
            pl.BlockSpec((n_ctx, FN_WIDTH), lambda b: (blk0 + b, 0)),
            pl.BlockSpec(cc.shape, const2),
            pl.BlockSpec(sc.shape, const2),
            pl.BlockSpec(g.shape, const2),
        ],
        out_specs=pl.BlockSpec((2, n_ctx, LANES), lambda b: (0, b, 0)),
        out_shape=jax.ShapeDtypeStruct((2, n_batch * n_ctx, LANES), F32),
        compiler_params=_cparams("parallel"),
        name="fourier_ctx",
    )(z_fn, cc, sc, g)


NEG_BIG = -1e30
HEAD_PAIR = 2 * NA_HEAD_DIM


def _na_bias_table(rpb):
    wr, wc, w = NA_WIN_ROWS, NA_WIN_COLS, GRID_W
    col = np.arange(w)
    col_start = np.clip(col - wc // 2, 0, w - wc)
    in_win = (col[None, :] >= col_start[:, None]) & (col[None, :] < col_start[:, None] + wc)
    dc_idx = np.clip(col[None, :] - col[:, None], 1 - wc, wc - 1) + (wc - 1)
    dr_idx = np.arange(wr)[:, None] + np.arange(wr)[None, :]
    b = rpb.astype(F32)[:, dr_idx[:, :, None, None], dc_idx[None, None, :, :]]
    b = jnp.where(in_win[None, None, None], b, NEG_BIG)
    return b.transpose(0, 1, 3, 2, 4).reshape(rpb.shape[0], wr, w, wr * w)


def _softmax_pv(q, keys, vals, biases):
    s = []
    for k, b in zip(keys, biases):
        si = _dot_nt(q, k)
        s.append(si if b is None else si + b)
    m = s[0].max(axis=-1, keepdims=True)
    for si in s[1:]:
        m = jnp.maximum(m, si.max(axis=-1, keepdims=True))
    acc = None
    l = None
    for si, v in zip(s, vals):
        p = jnp.exp(si - m)
        li = p.sum(axis=-1, keepdims=True)
        oi = _dot(p.astype(BF16), v)
        acc = oi if acc is None else acc + oi
        l = li if l is None else l + li
    return acc / l


def _na_lat_kernel(q_ref, k_ref, v_ref, kc_ref, vc_ref, bias_ref, o_ref):
    w = GRID_W
    n_loc = NA_WIN_ROWS * w
    rows = q_ref.shape[0] // w
    lane = lax.broadcasted_iota(jnp.int32, (w, HEAD_PAIR), 1)
    first = lane < NA_HEAD_DIM
    kc = kc_ref[...]
    vc = vc_ref[...]

    def body(r, carry):
        kr0 = jnp.clip(r - NA_WIN_ROWS // 2, 0, rows - NA_WIN_ROWS)
        s = kr0 - r + (NA_WIN_ROWS - 1)
        q = q_ref[pl.ds(pl.multiple_of(r * w, w), w), :] * jnp.asarray(NA_HEAD_DIM ** -0.5, BF16)
        ks = k_ref[pl.ds(pl.multiple_of(kr0 * w, w), n_loc), :]
        vs = v_ref[pl.ds(pl.multiple_of(kr0 * w, w), n_loc), :]
        zero = jnp.zeros_like(q)
        o0 = _softmax_pv(jnp.where(first, q, zero), [ks, kc], [vs, vc], [bias_ref[0, s], None])
        o1 = _softmax_pv(jnp.where(first, zero, q), [ks, kc], [vs, vc], [bias_ref[1, s], None])
        o_ref[pl.ds(pl.multiple_of(r * w, w), w), :] = jnp.where(first, o0, o1).astype(o_ref.dtype)
        return carry

    lax.fori_loop(0, rows, body, 0)


def _na_lat(qkv, bias, n_batch, seq, n_ctx):
    n_pairs = NA_WIDTH // HEAD_PAIR
    ctx0 = n_batch * seq // n_ctx
    wr, w = NA_WIN_ROWS, GRID_W
    return pl.pallas_call(
        _na_lat_kernel,
        grid=(n_batch, n_pairs),
        in_specs=[
            pl.BlockSpec((seq, HEAD_PAIR), lambda b, p: (b, p)),
            pl.BlockSpec((seq, HEAD_PAIR), lambda b, p: (b, n_pairs + p)),
            pl.BlockSpec((seq, HEAD_PAIR), lambda b, p: (b, 2 * n_pairs + p)),
            pl.BlockSpec((n_ctx, HEAD_PAIR), lambda b, p: (ctx0 + b, n_pairs + p)),
            pl.BlockSpec((n_ctx, HEAD_PAIR), lambda b, p: (ctx0 + b, 2 * n_pairs + p)),
            pl.BlockSpec((2, wr, w, wr * w), lambda b, p: (p, 0, 0, 0)),
        ],
        out_specs=pl.BlockSpec((seq, HEAD_PAIR), lambda b, p: (b, p)),
        out_shape=jax.ShapeDtypeStruct((n_batch * seq, NA_WIDTH), BF16),
        compiler_params=_cparams("parallel", "parallel"),
        name="na_lat",
    )(qkv, qkv, qkv, qkv, qkv, bias)


def _na_ctx_kernel(q_ref, k_ref, v_ref, o_ref):
    lane = lax.broadcasted_iota(jnp.int32, q_ref.shape, 1)
    first = lane < NA_HEAD_DIM
    q = q_ref[...] * jnp.asarray(NA_HEAD_DIM ** -0.5, BF16)
    k = k_ref[...]
    v = v_ref[...]
    zero = jnp.zeros_like(q)
    o0 = _softmax_pv(jnp.where(first, q, zero), [k], [v], [None])
    o1 = _softmax_pv(jnp.where(first, zero, q), [k], [v], [None])
    o_ref[...] = jnp.where(first, o0, o1).astype(o_ref.dtype)


def _na_ctx(qkv, n_batch, seq, n_ctx):
    n_pairs = NA_WIDTH // HEAD_PAIR
    ctx0 = n_batch * seq // n_ctx
    return pl.pallas_call(
        _na_ctx_kernel,
        grid=(n_batch, n_pairs),
        in_specs=[
            pl.BlockSpec((n_ctx, HEAD_PAIR), lambda b, p: (ctx0 + b, p)),
            pl.BlockSpec((n_ctx, HEAD_PAIR), lambda b, p: (ctx0 + b, n_pairs + p)),
            pl.BlockSpec((n_ctx, HEAD_PAIR), lambda b, p: (ctx0 + b, 2 * n_pairs + p)),
        ],
        out_specs=pl.BlockSpec((n_ctx, HEAD_PAIR), lambda b, p: (b, p)),
        out_shape=jax.ShapeDtypeStruct((n_batch * n_ctx, NA_WIDTH), BF16),
        compiler_params=_cparams("parallel", "parallel"),
        name="na_ctx",
    )(qkv, qkv, qkv)


HG_CHUNK = 64
HG_LEVELS = (32, 16, 8)
HG_DIAG = 8
HG_NMAT = 3 + 2 * len(HG_LEVELS)


def _hgrn_consts():
    c = HG_CHUNK
    mats, masks = [], []
    for reverse in (False, True):
        u = np.arange(c)[::-1] if reverse else np.arange(c)
        ut, uj = u[:, None], u[None, :]
        m = [uj <= ut, uj > ut]
        lv = []
        for h in HG_LEVELS:
            same = (ut // (2 * h)) == (uj // (2 * h))
            up_t, up_j = (ut % (2 * h)) >= h, (uj % (2 * h)) >= h
            m.append(same & up_t & up_j & (uj <= ut))
            m.append(same & ~up_t & ~up_j & (uj > ut))
            lv.append(same & up_t & ~up_j)
        same_d = (ut // HG_DIAG) == (uj // HG_DIAG)
        m.append(same_d & (uj <= ut))
        lv.append(same_d & (uj <= ut))
        mats.append(np.concatenate(m, axis=0).astype(np.float32))
        masks.append(np.stack([np.tile(x, (1, HG_HEADS)) for x in lv]).astype(np.float32))
    hm = (np.arange(HG_WIDTH)[:, None] // HG_HEAD_DIM) == (np.arange(HG_WIDTH)[None, :] // HG_HEAD_DIM)
    return (jnp.asarray(np.stack(mats), BF16), jnp.asarray(np.stack(masks), F32), jnp.asarray(hm, F32))


def _split3(x):
    hi = x.astype(BF16)
    r = x - hi.astype(F32)
    mid = r.astype(BF16)
    lo = (r - mid.astype(F32)).astype(BF16)
    return hi, mid, lo


def _hgrn_chunk(zq, zv, zf, lbp, mstack, masks, hm, st, last_row):
    c = HG_CHUNK
    q = _silu(zq)
    e = jnp.exp(-jnp.abs(zf))
    inv = 1.0 / (1.0 + e)
    log_sig = jnp.minimum(zf, 0.0) - jnp.log(1.0 + e)
    sig_neg = jnp.where(zf >= 0, e * inv, inv)
    a = lbp[0:1, :]
    cc = lbp[1:2, :] + log_sig
    log_f = jnp.maximum(a, cc) + jnp.log(1.0 + jnp.exp(-jnp.abs(a - cc)))
    k = lbp[2:3, :] * sig_neg

    hi, mid, lo = _split3(log_f)
    ex = _dot(mstack, hi) + _dot(mstack, mid) + _dot(mstack, lo)
    ee = jnp.exp(ex)
    blk = lambda i: ee[i * c:(i + 1) * c]
    kd_fac = jnp.exp(-ex[(HG_NMAT - 1) * c:])

    def heads_bd(x):
        return (jnp.concatenate([x] * HG_HEADS, axis=0) * hm).astype(BF16)

    n_lv = len(HG_LEVELS)
    a_all = None
    for i in range(n_lv + 1):
        if i < n_lv:
            qx, kx = q * blk(2 + 2 * i), k * blk(3 + 2 * i)
        else:
            qx, kx = q * blk(HG_NMAT - 1), k * kd_fac
        ai = _dot_nt(qx.astype(BF16), heads_bd(kx)) * masks[i]
        a_all = ai if a_all is None else a_all + ai
    o = _dot(a_all.astype(BF16), heads_bd(zv)) + _dot_nt((q * blk(0)).astype(BF16), st.astype(BF16))
    kl = (k * blk(1)).astype(BF16)
    upd = lax.dot_general(zv.astype(BF16), kl, (((0,), (0,)), ((), ())), preferred_element_type=F32)
    st_new = st * ee[last_row:last_row + 1] + upd * hm
    return o, st_new


def _hgrn_kernel(zf_ref, zb_ref, zbg_ref, s0_ref, lbp_ref, m_ref, mask_ref, hm_ref,
                 of_ref, ob_ref, sfin_ref, st_f, st_b):
    i = pl.program_id(1)

    @pl.when(i == 0)
    def _():
        st_f[...] = s0_ref[0, 0]
        st_b[...] = s0_ref[0, 1]

    w = HG_WIDTH
    hm = hm_ref[...]
    o, s = _hgrn_chunk(zf_ref[:, 0:w], zf_ref[:, w:2 * w], zf_ref[:, 2 * w:3 * w], lbp_ref[0],
                       m_ref[0], mask_ref[0], hm, st_f[...], HG_CHUNK - 1)
    of_ref[...] = o
    st_f[...] = s
    o, s = _hgrn_chunk(zb_ref[:, 0:w], zb_ref[:, w:2 * w], zbg_ref[...], lbp_ref[1],
                       m_ref[1], mask_ref[1], hm, st_b[...], 0)
    ob_ref[...] = o
    st_b[...] = s

    @pl.when(i == pl.num_programs(1) - 1)
    def _():
        sfin_ref[0, 0] = st_f[...]
        sfin_ref[0, 1] = st_b[...]


def _hgrn(z_hg, s0, lbp, consts, n_batch, n_tok, row0):
    mstack, masks, hm = consts
    c, w = HG_CHUNK, HG_WIDTH
    nch = n_tok // c
    base = row0 // c
    fwd = lambda b, i: (base + b * nch + i, 0)
    bwd = lambda b, i: (base + b * nch + (nch - 1 - i), 0)
    const3 = lambda b, i: (0, 0, 0)
    return pl.pallas_call(
        _hgrn_kernel,
        grid=(n_batch, nch),
        in_specs=[
            pl.BlockSpec((c, 3 * w), fwd),
            pl.BlockSpec((c, 2 * w), bwd),
            pl.BlockSpec((c, w), lambda b, i: (base + b * nch + (nch - 1 - i), 3)),
            pl.BlockSpec((1, 2, w, w), lambda b, i: (b, 0, 0, 0)),
            pl.BlockSpec(lbp.shape, const3),
            pl.BlockSpec(mstack.shape, const3),
            pl.BlockSpec(masks.shape, lambda b, i: (0, 0, 0, 0)),
            pl.BlockSpec(hm.shape, lambda b, i: (0, 0)),
        ],
        out_specs=[
            pl.BlockSpec((c, w), lambda b, i: (b * nch + i, 0)),
            pl.BlockSpec((c, w), lambda b, i: (b * nch + (nch - 1 - i), 0)),
            pl.BlockSpec((1, 2, w, w), lambda b, i: (b, 0, 0, 0)),
        ],
        out_shape=[
            jax.ShapeDtypeStruct((n_batch * n_tok, w), F32),
            jax.ShapeDtypeStruct((n_batch * n_tok, w), F32),
            jax.ShapeDtypeStruct((n_batch, 2, w, w), F32),
        ],
        scratch_shapes=[pltpu.VMEM((w, w), F32), pltpu.VMEM((w, w), F32)],
        compiler_params=_cparams("parallel", "arbitrary"),
        name="hgrn",
    )(z_hg, z_hg, z_hg, s0, lbp, mstack, masks, hm)


def _hgrn_lb_params(lower_l):
    rows = jnp.stack([jnp.log(lower_l), jnp.log1p(-lower_l), 1.0 - lower_l], axis=1)
    return jnp.pad(rows, ((0, 0), (0, 5), (0, 0)))


def _merge_kernel(x_ref, mod_ref, g1_ref, g2_ref, fn_ref, na_ref, of_ref, ob_ref, hgz_ref, hgain_ref,
                  hmean_ref, wfn_ref, wna_ref, whg_ref, wbg_ref, wout_ref, x1_ref, h2_ref):
    d = x_ref.shape[1]
    x = x_ref[...]
    h = _norm_mod(x, g1_ref[...], mod_ref[0, 0:1, :], mod_ref[0, 1:2, :]).astype(BF16)
    o = of_ref[...] + ob_ref[...]
    hi, mid, lo = _split3(o * o)
    ms = _dot(hi, hmean_ref[...]) + _dot(mid, hmean_ref[...]) + _dot(lo, hmean_ref[...])
    y_hg_in = o * lax.rsqrt(ms + RMS_EPS) * hgain_ref[...] * _silu(hgz_ref[...])
    y_fn = _dot(jnp.concatenate([fn_ref[0], fn_ref[1]], axis=1).astype(BF16), wfn_ref[...])
    y_na = _dot(na_ref[...], wna_ref[...])
    y_hg = _dot(y_hg_in.astype(BF16), whg_ref[...])
    m = (_sigmoid(_dot(h, wbg_ref[:, 0:d])) * y_fn
         + _sigmoid(_dot(h, wbg_ref[:, d:2 * d])) * y_na
         + _sigmoid(_dot(h, wbg_ref[:, 2 * d:3 * d])) * y_hg)
    x1 = x + mod_ref[0, 2:3, :] * _dot(m.astype(BF16), wout_ref[...])
    x1_ref[...] = x1
    h2_ref[...] = _norm_mod(x1, g2_ref[...], mod_ref[0, 3:4, :], mod_ref[0, 4:5, :])


def _merge(x_all, mod, gain1, gain2, fn, na, o_f, o_b, z_hg, hgain, w_fnb, w_nab, w_hgb, w_bg, w_out,
           n_rows, n_lat, seq, tm=256):
    t, d = n_rows, x_all.shape[1]
    n_lat_tiles = n_lat // tm
    per_batch = seq // tm
    ctx_row = mod.shape[0] - 1
    w = HG_WIDTH
    hmean = jnp.asarray(np.kron(np.eye(HG_HEADS), np.full((HG_HEAD_DIM, HG_HEAD_DIM), 1.0 / HG_HEAD_DIM)), BF16)

    def mod_map(i):
        return (jnp.where(i < n_lat_tiles, i // per_batch, ctx_row), 0, 0)

    const = lambda i: (0, 0)
    row = lambda i: (i, 0)
    full = lambda a: pl.BlockSpec(a.shape, const)
    return pl.pallas_call(
        _merge_kernel,
        grid=(t // tm,),
        in_specs=[
            pl.BlockSpec((tm, d), row),
            pl.BlockSpec((1, 6, d), mod_map),
            pl.BlockSpec((1, d), const),
            pl.BlockSpec((1, d), const),
            pl.BlockSpec((2, tm, LANES), lambda i: (0, i, 0)),
            pl.BlockSpec((tm, NA_WIDTH), row),
            pl.BlockSpec((tm, w), row),
            pl.BlockSpec((tm, w), row),
            pl.BlockSpec((tm, w), lambda i: (i, 4)),
            pl.BlockSpec((1, w), const),
            full(hmean), full(w_fnb), full(w_nab), full(w_hgb), full(w_bg), full(w_out),
        ],
        out_specs=[pl.BlockSpec((tm, d), row), pl.BlockSpec((tm, d), row)],
        out_shape=[jax.ShapeDtypeStruct((t, d), F32), jax.ShapeDtypeStruct((t, d), F32)],
        compiler_params=_cparams("parallel"),
        name="merge",
    )(x_all, mod, gain1.reshape(1, d), gain2.reshape(1, d), fn, na, o_f, o_b, z_hg,
      jnp.tile(hgain, HG_HEADS).reshape(1, w), hmean, w_fnb, w_nab, w_hgb, w_bg, w_out)


def _router_kernel(h_ref, whi_ref, wlo_ref, b_ref, idx_ref, gate_ref):
    h = h_ref[...]
    hi = h.astype(BF16)
    lo = (h - hi.astype(F32)).astype(BF16)
    logits = _dot(hi, whi_ref[...]) + (_dot(hi, wlo_ref[...]) + _dot(lo, whi_ref[...]))
    scores = _sigmoid(logits)
    sel = scores + b_ref[...]
    tm, ne = sel.shape
    col = lax.broadcasted_iota(jnp.int32, (tm, ne), 1)
    out_lane = lax.broadcasted_iota(jnp.int32, (tm, LANES), 1)
    idx_out = jnp.zeros((tm, LANES), jnp.int32)
    gate_out = jnp.zeros((tm, LANES), F32)
    total = jnp.zeros((tm, 1), F32)
    for k in range(TOP_K):
        m = sel.max(axis=-1, keepdims=True)
        idx = jnp.where(sel == m, col, ne).min(axis=-1, keepdims=True)
        hit = col == idx
        g = jnp.where(hit, scores, 0.0).sum(axis=-1, keepdims=True)
        sel = jnp.where(hit, -jnp.inf, sel)
        idx_out = jnp.where(out_lane == k, idx, idx_out)
        gate_out = jnp.where(out_lane == k, g, gate_out)
        total = total + g
    idx_ref[...] = idx_out
    gate_ref[...] = gate_out * (ROUTE_SCALE / total)


def _router(h2, w_router, b_router, tm=512):
    t, d = h2.shape
    ne = w_router.shape[1]
    w_hi = w_router.astype(BF16)
    w_lo = (w_router - w_hi.astype(F32)).astype(BF16)
    const = lambda i: (0, 0)
    row = lambda i: (i, 0)
    return pl.pallas_call(
        _router_kernel,
        grid=(t // tm,),
        in_specs=[pl.BlockSpec((tm, d), row), pl.BlockSpec((d, ne), const), pl.BlockSpec((d, ne), const),
                  pl.BlockSpec((1, ne), const)],
        out_specs=[pl.BlockSpec((tm, LANES), row), pl.BlockSpec((tm, LANES), row)],
        out_shape=[jax.ShapeDtypeStruct((t, LANES), jnp.int32), jax.ShapeDtypeStruct((t, LANES), F32)],
        compiler_params=_cparams("parallel"),
        name="router",
    )(h2, w_hi, w_lo, b_router.reshape(1, ne).astype(F32))


MOE_TILE = 4096
MOE_ROWS = 128


def _moe_dispatch(idx, gate, n_tiles):
    t, k = idx.shape
    pad = n_tiles * MOE_TILE - t
    e = jnp.pad(idx, ((0, pad), (0, 0)), constant_values=N_EXPERTS).reshape(n_tiles, MOE_TILE * k)
    g = jnp.pad(gate, ((0, pad), (0, 0))).reshape(n_tiles, MOE_TILE * k)
    order = jnp.argsort(e, axis=1).astype(jnp.int32)
    e_sorted = jnp.take_along_axis(e, order, axis=1)
    g_sorted = jnp.take_along_axis(g, order, axis=1)
    tok = order // k
    offs = jax.vmap(lambda row: jnp.searchsorted(row, jnp.arange(N_EXPERTS + 1, dtype=e.dtype), side='left'))(e_sorted)
    return tok.reshape(-1), g_sorted.reshape(-1), offs.astype(jnp.int32).reshape(-1)


def _moe_kernel(offs_ref, x_ref, tok_ref, gate_ref, wg_ref, wu_ref, wd_ref, o_ref, g_s, y_s):
    i = pl.program_id(0)
    e = pl.program_id(1)
    r, sl = MOE_ROWS, SUBLANES
    n_slots = tok_ref.shape[0]

    @pl.when(jnp.logical_and(i == 0, e == 0))
    def _():
        g_s[...] = jnp.zeros_like(g_s)

    @pl.when(e == 0)
    def _():
        o_ref[...] = jnp.zeros_like(o_ref)

    start = offs_ref[i * (N_EXPERTS + 1) + e]
    n = offs_ref[i * (N_EXPERTS + 1) + e + 1] - start
    wg = wg_ref[0].astype(BF16)
    wu = wu_ref[0].astype(BF16)
    wd = wd_ref[0].astype(BF16)

    def chunk(c, carry):
        base = start + c * r
        n_valid = jnp.minimum(n - c * r, r)
        n_groups = (n_valid + sl - 1) // sl

        def gather(gi, carry):
            for j in range(sl):
                row = gi * sl + j
                tok = tok_ref[jnp.minimum(base + row, n_slots - 1)]
                g_s[pl.ds(pl.multiple_of(row * sl, sl), sl), :] = x_ref[pl.ds(pl.multiple_of(tok * sl, sl), sl), :]
            return carry

        lax.fori_loop(0, n_groups, gather, 0)
        x = jnp.concatenate([g_s[pl.ds(s, r, stride=sl), :] for s in range(sl)], axis=1).astype(BF16)
        hid = (_silu(_dot(x, wg)) * _dot(x, wu)).astype(BF16)
        y = _dot(hid, wd)
        for s in range(sl):
            y_s[pl.ds(s, r, stride=sl), :] = y[:, s * LANES:(s + 1) * LANES]

        def scatter(gi, carry):
            for j in range(sl):
                row = gi * sl + j
                pos = jnp.minimum(base + row, n_slots - 1)
                tok = tok_ref[pos]
                gt = jnp.where(row < n_valid, gate_ref[pos], 0.0)
                dst = pl.ds(pl.multiple_of(tok * sl, sl), sl)
                o_ref[dst, :] = o_ref[dst, :] + gt * y_s[pl.ds(pl.multiple_of(row * sl, sl), sl), :]
            return carry

        lax.fori_loop(0, n_groups, scatter, 0)
        return carry

    lax.fori_loop(0, (n + r - 1) // r, chunk, 0)


def _moe_routed(x_tiles, tok, gate, offs, w_gate, w_up, w_down, n_tiles):
    ne, d, f = w_gate.shape
    rows = MOE_TILE * SUBLANES
    slots = MOE_TILE * TOP_K
    grid_spec = pltpu.PrefetchScalarGridSpec(
        num_scalar_prefetch=1,
        grid=(n_tiles, ne),
        in_specs=[
            pl.BlockSpec((rows, LANES), lambda i, e, offs: (i, 0), pipeline_mode=pl.Buffered(1)),
            pl.BlockSpec((slots,), lambda i, e, offs: (i,), memory_space=pltpu.SMEM),
            pl.BlockSpec((slots,), lambda i, e, offs: (i,), memory_space=pltpu.SMEM),
            pl.BlockSpec((1, d, f), lambda i, e, offs: (e, 0, 0)),
            pl.BlockSpec((1, d, f), lambda i, e, offs: (e, 0, 0)),
            pl.BlockSpec((1, f, d), lambda i, e, offs: (e, 0, 0)),
        ],
        out_specs=pl.BlockSpec((rows, LANES), lambda i, e, offs: (i, 0), pipeline_mode=pl.Buffered(1)),
        scratch_shapes=[pltpu.VMEM((MOE_ROWS * SUBLANES, LANES), F32), pltpu.VMEM((MOE_ROWS * SUBLANES, LANES), F32)],
    )
    return pl.pallas_call(
        _moe_kernel,
        grid_spec=grid_spec,
        out_shape=jax.ShapeDtypeStruct((n_tiles * rows, LANES), F32),
        compiler_params=_cparams("arbitrary", "arbitrary"),
        name="moe_routed",
    )(offs, x_tiles, tok, gate, w_gate, w_up, w_down)


def _shared_kernel(final, h_ref, routed_ref, x1_ref, mod_ref, wg_ref, wu_ref, wd_ref, fg_ref, o_ref):
    h = h_ref[...].astype(BF16)
    hid = (_silu(_dot(h, wg_ref[...])) * _dot(h, wu_ref[...])).astype(BF16)
    x2 = x1_ref[...] + mod_ref[0, 5:6, :] * (routed_ref[...] + _dot(hid, wd_ref[...]))
    if final:
        x2 = x2 * lax.rsqrt(jnp.mean(x2 * x2, axis=-1, keepdims=True) + RMS_EPS) * fg_ref[...]
    o_ref[...] = x2


def _shared_residual(h2, routed, x1, mod, ws_gate, ws_up, ws_down, final_gain, final, n_lat, seq, tm=512):
    t, d = x1.shape
    n_lat_tiles = n_lat // tm
    per_batch = seq // tm
    ctx_row = mod.shape[0] - 1

    def mod_map(i):
        return (jnp.where(i < n_lat_tiles, i // per_batch, ctx_row), 0, 0)

    const = lambda i: (0, 0)
    row = lambda i: (i, 0)
    full = lambda a: pl.BlockSpec(a.shape, const)
    return pl.pallas_call(
        functools.partial(_shared_kernel, final),
        grid=(t // tm,),
        in_specs=[pl.BlockSpec((tm, d), row), pl.BlockSpec((tm, d), row), pl.BlockSpec((tm, d), row),
                  pl.BlockSpec((1, 6, d), mod_map), full(ws_gate), full(ws_up), full(ws_down),
                  pl.BlockSpec((1, d), const)],
        out_specs=pl.BlockSpec((tm, d), row),
        out_shape=jax.ShapeDtypeStruct((t, d), F32),
        compiler_params=_cparams("parallel"),
        name="shared_residual",
    )(h2, routed, x1, mod, ws_gate, ws_up, ws_down, final_gain.reshape(1, d))


MOD_ROWS = 16


def kernel(x, c, ctx, c_ctx, w_ada, b_ada, norm1_gain, norm2_gain, w_in, na_rpb, hg_lower_bounds, hg_norm_gain,
           w_fn_branch, w_na_branch, w_hg_branch, w_out, w_router, b_router, w_exp_gate, w_exp_up, w_exp_down,
           w_sh_gate, w_sh_up, w_sh_down, final_gain):
    b, n, d = x.shape
    nc = ctx.shape[1]
    depth = w_ada.shape[0]
    n_lat, n_ctx = b * n, b * nc

    sm = jax.nn.softmax(hg_lower_bounds.astype(F32), axis=0)
    cs = jnp.cumsum(sm, axis=0)
    lower = cs - cs[0]

    cin = jnp.concatenate([c, c_ctx[None, :], jnp.zeros((MOD_ROWS - b - 1, d), F32)], axis=0)
    mod = _ada_mod(cin, w_ada, b_ada)[:, :b + 1].reshape(depth, b + 1, 6, d)
    hg_consts = _hgrn_consts()
    zero_state = jnp.zeros((b, 2, HG_WIDTH, HG_WIDTH), F32)

    x_all = jnp.concatenate([x.reshape(n_lat, d), ctx.reshape(n_ctx, d)], axis=0)
    for l in range(depth):
        last = l == depth - 1
        w = w_in[l].astype(BF16)
        c0, c1, c2 = FN_WIDTH, FN_WIDTH + 3 * NA_WIDTH, FN_WIDTH + 3 * NA_WIDTH + 5 * HG_WIDTH
        z_fn, z_qkv, z_hg = _proj_in(x_all, mod[l], norm1_gain[l], w[:, :c0], w[:, c0:c1], w[:, c1:c2], n_lat, n)

        lbp = _hgrn_lb_params(lower[l])
        of_c, ob_c, s_ctx = _hgrn(z_hg, zero_state, lbp, hg_consts, b, nc, n_lat)
        of_l, ob_l, _ = _hgrn(z_hg, s_ctx, lbp, hg_consts, b, n, 0)
        fn_l = _fourier_lat(z_fn, b, n)
        na_l = _na_lat(z_qkv, _na_bias_table(na_rpb[l]), b, n, nc)
        if last:
            n_rows = n_lat
            fn_a, na_a, of_a, ob_a = fn_l, na_l, of_l, ob_l
        else:
            n_rows = n_lat + n_ctx
            cat = lambda p, q: jnp.concatenate([p, q], axis=0)
            fn_a = jnp.concatenate([fn_l, _fourier_ctx(z_fn, b, nc, n_lat)], axis=1)
            na_a = cat(na_l, _na_ctx(z_qkv, b, n, nc))
            of_a, ob_a = cat(of_l, of_c), cat(ob_l, ob_c)
        x1, h2 = _merge(x_all, mod[l], norm1_gain[l], norm2_gain[l], fn_a, na_a, of_a, ob_a, z_hg, hg_norm_gain[l],
                        w_fn_branch[l].astype(BF16), w_na_branch[l].astype(BF16), w_hg_branch[l].astype(BF16),
                        w[:, c2:], w_out[l].astype(BF16), n_rows, n_lat, n)

        idx, gate = _router(h2, w_router[l], b_router[l])
        n_tiles = -(-n_rows // MOE_TILE)
        tok, gate_sorted, offs = _moe_dispatch(idx[:, :TOP_K], gate[:, :TOP_K], n_tiles)
        x_tiles = jnp.pad(h2, ((0, n_tiles * MOE_TILE - n_rows), (0, 0))).reshape(-1, LANES)
        routed = _moe_routed(x_tiles, tok, gate_sorted, offs, w_exp_gate[l], w_exp_up[l], w_exp_down[l], n_tiles)
        x_all = _shared_residual(h2, routed.reshape(-1, d), x1, mod[l], w_sh_gate[l].astype(BF16),
                                 w_sh_up[l].astype(BF16), w_sh_down[l].astype(BF16), final_gain, last, n_lat, n)
    return x_all.reshape(b, n, d)
```

```python
import functools

import numpy as np
import jax
import jax.numpy as jnp
from jax import lax
from jax.experimental import pallas as pl
from jax.experimental.pallas import tpu as pltpu

D_MODEL = 1024
GRID_W = 64
RMS_EPS = 1e-6
FN_GROUPS = 4
FN_GROUP_DIM = 64
FN_WIDTH = 256
NA_HEADS = 8
NA_HEAD_DIM = 64
NA_WIDTH = 512
NA_WIN_ROWS = 8
NA_WIN_COLS = 16
HG_HEADS = 4
HG_HEAD_DIM = 64
HG_WIDTH = 256
N_EXPERTS = 256
TOP_K = 8
EXPERT_DIM = 256
ROUTE_SCALE = 2.5

LANES = 128
SUBLANES = 8

F32 = jnp.float32
BF16 = jnp.bfloat16
VMEM_LIMIT = 56 * 1024 * 1024


def _cparams(*sem):
    return pltpu.CompilerParams(dimension_semantics=sem, vmem_limit_bytes=VMEM_LIMIT)


def _dot(a, b):
    return jnp.dot(a, b, preferred_element_type=F32)


def _dot_nt(a, b):
    return lax.dot_general(a, b, (((1,), (1,)), ((), ())), preferred_element_type=F32)


def _sigmoid(z):
    e = jnp.exp(-jnp.abs(z))
    r = 1.0 / (1.0 + e)
    return jnp.where(z >= 0, r, e * r)


def _silu(z):
    return z * _sigmoid(z)


def _norm_mod(x, gain, shift, scale):
    y = x * lax.rsqrt(jnp.mean(x * x, axis=-1, keepdims=True) + RMS_EPS)
    return (y * gain) * (1.0 + scale) + shift


def _ada_kernel(c_ref, w_ref, b_ref, o_ref):
    s = _silu(c_ref[...]).astype(BF16)
    o_ref[0] = _dot(s, w_ref[0].astype(BF16)) + b_ref[0]


def _ada_mod(cin, w_ada, b_ada):
    depth, d, n = w_ada.shape
    r = cin.shape[0]
    tn = 1536
    return pl.pallas_call(
        _ada_kernel,
        grid=(depth, n // tn),
        in_specs=[
            pl.BlockSpec((r, d), lambda l, j: (0, 0)),
            pl.BlockSpec((1, d, tn), lambda l, j: (l, 0, j)),
            pl.BlockSpec((1, 1, tn), lambda l, j: (l, 0, j)),
        ],
        out_specs=pl.BlockSpec((1, r, tn), lambda l, j: (l, 0, j)),
        out_shape=jax.ShapeDtypeStruct((depth, r, n), F32),
        compiler_params=_cparams("parallel", "parallel"),
        name="ada_mod",
    )(cin, w_ada, b_ada.reshape(depth, 1, n))


def _proj_in_kernel(x_ref, mod_ref, gain_ref, wfn_ref, wqkv_ref, whg_ref, fn_ref, qkv_ref, hg_ref):
    h = _norm_mod(x_ref[...], gain_ref[...], mod_ref[0, 0:1, :], mod_ref[0, 1:2, :]).astype(BF16)
    fn_ref[...] = _dot(h, wfn_ref[...])
    qkv_ref[...] = _dot(h, wqkv_ref[...]).astype(BF16)
    hg_ref[...] = _dot(h, whg_ref[...])


def _proj_in(x_all, mod, gain, w_fn, w_qkv, w_hg, n_lat, seq, tm=512):
    t, d = x_all.shape
    n_lat_tiles = n_lat // tm
    per_batch = seq // tm
    ctx_row = mod.shape[0] - 1

    def mod_map(i):
        return (jnp.where(i < n_lat_tiles, i // per_batch, ctx_row), 0, 0)

    const = lambda i: (0, 0)
    row = lambda i: (i, 0)
    return pl.pallas_call(
        _proj_in_kernel,
        grid=(t // tm,),
        in_specs=[
            pl.BlockSpec((tm, d), row),
            pl.BlockSpec((1, 6, d), mod_map),
            pl.BlockSpec((1, d), const),
            pl.BlockSpec(w_fn.shape, const),
            pl.BlockSpec(w_qkv.shape, const),
            pl.BlockSpec(w_hg.shape, const),
        ],
        out_specs=[
            pl.BlockSpec((tm, w_fn.shape[1]), row),
            pl.BlockSpec((tm, w_qkv.shape[1]), row),
            pl.BlockSpec((tm, w_hg.shape[1]), row),
        ],
        out_shape=[
            jax.ShapeDtypeStruct((t, w_fn.shape[1]), F32),
            jax.ShapeDtypeStruct((t, w_qkv.shape[1]), BF16),
            jax.ShapeDtypeStruct((t, w_hg.shape[1]), F32),
        ],
        compiler_params=_cparams("parallel"),
        name="proj_in",
    )(x_all, mod, gain.reshape(1, d), w_fn, w_qkv, w_hg)


def _dft_cos_sin(n):
    k = np.arange(n)
    ang = 2.0 * np.pi * ((k[:, None] * k[None, :]) % n) / n
    return np.cos(ang), np.sin(ang)


def _channel_dft_mats(scale):
    c, s = _dft_cos_sin(FN_GROUP_DIM)
    eye = np.eye(FN_GROUPS)
    return (jnp.asarray(np.kron(eye, c) * scale, BF16), jnp.asarray(np.kron(eye, -s) * scale, BF16))


FFT_R = 64


def _store_halves(ref, lead, rows, val):
    ref[lead + (0, rows, slice(None))] = val[:, :LANES]
    ref[lead + (1, rows, slice(None))] = val[:, LANES:]


def _load_halves(ref, lead, rows):
    return jnp.concatenate([ref[lead + (0, rows, slice(None))], ref[lead + (1, rows, slice(None))]], axis=1)


def _fourier_lat_kernel(u_ref, cc_ref, sc_ref, ga_ref, gb_ref, tc_ref, ts_ref, o_ref, u_s, z_s):
    ub = u_ref[...].astype(BF16)
    every = slice(None)
    _store_halves(u_s, (0,), every, _dot(ub, cc_ref[...]))
    _store_halves(u_s, (1,), every, _dot(ub, sc_ref[...]))

    def stage_a(n2, carry):
        rows = pl.ds(n2, FFT_R, stride=FFT_R)
        x = jnp.concatenate([_load_halves(u_s, (0,), rows), _load_halves(u_s, (1,), rows)], axis=0).astype(BF16)
        a = _dot(ga_ref[...], x)
        ar, ai = a[:FFT_R], a[FFT_R:]
        tc = jnp.concatenate([tc_ref[n2], tc_ref[n2]], axis=1)
        ts = jnp.concatenate([ts_ref[n2], ts_ref[n2]], axis=1)
        dst = pl.ds(pl.multiple_of(n2 * FFT_R, FFT_R), FFT_R)
        _store_halves(z_s, (0,), dst, ar * tc + ai * ts)
        _store_halves(z_s, (1,), dst, ai * tc - ar * ts)
        return carry

    lax.fori_loop(0, FFT_R, stage_a, 0)

    def stage_b(k1, carry):
        rows = pl.ds(k1, FFT_R, stride=FFT_R)
        z = jnp.concatenate([_load_halves(z_s, (0,), rows), _load_halves(z_s, (1,), rows)], axis=0).astype(BF16)
        _store_halves(o_ref, (), rows, _dot(gb_ref[...], z))
        return carry

    lax.fori_loop(0, FFT_R, stage_b, 0)


def _fourier_lat(z_fn, n_batch, seq):
    assert seq == FFT_R * FFT_R
    c, s = _dft_cos_sin(FFT_R)
    cc, sc = _channel_dft_mats((seq * FN_GROUP_DIM) ** -0.5)
    ga = jnp.asarray(np.block([[c, s], [-s, c]]), BF16)
    gb = jnp.asarray(np.concatenate([c, s], axis=1), BF16)
    k = np.arange(FFT_R)
    ang = 2.0 * np.pi * (k[:, None] * k[None, :]) / seq
    tc = jnp.asarray(np.broadcast_to(np.cos(ang)[:, :, None], (FFT_R, FFT_R, 128)), F32)
    ts = jnp.asarray(np.broadcast_to(np.sin(ang)[:, :, None], (FFT_R, FFT_R, 128)), F32)
    const2 = lambda b: (0, 0)
    const3 = lambda b: (0, 0, 0)
    return pl.pallas_call(
        _fourier_lat_kernel,
        grid=(n_batch,),
        in_specs=[
            pl.BlockSpec((seq, FN_WIDTH), lambda b: (b, 0)),
            pl.BlockSpec(cc.shape, const2),
            pl.BlockSpec(sc.shape, const2),
            pl.BlockSpec(ga.shape, const2),
            pl.BlockSpec(gb.shape, const2),
            pl.BlockSpec(tc.shape, const3),
            pl.BlockSpec(ts.shape, const3),
        ],
        out_specs=pl.BlockSpec((2, seq, LANES), lambda b: (0, b, 0)),
        out_shape=jax.ShapeDtypeStruct((2, n_batch * seq, LANES), F32),
        scratch_shapes=[pltpu.VMEM((2, 2, seq, LANES), F32), pltpu.VMEM((2, 2, seq, LANES), F32)],
        compiler_params=_cparams("parallel"),
        name="fourier_lat",
    )(z_fn, cc, sc, ga, gb, tc, ts)


def _fourier_ctx_kernel(u_ref, cc_ref, sc_ref, g_ref, o_ref):
    ub = u_ref[...].astype(BF16)
    x = jnp.concatenate([_dot(ub, cc_ref[...]), _dot(ub, sc_ref[...])], axis=0).astype(BF16)
    _store_halves(o_ref, (), slice(None), _dot(g_ref[...], x))


def _fourier_ctx(z_fn, n_batch, n_ctx, row0):
    blk0 = row0 // n_ctx
    c, s = _dft_cos_sin(n_ctx)
    cc, sc = _channel_dft_mats((n_ctx * FN_GROUP_DIM) ** -0.5)
    g = jnp.asarray(np.concatenate([c, s], axis=1), BF16)
    const2 = lambda b: (0, 0)
    return pl.pallas_call(
        _fourier_ctx_kernel,
        grid=(n_batch,),
        in_specs=[
            pl.BlockSpec((n_ctx, FN_WIDTH), lambda b: (blk0 + b, 0)),
            pl.BlockSpec(cc.shape, const2),
            pl.BlockSpec(sc.shape, const2),
            pl.BlockSpec(g.shape, const2),
        ],
        out_specs=pl.BlockSpec((2, n_ctx, LANES), lambda b: (0, b, 0)),
        out_shape=jax.ShapeDtypeStruct((2, n_batch * n_ctx, LANES), F32),
        compiler_params=_cparams("parallel"),
        name="fourier_ctx",
    )(z_fn, cc, sc, g)


NEG_BIG = -1e30
HEAD_PAIR = 2 * NA_HEAD_DIM


def _na_bias_table(rpb):
    wr, wc, w = NA_WIN_ROWS, NA_WIN_COLS, GRID_W
    col = np.arange(w)
    col_start = np.clip(col - wc // 2, 0, w - wc)
    in_win = (col[None, :] >= col_start[:, None]) & (col[None, :] < col_start[:, None] + wc)
    dc_idx = np.clip(col[None, :] - col[:, None], 1 - wc, wc - 1) + (wc - 1)
    dr_idx = np.arange(wr)[:, None] + np.arange(wr)[None, :]
    b = rpb.astype(F32)[:, dr_idx[:, :, None, None], dc_idx[None, None, :, :]]
    b = jnp.where(in_win[None, None, None], b, NEG_BIG)
    return b.transpose(0, 1, 3, 2, 4).reshape(rpb.shape[0], wr, w, wr * w)


def _softmax_pv(q, keys, vals, biases):
    s = []
    for k, b in zip(keys, biases):
        si = _dot_nt(q, k)
        s.append(si if b is None else si + b)
    m = s[0].max(axis=-1, keepdims=True)
    for si in s[1:]:
        m = jnp.maximum(m, si.max(axis=-1, keepdims=True))
    acc = None
    l = None
    for si, v in zip(s, vals):
        p = jnp.exp(si - m)
        li = p.sum(axis=-1, keepdims=True)
        oi = _dot(p.astype(BF16), v)
        acc = oi if acc is None else acc + oi
        l = li if l is None else l + li
    return acc / l


def _na_lat_kernel(q_ref, k_ref, v_ref, kc_ref, vc_ref, bias_ref, o_ref):
    w = GRID_W
    n_loc = NA_WIN_ROWS * w
    rows = q_ref.shape[0] // w
    lane = lax.broadcasted_iota(jnp.int32, (w, HEAD_PAIR), 1)
    first = lane < NA_HEAD_DIM
    kc = kc_ref[...]
    vc = vc_ref[...]

    def body(r, carry):
        kr0 = jnp.clip(r - NA_WIN_ROWS // 2, 0, rows - NA_WIN_ROWS)
        s = kr0 - r + (NA_WIN_ROWS - 1)
        q = q_ref[pl.ds(pl.multiple_of(r * w, w), w), :] * jnp.asarray(NA_HEAD_DIM ** -0.5, BF16)
        ks = k_ref[pl.ds(pl.multiple_of(kr0 * w, w), n_loc), :]
        vs = v_ref[pl.ds(pl.multiple_of(kr0 * w, w), n_loc), :]
        zero = jnp.zeros_like(q)
        q2 = jnp.concatenate([jnp.where(first, q, zero), jnp.where(first, zero, q)], axis=0)
        bias = jnp.concatenate([bias_ref[0, s], bias_ref[1, s]], axis=0)
        o2 = _softmax_pv(q2, [ks, kc], [vs, vc], [bias, None])
        o_ref[pl.ds(pl.multiple_of(r * w, w), w), :] = jnp.where(first, o2[:w], o2[w:]).astype(o_ref.dtype)
        return carry

    lax.fori_loop(0, rows, body, 0, unroll=2)


def _na_lat(qkv, bias, n_batch, seq, n_ctx):
    n_pairs = NA_WIDTH // HEAD_PAIR
    ctx0 = n_batch * seq // n_ctx
    wr, w = NA_WIN_ROWS, GRID_W
    return pl.pallas_call(
        _na_lat_kernel,
        grid=(n_batch, n_pairs),
        in_specs=[
            pl.BlockSpec((seq, HEAD_PAIR), lambda b, p: (b, p)),
            pl.BlockSpec((seq, HEAD_PAIR), lambda b, p: (b, n_pairs + p)),
            pl.BlockSpec((seq, HEAD_PAIR), lambda b, p: (b, 2 * n_pairs + p)),
            pl.BlockSpec((n_ctx, HEAD_PAIR), lambda b, p: (ctx0 + b, n_pairs + p)),
            pl.BlockSpec((n_ctx, HEAD_PAIR), lambda b, p: (ctx0 + b, 2 * n_pairs + p)),
            pl.BlockSpec((2, wr, w, wr * w), lambda b, p: (p, 0, 0, 0)),
        ],
        out_specs=pl.BlockSpec((seq, HEAD_PAIR), lambda b, p: (b, p)),
        out_shape=jax.ShapeDtypeStruct((n_batch * seq, NA_WIDTH), BF16),
        compiler_params=_cparams("parallel", "parallel"),
        name="na_lat",
    )(qkv, qkv, qkv, qkv, qkv, bias)


def _na_ctx_kernel(q_ref, k_ref, v_ref, o_ref):
    lane = lax.broadcasted_iota(jnp.int32, q_ref.shape, 1)
    first = lane < NA_HEAD_DIM
    q = q_ref[...] * jnp.asarray(NA_HEAD_DIM ** -0.5, BF16)
    k = k_ref[...]
    v = v_ref[...]
    zero = jnp.zeros_like(q)
    o0 = _softmax_pv(jnp.where(first, q, zero), [k], [v], [None])
    o1 = _softmax_pv(jnp.where(first, zero, q), [k], [v], [None])
    o_ref[...] = jnp.where(first, o0, o1).astype(o_ref.dtype)


def _na_ctx(qkv, n_batch, seq, n_ctx):
    n_pairs = NA_WIDTH // HEAD_PAIR
    ctx0 = n_batch * seq // n_ctx
    return pl.pallas_call(
        _na_ctx_kernel,
        grid=(n_batch, n_pairs),
        in_specs=[
            pl.BlockSpec((n_ctx, HEAD_PAIR), lambda b, p: (ctx0 + b, p)),
            pl.BlockSpec((n_ctx, HEAD_PAIR), lambda b, p: (ctx0 + b, n_pairs + p)),
            pl.BlockSpec((n_ctx, HEAD_PAIR), lambda b, p: (ctx0 + b, 2 * n_pairs + p)),
        ],
        out_specs=pl.BlockSpec((n_ctx, HEAD_PAIR), lambda b, p: (b, p)),
        out_shape=jax.ShapeDtypeStruct((n_batch * n_ctx, NA_WIDTH), BF16),
        compiler_params=_cparams("parallel", "parallel"),
        name="na_ctx",
    )(qkv, qkv, qkv)


HG_CHUNK = 64
HG_LEVELS = (32, 16)
HG_DIAG = 16
HG_BATCH = 2


def _hgrn_tables():
    c = HG_CHUNK
    tri, masks, refs, last = [], [], [], []
    for reverse in (False, True):
        u = np.arange(c)[::-1] if reverse else np.arange(c)
        row_of = {int(uu): t for t, uu in enumerate(u)}
        ut, uj = u[:, None], u[None, :]
        tri.append(uj <= ut)
        lv, rf = [], []
        for h in HG_LEVELS:
            same = (ut // (2 * h)) == (uj // (2 * h))
            up_t, up_j = (ut % (2 * h)) >= h, (uj % (2 * h)) >= h
            lv.append(same & up_t & ~up_j)
            rf.append([row_of[int(uu) // (2 * h) * (2 * h) + h - 1] for uu in u])
        same_d = (ut // HG_DIAG) == (uj // HG_DIAG)
        lv.append(same_d & (uj <= ut))
        rf.append([row_of.get(int(uu) // HG_DIAG * HG_DIAG - 1, -1) for uu in u])
        masks.append(np.stack([np.tile(x, (1, HG_HEADS)) for x in lv]).astype(np.float32))
        refs.append(rf)
        last.append(row_of[c - 1])
    hm = (np.arange(HG_WIDTH)[:, None] // HG_HEAD_DIM) == (np.arange(HG_WIDTH)[None, :] // HG_HEAD_DIM)
    arrays = (jnp.asarray(np.stack(tri), BF16), jnp.asarray(np.stack(masks), F32), jnp.asarray(hm, F32))
    return arrays, refs, last


def _ref_rows(b, ref):
    pieces, t = [], 0
    while t < len(ref):
        t1 = t
        while t1 < len(ref) and ref[t1] == ref[t]:
            t1 += 1
        shape = (t1 - t, b.shape[1])
        pieces.append(jnp.zeros(shape, b.dtype) if ref[t] < 0 else jnp.broadcast_to(b[ref[t]:ref[t] + 1, :], shape))
        t = t1
    return jnp.concatenate(pieces, axis=0)


def _split3(x):
    hi = x.astype(BF16)
    r = x - hi.astype(F32)
    mid = r.astype(BF16)
    lo = (r - mid.astype(F32)).astype(BF16)
    return hi, mid, lo


def _hgrn_chunk(zq, zv, zf, lbp, tri, masks, hm, st, refs, last_row):
    q = _silu(zq)
    e = jnp.exp(-jnp.abs(zf))
    inv = 1.0 / (1.0 + e)
    log_sig = jnp.minimum(zf, 0.0) - jnp.log(1.0 + e)
    sig_neg = jnp.where(zf >= 0, e * inv, inv)
    a = lbp[0:1, :]
    cc = lbp[1:2, :] + log_sig
    log_f = jnp.maximum(a, cc) + jnp.log(1.0 + jnp.exp(-jnp.abs(a - cc)))
    k = lbp[2:3, :] * sig_neg

    hi, mid, lo = _split3(log_f)
    b = _dot(tri, hi) + _dot(tri, mid) + _dot(tri, lo)
    b_last = b[last_row:last_row + 1]

    def heads_bd(x):
        return (jnp.concatenate([x] * HG_HEADS, axis=0) * hm).astype(BF16)

    a_all = None
    for i, ref in enumerate(refs):
        rel = b - _ref_rows(b, ref)
        if i < len(HG_LEVELS):
            qx, kx = q * jnp.exp(jnp.minimum(rel, 0.0)), k * jnp.exp(jnp.minimum(-rel, 0.0))
        else:
            qx, kx = q * jnp.exp(rel), k * jnp.exp(-rel)
        ai = _dot_nt(qx.astype(BF16), heads_bd(kx)) * masks[i]
        a_all = ai if a_all is None else a_all + ai
    o = _dot(a_all.astype(BF16), heads_bd(zv)) + _dot_nt((q * jnp.exp(b)).astype(BF16), st.astype(BF16))
    kl = (k * jnp.exp(b_last - b)).astype(BF16)
    upd = lax.dot_general(zv.astype(BF16), kl, (((0,), (0,)), ((), ())), preferred_element_type=F32)
    st_new = st * jnp.exp(b_last) + upd * hm
    return o, st_new


def _hgrn_kernel(refs, last, *args):
    nb = HG_BATCH
    z_refs = args[:3 * nb]
    s0_ref, lbp_ref, tri_ref, mask_ref, hm_ref = args[3 * nb:3 * nb + 5]
    of_ref, ob_ref, sfin_ref, st = args[3 * nb + 5:]
    i = pl.program_id(1)

    @pl.when(i == 0)
    def _():
        st[...] = s0_ref[...]

    w = HG_WIDTH
    hm = hm_ref[...]
    for bb in range(nb):
        zf_ref, zb_ref, zbg_ref = z_refs[3 * bb:3 * bb + 3]
        o, s = _hgrn_chunk(zf_ref[:, 0:w], zf_ref[:, w:2 * w], zf_ref[:, 2 * w:3 * w], lbp_ref[0],
                           tri_ref[0], mask_ref[0], hm, st[bb, 0], refs[0], last[0])
        of_ref[bb] = o
        st[bb, 0] = s
        o, s = _hgrn_chunk(zb_ref[:, 0:w], zb_ref[:, w:2 * w], zbg_ref[...], lbp_ref[1],
                           tri_ref[1], mask_ref[1], hm, st[bb, 1], refs[1], last[1])
        ob_ref[bb] = o
        st[bb, 1] = s

    @pl.when(i == pl.num_programs(1) - 1)
    def _():
        sfin_ref[...] = st[...]


def _hgrn(z_hg, s0, lbp, tables, n_batch, n_tok, row0):
    (tri, masks, hm), refs, last = tables
    c, w, nb = HG_CHUNK, HG_WIDTH, HG_BATCH
    nch = n_tok // c
    base = row0 // c
    const3 = lambda b, i: (0, 0, 0)
    in_specs = []
    for bb in range(nb):
        fwd = lambda b, i, bb=bb: (base + (b * nb + bb) * nch + i, 0)
        bwd = lambda b, i, bb=bb: (base + (b * nb + bb) * nch + (nch - 1 - i), 0)
        bwd_gate = lambda b, i, bb=bb: (base + (b * nb + bb) * nch + (nch - 1 - i), 3)
        in_specs += [pl.BlockSpec((c, 3 * w), fwd), pl.BlockSpec((c, 2 * w), bwd), pl.BlockSpec((c, w), bwd_gate)]
    in_specs += [
        pl.BlockSpec((nb, 2, w, w), lambda b, i: (b, 0, 0, 0)),
        pl.BlockSpec(lbp.shape, const3),
        pl.BlockSpec(tri.shape, const3),
        pl.BlockSpec(masks.shape, lambda b, i: (0, 0, 0, 0)),
        pl.BlockSpec(hm.shape, lambda b, i: (0, 0)),
    ]
    o_f, o_b, s_fin = pl.pallas_call(
        functools.partial(_hgrn_kernel, refs, last),
        grid=(n_batch // nb, nch),
        in_specs=in_specs,
        out_specs=[
            pl.BlockSpec((nb, c, w), lambda b, i: (b, i, 0)),
            pl.BlockSpec((nb, c, w), lambda b, i: (b, nch - 1 - i, 0)),
            pl.BlockSpec((nb, 2, w, w), lambda b, i: (b, 0, 0, 0)),
        ],
        out_shape=[
            jax.ShapeDtypeStruct((n_batch, n_tok, w), F32),
            jax.ShapeDtypeStruct((n_batch, n_tok, w), F32),
            jax.ShapeDtypeStruct((n_batch, 2, w, w), F32),
        ],
        scratch_shapes=[pltpu.VMEM((nb, 2, w, w), F32)],
        compiler_params=_cparams("parallel", "arbitrary"),
        name="hgrn",
    )(*([z_hg] * (3 * nb)), s0, lbp, tri, masks, hm)
    return o_f.reshape(n_batch * n_tok, w), o_b.reshape(n_batch * n_tok, w), s_fin


def _hgrn_lb_params(lower_l):
    rows = jnp.stack([jnp.log(lower_l), jnp.log1p(-lower_l), 1.0 - lower_l], axis=1)
    return jnp.pad(rows, ((0, 0), (0, 5), (0, 0)))


def _merge_kernel(x_ref, mod_ref, g1_ref, g2_ref, fn_ref, na_ref, of_ref, ob_ref, hgz_ref, hgain_ref,
                  hmean_ref, wfn_ref, wna_ref, whg_ref, wbg_ref, wout_ref, x1_ref, h2_ref, h2t_ref):
    d = x_ref.shape[1]
    x = x_ref[...]
    h = _norm_mod(x, g1_ref[...], mod_ref[0, 0:1, :], mod_ref[0, 1:2, :]).astype(BF16)
    o = of_ref[...] + ob_ref[...]
    hi, mid, lo = _split3(o * o)
    ms = _dot(hi, hmean_ref[...]) + _dot(mid, hmean_ref[...]) + _dot(lo, hmean_ref[...])
    y_hg_in = o * lax.rsqrt(ms + RMS_EPS) * hgain_ref[...] * _silu(hgz_ref[...])
    y_fn = _dot(jnp.concatenate([fn_ref[0], fn_ref[1]], axis=1).astype(BF16), wfn_ref[...])
    y_na = _dot(na_ref[...], wna_ref[...])
    y_hg = _dot(y_hg_in.astype(BF16), whg_ref[...])
    m = (_sigmoid(_dot(h, wbg_ref[:, 0:d])) * y_fn
         + _sigmoid(_dot(h, wbg_ref[:, d:2 * d])) * y_na
         + _sigmoid(_dot(h, wbg_ref[:, 2 * d:3 * d])) * y_hg)
    x1 = x + mod_ref[0, 2:3, :] * _dot(m.astype(BF16), wout_ref[...])
    x1_ref[...] = x1
    h2 = _norm_mod(x1, g2_ref[...], mod_ref[0, 3:4, :], mod_ref[0, 4:5, :])
    h2_ref[...] = h2
    tm = x.shape[0]
    for s in range(SUBLANES):
        h2t_ref[pl.ds(s, tm, stride=SUBLANES), :] = h2[:, s * LANES:(s + 1) * LANES]


def _merge(x_all, mod, gain1, gain2, fn, na, o_f, o_b, z_hg, hgain, w_fnb, w_nab, w_hgb, w_bg, w_out,
           n_rows, n_lat, seq, tm=256):
    t, d = n_rows, x_all.shape[1]
    n_lat_tiles = n_lat // tm
    per_batch = seq // tm
    ctx_row = mod.shape[0] - 1
    w = HG_WIDTH
    hmean = jnp.asarray(np.kron(np.eye(HG_HEADS), np.full((HG_HEAD_DIM, HG_HEAD_DIM), 1.0 / HG_HEAD_DIM)), BF16)

    def mod_map(i):
        return (jnp.where(i < n_lat_tiles, i // per_batch, ctx_row), 0, 0)

    const = lambda i: (0, 0)
    row = lambda i: (i, 0)
    full = lambda a: pl.BlockSpec(a.shape, const)
    return pl.pallas_call(
        _merge_kernel,
        grid=(t // tm,),
        in_specs=[
            pl.BlockSpec((tm, d), row),
            pl.BlockSpec((1, 6, d), mod_map),
            pl.BlockSpec((1, d), const),
            pl.BlockSpec((1, d), const),
            pl.BlockSpec((2, tm, LANES), lambda i: (0, i, 0)),
            pl.BlockSpec((tm, NA_WIDTH), row),
            pl.BlockSpec((tm, w), row),
            pl.BlockSpec((tm, w), row),
            pl.BlockSpec((tm, w), lambda i: (i, 4)),
            pl.BlockSpec((1, w), const),
            full(hmean), full(w_fnb), full(w_nab), full(w_hgb), full(w_bg), full(w_out),
        ],
        out_specs=[pl.BlockSpec((tm, d), row), pl.BlockSpec((tm, d), row), pl.BlockSpec((tm * SUBLANES, LANES), row)],
        out_shape=[jax.ShapeDtypeStruct((t, d), F32), jax.ShapeDtypeStruct((t, d), F32),
                   jax.ShapeDtypeStruct((t * SUBLANES, LANES), F32)],
        compiler_params=_cparams("parallel"),
        name="merge",
    )(x_all, mod, gain1.reshape(1, d), gain2.reshape(1, d), fn, na, o_f, o_b, z_hg,
      jnp.tile(hgain, HG_HEADS).reshape(1, w), hmean, w_fnb, w_nab, w_hgb, w_bg, w_out)


def _router_kernel(h_ref, whi_ref, wlo_ref, b_ref, idx_ref, gate_ref):
    h = h_ref[...]
    hi = h.astype(BF16)
    lo = (h - hi.astype(F32)).astype(BF16)
    logits = _dot(hi, whi_ref[...]) + (_dot(hi, wlo_ref[...]) + _dot(lo, whi_ref[...]))
    scores = _sigmoid(logits)
    sel = scores + b_ref[...]
    tm, ne = sel.shape
    col = lax.broadcasted_iota(jnp.int32, (tm, ne), 1)
    out_lane = lax.broadcasted_iota(jnp.int32, (tm, LANES), 1)
    idx_out = jnp.zeros((tm, LANES), jnp.int32)
    gate_out = jnp.zeros((tm, LANES), F32)
    total = jnp.zeros((tm, 1), F32)
    for k in range(TOP_K):
        m = sel.max(axis=-1, keepdims=True)
        idx = jnp.where(sel == m, col, ne).min(axis=-1, keepdims=True)
        hit = col == idx
        g = jnp.where(hit, scores, 0.0).sum(axis=-1, keepdims=True)
        sel = jnp.where(hit, -jnp.inf, sel)
        idx_out = jnp.where(out_lane == k, idx, idx_out)
        gate_out = jnp.where(out_lane == k, g, gate_out)
        total = total + g
    idx_ref[...] = idx_out
    gate_ref[...] = gate_out * (ROUTE_SCALE / total)


def _router(h2, w_router, b_router, tm=512):
    t, d = h2.shape
    ne = w_router.shape[1]
    w_hi = w_router.astype(BF16)
    w_lo = (w_router - w_hi.astype(F32)).astype(BF16)
    const = lambda i: (0, 0)
    row = lambda i: (i, 0)
    return pl.pallas_call(
        _router_kernel,
        grid=(t // tm,),
        in_specs=[pl.BlockSpec((tm, d), row), pl.BlockSpec((d, ne), const), pl.BlockSpec((d, ne), const),
                  pl.BlockSpec((1, ne), const)],
        out_specs=[pl.BlockSpec((tm, LANES), row), pl.BlockSpec((tm, LANES), row)],
        out_shape=[jax.ShapeDtypeStruct((t, LANES), jnp.int32), jax.ShapeDtypeStruct((t, LANES), F32)],
        compiler_params=_cparams("parallel"),
        name="router",
    )(h2, w_hi, w_lo, b_router.reshape(1, ne).astype(F32))


MOE_TILE = 4096
MOE_ROWS = 160


def _moe_dispatch(idx, gate, n_tiles):
    t, k = idx.shape
    pad = n_tiles * MOE_TILE - t
    slots = MOE_TILE * k
    e = jnp.pad(idx, ((0, pad), (0, 0)), constant_values=N_EXPERTS).reshape(n_tiles, slots)
    g = jnp.pad(gate, ((0, pad), (0, 0))).reshape(n_tiles, slots)
    row8 = jnp.broadcast_to((jnp.arange(slots, dtype=jnp.int32) // k) * SUBLANES, (n_tiles, slots))
    e_sorted, g_sorted, row8_sorted = lax.sort((e, g, row8), dimension=1, num_keys=1)
    offs = jnp.sum(e_sorted[:, None, :] < jnp.arange(N_EXPERTS + 1, dtype=e.dtype)[None, :, None], axis=2)
    offs = jnp.pad(offs.astype(jnp.int32).reshape(-1), (0, SUBLANES))
    return row8_sorted.reshape(-1), g_sorted.reshape(-1), offs


def _moe_kernel(offs_ref, x_ref, row_ref, gate_ref, wg_ref, wu_ref, wd_ref, o_ref, g_a, g_b, y_a, y_b):
    i = pl.program_id(0)
    e = pl.program_id(1)
    ne = pl.num_programs(1)
    r, sl = MOE_ROWS, SUBLANES
    n_slots = row_ref.shape[0]

    def segment(ee):
        start = offs_ref[i * (N_EXPERTS + 1) + ee]
        return start, offs_ref[i * (N_EXPERTS + 1) + ee + 1] - start

    def token_rows(base, row):
        pos = jnp.minimum(base + row, n_slots - 1)
        return pos, pl.ds(pl.multiple_of(row_ref[pos], sl), sl)

    def buf_rows(row):
        return pl.ds(row * sl, sl) if isinstance(row, int) else pl.ds(pl.multiple_of(row * sl, sl), sl)

    def gather(buf, base, rows):
        for row in rows:
            _, src = token_rows(base, row)
            buf[buf_rows(row), :] = x_ref[src, :]

    def experts(buf_in, buf_out):
        x = jnp.concatenate([buf_in[pl.ds(s, r, stride=sl), :] for s in range(sl)], axis=1).astype(BF16)
        hid = (_silu(_dot(x, wg_ref[0])) * _dot(x, wu_ref[0])).astype(BF16)
        y = _dot(hid, wd_ref[0])
        for s in range(sl):
            buf_out[pl.ds(s, r, stride=sl), :] = y[:, s * LANES:(s + 1) * LANES]

    def scatter(buf, base, n_valid, rows):
        for g0 in range(0, len(rows), sl):
            dsts, news = [], []
            for row in rows[g0:g0 + sl]:
                pos, dst = token_rows(base, row)
                gt = jnp.where(row < n_valid, gate_ref[pos], 0.0)
                dsts.append(dst)
                news.append(o_ref[dst, :] + gt * buf[buf_rows(row), :])
            for dst, new in reversed(list(zip(dsts, news))):
                o_ref[dst, :] = new

    all_rows = list(range(r))
    start, n = segment(e)
    start_next, _ = segment(jnp.minimum(e + 1, ne - 1))
    start_prev, n_prev = segment(jnp.maximum(e - 1, 0))
    n_prev = jnp.where(e > 0, jnp.minimum(n_prev, r), 0)

    @pl.when(jnp.logical_and(i == 0, e == 0))
    def _():
        y_b[...] = jnp.zeros_like(y_b)

    @pl.when(e == 0)
    def _():
        o_ref[...] = jnp.zeros_like(o_ref)
        gather(g_a, start, all_rows)

    def overflow_chunk(g_free, y_free, c):
        base = start + c * r
        n_valid = jnp.minimum(n - c * r, r)

        def gather_group(gi, carry):
            gather(g_free, base, [gi * sl + j for j in range(sl)])
            return carry

        def scatter_group(gi, carry):
            scatter(y_free, base, n_valid, [gi * sl + j for j in range(sl)])
            return carry

        lax.fori_loop(0, r // sl, gather_group, 0)
        experts(g_free, y_free)
        lax.fori_loop(0, r // sl, scatter_group, 0)

    def step(g_cur, g_next, y_cur, y_prev):
        experts(g_cur, y_cur)
        gather(g_next, start_next, all_rows)
        scatter(y_prev, start_prev, n_prev, all_rows)

        @pl.when(n > r)
        def _():
            lax.fori_loop(1, (n + r - 1) // r, lambda c, carry: (overflow_chunk(g_cur, y_prev, c), carry)[1], 0)

    parity = e % 2

    @pl.when(parity == 0)
    def _():
        step(g_a, g_b, y_a, y_b)

    @pl.when(parity == 1)
    def _():
        step(g_b, g_a, y_b, y_a)

    @pl.when(e == ne - 1)
    def _():
        scatter(y_b, start, jnp.minimum(n, r), all_rows)


def _moe_routed(x_tiles, tok, gate, offs, w_gate, w_up, w_down, n_tiles):
    ne, d, f = w_gate.shape
    assert ne % 2 == 0
    rows = MOE_TILE * SUBLANES
    slots = MOE_TILE * TOP_K
    grid_spec = pltpu.PrefetchScalarGridSpec(
        num_scalar_prefetch=1,
        grid=(n_tiles, ne),
        in_specs=[
            pl.BlockSpec((rows, LANES), lambda i, e, offs: (i, 0), pipeline_mode=pl.Buffered(1)),
            pl.BlockSpec((slots,), lambda i, e, offs: (i,), memory_space=pltpu.SMEM),
            pl.BlockSpec((slots,), lambda i, e, offs: (i,), memory_space=pltpu.SMEM),
            pl.BlockSpec((1, d, f), lambda i, e, offs: (e, 0, 0)),
            pl.BlockSpec((1, d, f), lambda i, e, offs: (e, 0, 0)),
            pl.BlockSpec((1, f, d), lambda i, e, offs: (e, 0, 0)),
        ],
        out_specs=pl.BlockSpec((rows, LANES), lambda i, e, offs: (i, 0), pipeline_mode=pl.Buffered(1)),
        scratch_shapes=[pltpu.VMEM((MOE_ROWS * SUBLANES, LANES), F32)] * 4,
    )
    return pl.pallas_call(
        _moe_kernel,
        grid_spec=grid_spec,
        out_shape=jax.ShapeDtypeStruct(x_tiles.shape, F32),
        compiler_params=_cparams("arbitrary", "arbitrary"),
        name="moe_routed",
    )(offs, x_tiles, tok, gate, w_gate, w_up, w_down)


def _shared_kernel(final, h_ref, routed_ref, x1_ref, mod_ref, wg_ref, wu_ref, wd_ref, fg_ref, o_ref):
    h = h_ref[...].astype(BF16)
    hid = (_silu(_dot(h, wg_ref[...])) * _dot(h, wu_ref[...])).astype(BF16)
    tm = h.shape[0]
    routed = jnp.concatenate([routed_ref[pl.ds(s, tm, stride=SUBLANES), :] for s in range(SUBLANES)], axis=1)
    x2 = x1_ref[...] + mod_ref[0, 5:6, :] * (routed + _dot(hid, wd_ref[...]))
    if final:
        x2 = x2 * lax.rsqrt(jnp.mean(x2 * x2, axis=-1, keepdims=True) + RMS_EPS) * fg_ref[...]
    o_ref[...] = x2


def _shared_residual(h2, routed, x1, mod, ws_gate, ws_up, ws_down, final_gain, final, n_lat, seq, tm=512):
    t, d = x1.shape
    n_lat_tiles = n_lat // tm
    per_batch = seq // tm
    ctx_row = mod.shape[0] - 1

    def mod_map(i):
        return (jnp.where(i < n_lat_tiles, i // per_batch, ctx_row), 0, 0)

    const = lambda i: (0, 0)
    row = lambda i: (i, 0)
    full = lambda a: pl.BlockSpec(a.shape, const)
    return pl.pallas_call(
        functools.partial(_shared_kernel, final),
        grid=(t // tm,),
        in_specs=[pl.BlockSpec((tm, d), row), pl.BlockSpec((tm * SUBLANES, LANES), row), pl.BlockSpec((tm, d), row),
                  pl.BlockSpec((1, 6, d), mod_map), full(ws_gate), full(ws_up), full(ws_down),
                  pl.BlockSpec((1, d), const)],
        out_specs=pl.BlockSpec((tm, d), row),
        out_shape=jax.ShapeDtypeStruct((t, d), F32),
        compiler_params=_cparams("parallel"),
        name="shared_residual",
    )(h2, routed, x1, mod, ws_gate, ws_up, ws_down, final_gain.reshape(1, d))


MOD_ROWS = 16


def kernel(x, c, ctx, c_ctx, w_ada, b_ada, norm1_gain, norm2_gain, w_in, na_rpb, hg_lower_bounds, hg_norm_gain,
           w_fn_branch, w_na_branch, w_hg_branch, w_out, w_router, b_router, w_exp_gate, w_exp_up, w_exp_down,
           w_sh_gate, w_sh_up, w_sh_down, final_gain):
    b, n, d = x.shape
    nc = ctx.shape[1]
    depth = w_ada.shape[0]
    n_lat, n_ctx = b * n, b * nc

    sm = jax.nn.softmax(hg_lower_bounds.astype(F32), axis=0)
    cs = jnp.cumsum(sm, axis=0)
    lower = cs - cs[0]

    cin = jnp.concatenate([c, c_ctx[None, :], jnp.zeros((MOD_ROWS - b - 1, d), F32)], axis=0)
    mod = _ada_mod(cin, w_ada, b_ada)[:, :b + 1].reshape(depth, b + 1, 6, d)
    hg_consts = _hgrn_tables()
    zero_state = jnp.zeros((b, 2, HG_WIDTH, HG_WIDTH), F32)

    x_all = jnp.concatenate([x.reshape(n_lat, d), ctx.reshape(n_ctx, d)], axis=0)
    for l in range(depth):
        last = l == depth - 1
        w = w_in[l].astype(BF16)
        c0, c1, c2 = FN_WIDTH, FN_WIDTH + 3 * NA_WIDTH, FN_WIDTH + 3 * NA_WIDTH + 5 * HG_WIDTH
        z_fn, z_qkv, z_hg = _proj_in(x_all, mod[l], norm1_gain[l], w[:, :c0], w[:, c0:c1], w[:, c1:c2], n_lat, n)

        lbp = _hgrn_lb_params(lower[l])
        of_c, ob_c, s_ctx = _hgrn(z_hg, zero_state, lbp, hg_consts, b, nc, n_lat)
        of_l, ob_l, _ = _hgrn(z_hg, s_ctx, lbp, hg_consts, b, n, 0)
        fn_l = _fourier_lat(z_fn, b, n)
        na_l = _na_lat(z_qkv, _na_bias_table(na_rpb[l]), b, n, nc)
        if last:
            n_rows = n_lat
            fn_a, na_a, of_a, ob_a = fn_l, na_l, of_l, ob_l
        else:
            n_rows = n_lat + n_ctx
            cat = lambda p, q: jnp.concatenate([p, q], axis=0)
            fn_a = jnp.concatenate([fn_l, _fourier_ctx(z_fn, b, nc, n_lat)], axis=1)
            na_a = cat(na_l, _na_ctx(z_qkv, b, n, nc))
            of_a, ob_a = cat(of_l, of_c), cat(ob_l, ob_c)
        x1, h2, h2_tiles = _merge(x_all, mod[l], norm1_gain[l], norm2_gain[l], fn_a, na_a, of_a, ob_a, z_hg, hg_norm_gain[l],
                        w_fn_branch[l].astype(BF16), w_na_branch[l].astype(BF16), w_hg_branch[l].astype(BF16),
                        w[:, c2:], w_out[l].astype(BF16), n_rows, n_lat, n)

        idx, gate = _router(h2, w_router[l], b_router[l])
        n_tiles = -(-n_rows // MOE_TILE)
        tok, gate_sorted, offs = _moe_dispatch(idx[:, :TOP_K], gate[:, :TOP_K], n_tiles)
        routed = _moe_routed(h2_tiles, tok, gate_sorted, offs, w_exp_gate[l].astype(BF16), w_exp_up[l].astype(BF16),
                             w_exp_down[l].astype(BF16), n_tiles)
        x_all = _shared_residual(h2, routed, x1, mod[l], w_sh_gate[l].astype(BF16),
                                 w_sh_up[l].astype(BF16), w_sh_down[l].astype(BF16), final_gain, last, n_lat, n)
    return x_all.reshape(b, n, d)
```

```python
import functools

import numpy as np
import jax
import jax.numpy as jnp
from jax import lax
from jax.experimental import pallas as pl
from jax.experimental.pallas import tpu as pltpu

D_MODEL = 1024
GRID_W = 64
RMS_EPS = 1e-6
FN_GROUPS = 4
FN_GROUP_DIM = 64
FN_WIDTH = 256
NA_HEADS = 8
NA_HEAD_DIM = 64
NA_WIDTH = 512
NA_WIN_ROWS = 8
NA_WIN_COLS = 16
HG_HEADS = 4
HG_HEAD_DIM = 64
HG_WIDTH = 256
N_EXPERTS = 256
TOP_K = 8
EXPERT_DIM = 256
ROUTE_SCALE = 2.5

LANES = 128
SUBLANES = 8

F32 = jnp.float32
BF16 = jnp.bfloat16
VMEM_LIMIT = 56 * 1024 * 1024


def _cparams(*sem):
    return pltpu.CompilerParams(dimension_semantics=sem, vmem_limit_bytes=VMEM_LIMIT)


def _dot(a, b):
    return jnp.dot(a, b, preferred_element_type=F32)


def _dot_nt(a, b):
    return lax.dot_general(a, b, (((1,), (1,)), ((), ())), preferred_element_type=F32)


def _sigmoid(z):
    e = jnp.exp(-jnp.abs(z))
    r = 1.0 / (1.0 + e)
    return jnp.where(z >= 0, r, e * r)


def _silu(z):
    return z * _sigmoid(z)


def _norm_mod(x, gain, shift, scale):
    y = x * lax.rsqrt(jnp.mean(x * x, axis=-1, keepdims=True) + RMS_EPS)
    return (y * gain) * (1.0 + scale) + shift


def _ada_kernel(c_ref, w_ref, b_ref, o_ref):
    s = _silu(c_ref[...]).astype(BF16)
    o_ref[0] = _dot(s, w_ref[0].astype(BF16)) + b_ref[0]


def _ada_mod(cin, w_ada, b_ada):
    depth, d, n = w_ada.shape
    r = cin.shape[0]
    tn = 1536
    return pl.pallas_call(
        _ada_kernel,
        grid=(depth, n // tn),
        in_specs=[
            pl.BlockSpec((r, d), lambda l, j: (0, 0)),
            pl.BlockSpec((1, d, tn), lambda l, j: (l, 0, j)),
            pl.BlockSpec((1, 1, tn), lambda l, j: (l, 0, j)),
        ],
        out_specs=pl.BlockSpec((1, r, tn), lambda l, j: (l, 0, j)),
        out_shape=jax.ShapeDtypeStruct((depth, r, n), F32),
        compiler_params=_cparams("parallel", "parallel"),
        name="ada_mod",
    )(cin, w_ada, b_ada.reshape(depth, 1, n))


def _proj_in_kernel(x_ref, mod_ref, gain_ref, wfn_ref, wqkv_ref, whg_ref, fn_ref, qkv_ref, hg_ref):
    h = _norm_mod(x_ref[...], gain_ref[...], mod_ref[0, 0:1, :], mod_ref[0, 1:2, :]).astype(BF16)
    fn_ref[...] = _dot(h, wfn_ref[...])
    qkv_ref[...] = _dot(h, wqkv_ref[...]).astype(BF16)
    hg_ref[...] = _dot(h, whg_ref[...])


def _proj_in(x_all, mod, gain, w_fn, w_qkv, w_hg, n_lat, seq, tm=512):
    t, d = x_all.shape
    n_lat_tiles = n_lat // tm
    per_batch = seq // tm
    ctx_row = mod.shape[0] - 1

    def mod_map(i):
        return (jnp.where(i < n_lat_tiles, i // per_batch, ctx_row), 0, 0)

    const = lambda i: (0, 0)
    row = lambda i: (i, 0)
    return pl.pallas_call(
        _proj_in_kernel,
        grid=(t // tm,),
        in_specs=[
            pl.BlockSpec((tm, d), row),
            pl.BlockSpec((1, 6, d), mod_map),
            pl.BlockSpec((1, d), const),
            pl.BlockSpec(w_fn.shape, const),
            pl.BlockSpec(w_qkv.shape, const),
            pl.BlockSpec(w_hg.shape, const),
        ],
        out_specs=[
            pl.BlockSpec((tm, w_fn.shape[1]), row),
            pl.BlockSpec((tm, w_qkv.shape[1]), row),
            pl.BlockSpec((tm, w_hg.shape[1]), row),
        ],
        out_shape=[
            jax.ShapeDtypeStruct((t, w_fn.shape[1]), F32),
            jax.ShapeDtypeStruct((t, w_qkv.shape[1]), BF16),
            jax.ShapeDtypeStruct((t, w_hg.shape[1]), F32),
        ],
        compiler_params=_cparams("parallel"),
        name="proj_in",
    )(x_all, mod, gain.reshape(1, d), w_fn, w_qkv, w_hg)


def _dft_cos_sin(n):
    k = np.arange(n)
    ang = 2.0 * np.pi * ((k[:, None] * k[None, :]) % n) / n
    return np.cos(ang), np.sin(ang)


def _channel_dft_mats(scale):
    c, s = _dft_cos_sin(FN_GROUP_DIM)
    eye = np.eye(FN_GROUPS)
    return (jnp.asarray(np.kron(eye, c) * scale, BF16), jnp.asarray(np.kron(eye, -s) * scale, BF16))


FFT_R = 64


def _store_halves(ref, lead, rows, val):
    ref[lead + (0, rows, slice(None))] = val[:, :LANES]
    ref[lead + (1, rows, slice(None))] = val[:, LANES:]


def _load_halves(ref, lead, rows):
    return jnp.concatenate([ref[lead + (0, rows, slice(None))], ref[lead + (1, rows, slice(None))]], axis=1)


def _fourier_lat_kernel(u_ref, cc_ref, sc_ref, ga_ref, gb_ref, tc_ref, ts_ref, o_ref, u_s, z_s):
    ub = u_ref[...].astype(BF16)
    every = slice(None)
    _store_halves(u_s, (0,), every, _dot(ub, cc_ref[...]))
    _store_halves(u_s, (1,), every, _dot(ub, sc_ref[...]))

    def stage_a(n2, carry):
        rows = pl.ds(n2, FFT_R, stride=FFT_R)
        x = jnp.concatenate([_load_halves(u_s, (0,), rows), _load_halves(u_s, (1,), rows)], axis=0).astype(BF16)
        a = _dot(ga_ref[...], x)
        ar, ai = a[:FFT_R], a[FFT_R:]
        tc = jnp.concatenate([tc_ref[n2], tc_ref[n2]], axis=1)
        ts = jnp.concatenate([ts_ref[n2], ts_ref[n2]], axis=1)
        dst = pl.ds(pl.multiple_of(n2 * FFT_R, FFT_R), FFT_R)
        _store_halves(z_s, (0,), dst, ar * tc + ai * ts)
        _store_halves(z_s, (1,), dst, ai * tc - ar * ts)
        return carry

    lax.fori_loop(0, FFT_R, stage_a, 0)

    def stage_b(k1, carry):
        rows = pl.ds(k1, FFT_R, stride=FFT_R)
        z = jnp.concatenate([_load_halves(z_s, (0,), rows), _load_halves(z_s, (1,), rows)], axis=0).astype(BF16)
        _store_halves(o_ref, (), rows, _dot(gb_ref[...], z))
        return carry

    lax.fori_loop(0, FFT_R, stage_b, 0)


def _fourier_lat(z_fn, n_batch, seq):
    assert seq == FFT_R * FFT_R
    c, s = _dft_cos_sin(FFT_R)
    cc, sc = _channel_dft_mats((seq * FN_GROUP_DIM) ** -0.5)
    ga = jnp.asarray(np.block([[c, s], [-s, c]]), BF16)
    gb = jnp.asarray(np.concatenate([c, s], axis=1), BF16)
    k = np.arange(FFT_R)
    ang = 2.0 * np.pi * (k[:, None] * k[None, :]) / seq
    tc = jnp.asarray(np.broadcast_to(np.cos(ang)[:, :, None], (FFT_R, FFT_R, 128)), F32)
    ts = jnp.asarray(np.broadcast_to(np.sin(ang)[:, :, None], (FFT_R, FFT_R, 128)), F32)
    const2 = lambda b: (0, 0)
    const3 = lambda b: (0, 0, 0)
    return pl.pallas_call(
        _fourier_lat_kernel,
        grid=(n_batch,),
        in_specs=[
            pl.BlockSpec((seq, FN_WIDTH), lambda b: (b, 0)),
            pl.BlockSpec(cc.shape, const2),
            pl.BlockSpec(sc.shape, const2),
            pl.BlockSpec(ga.shape, const2),
            pl.BlockSpec(gb.shape, const2),
            pl.BlockSpec(tc.shape, const3),
            pl.BlockSpec(ts.shape, const3),
        ],
        out_specs=pl.BlockSpec((2, seq, LANES), lambda b: (0, b, 0)),
        out_shape=jax.ShapeDtypeStruct((2, n_batch * seq, LANES), F32),
        scratch_shapes=[pltpu.VMEM((2, 2, seq, LANES), F32), pltpu.VMEM((2, 2, seq, LANES), F32)],
        compiler_params=_cparams("parallel"),
        name="fourier_lat",
    )(z_fn, cc, sc, ga, gb, tc, ts)


def _fourier_ctx_kernel(u_ref, cc_ref, sc_ref, g_ref, o_ref):
    ub = u_ref[...].astype(BF16)
    x = jnp.concatenate([_dot(ub, cc_ref[...]), _dot(ub, sc_ref[...])], axis=0).astype(BF16)
    _store_halves(o_ref, (), slice(None), _dot(g_ref[...], x))


def _fourier_ctx(z_fn, n_batch, n_ctx, row0):
    blk0 = row0 // n_ctx
    c, s = _dft_cos_sin(n_ctx)
    cc, sc = _channel_dft_mats((n_ctx * FN_GROUP_DIM) ** -0.5)
    g = jnp.asarray(np.concatenate([c, s], axis=1), BF16)
    const2 = lambda b: (0, 0)
    return pl.pallas_call(
        _fourier_ctx_kernel,
        grid=(n_batch,),
        in_specs=[
            pl.BlockSpec((n_ctx, FN_WIDTH), lambda b: (blk0 + b, 0)),
            pl.BlockSpec(cc.shape, const2),
            pl.BlockSpec(sc.shape, const2),
            pl.BlockSpec(g.shape, const2),
        ],
        out_specs=pl.BlockSpec((2, n_ctx, LANES), lambda b: (0, b, 0)),
        out_shape=jax.ShapeDtypeStruct((2, n_batch * n_ctx, LANES), F32),
        compiler_params=_cparams("parallel"),
        name="fourier_ctx",
    )(z_fn, cc, sc, g)


NEG_BIG = -1e30
HEAD_PAIR = 2 * NA_HEAD_DIM


def _na_bias_table(rpb):
    wr, wc, w = NA_WIN_ROWS, NA_WIN_COLS, GRID_W
    col = np.arange(w)
    col_start = np.clip(col - wc // 2, 0, w - wc)
    in_win = (col[None, :] >= col_start[:, None]) & (col[None, :] < col_start[:, None] + wc)
    dc_idx = np.clip(col[None, :] - col[:, None], 1 - wc, wc - 1) + (wc - 1)
    h, n_dr, n_dc = rpb.shape
    onehot = jnp.asarray(dc_idx.reshape(-1)[None, :] == np.arange(n_dc)[:, None], F32)
    t = jnp.dot(rpb.astype(F32).reshape(h * n_dr, n_dc), onehot, precision=lax.Precision.HIGHEST)
    t = jnp.where(in_win.reshape(-1)[None, :], t, NEG_BIG).reshape(h, n_dr, w, w)
    per_s = [jnp.concatenate([t[:, s + j] for j in range(wr)], axis=-1) for s in range(wr)]
    return jnp.stack(per_s, axis=1)


def _softmax_pv(q, keys, vals, biases):
    s = []
    for k, b in zip(keys, biases):
        si = _dot_nt(q, k)
        s.append(si if b is None else si + b)
    m = s[0].max(axis=-1, keepdims=True)
    for si in s[1:]:
        m = jnp.maximum(m, si.max(axis=-1, keepdims=True))
    acc = None
    l = None
    for si, v in zip(s, vals):
        p = jnp.exp(si - m)
        li = p.sum(axis=-1, keepdims=True)
        oi = _dot(p.astype(BF16), v)
        acc = oi if acc is None else acc + oi
        l = li if l is None else l + li
    return acc / l


def _na_lat_kernel(q_ref, k_ref, v_ref, kc_ref, vc_ref, bias_ref, o_ref):
    w = GRID_W
    n_loc = NA_WIN_ROWS * w
    rows = q_ref.shape[0] // w
    lane = lax.broadcasted_iota(jnp.int32, (w, HEAD_PAIR), 1)
    first = lane < NA_HEAD_DIM
    kc = kc_ref[...]
    vc = vc_ref[...]

    def body(r, carry):
        kr0 = jnp.clip(r - NA_WIN_ROWS // 2, 0, rows - NA_WIN_ROWS)
        s = kr0 - r + (NA_WIN_ROWS - 1)
        q = q_ref[pl.ds(pl.multiple_of(r * w, w), w), :] * jnp.asarray(NA_HEAD_DIM ** -0.5, BF16)
        ks = k_ref[pl.ds(pl.multiple_of(kr0 * w, w), n_loc), :]
        vs = v_ref[pl.ds(pl.multiple_of(kr0 * w, w), n_loc), :]
        zero = jnp.zeros_like(q)
        q2 = jnp.concatenate([jnp.where(first, q, zero), jnp.where(first, zero, q)], axis=0)
        bias = jnp.concatenate([bias_ref[0, s], bias_ref[1, s]], axis=0)
        o2 = _softmax_pv(q2, [ks, kc], [vs, vc], [bias, None])
        o_ref[pl.ds(pl.multiple_of(r * w, w), w), :] = jnp.where(first, o2[:w], o2[w:]).astype(o_ref.dtype)
        return carry

    lax.fori_loop(0, rows, body, 0, unroll=2)


def _na_lat(qkv, bias, n_batch, seq, n_ctx):
    n_pairs = NA_WIDTH // HEAD_PAIR
    ctx0 = n_batch * seq // n_ctx
    wr, w = NA_WIN_ROWS, GRID_W
    return pl.pallas_call(
        _na_lat_kernel,
        grid=(n_batch, n_pairs),
        in_specs=[
            pl.BlockSpec((seq, HEAD_PAIR), lambda b, p: (b, p)),
            pl.BlockSpec((seq, HEAD_PAIR), lambda b, p: (b, n_pairs + p)),
            pl.BlockSpec((seq, HEAD_PAIR), lambda b, p: (b, 2 * n_pairs + p)),
            pl.BlockSpec((n_ctx, HEAD_PAIR), lambda b, p: (ctx0 + b, n_pairs + p)),
            pl.BlockSpec((n_ctx, HEAD_PAIR), lambda b, p: (ctx0 + b, 2 * n_pairs + p)),
            pl.BlockSpec((2, wr, w, wr * w), lambda b, p: (p, 0, 0, 0)),
        ],
        out_specs=pl.BlockSpec((seq, HEAD_PAIR), lambda b, p: (b, p)),
        out_shape=jax.ShapeDtypeStruct((n_batch * seq, NA_WIDTH), BF16),
        compiler_params=_cparams("parallel", "parallel"),
        name="na_lat",
    )(qkv, qkv, qkv, qkv, qkv, bias)


def _na_ctx_kernel(q_ref, k_ref, v_ref, o_ref):
    lane = lax.broadcasted_iota(jnp.int32, q_ref.shape, 1)
    first = lane < NA_HEAD_DIM
    q = q_ref[...] * jnp.asarray(NA_HEAD_DIM ** -0.5, BF16)
    k = k_ref[...]
    v = v_ref[...]
    zero = jnp.zeros_like(q)
    o0 = _softmax_pv(jnp.where(first, q, zero), [k], [v], [None])
    o1 = _softmax_pv(jnp.where(first, zero, q), [k], [v], [None])
    o_ref[...] = jnp.where(first, o0, o1).astype(o_ref.dtype)


def _na_ctx(qkv, n_batch, seq, n_ctx):
    n_pairs = NA_WIDTH // HEAD_PAIR
    ctx0 = n_batch * seq // n_ctx
    return pl.pallas_call(
        _na_ctx_kernel,
        grid=(n_batch, n_pairs),
        in_specs=[
            pl.BlockSpec((n_ctx, HEAD_PAIR), lambda b, p: (ctx0 + b, p)),
            pl.BlockSpec((n_ctx, HEAD_PAIR), lambda b, p: (ctx0 + b, n_pairs + p)),
            pl.BlockSpec((n_ctx, HEAD_PAIR), lambda b, p: (ctx0 + b, 2 * n_pairs + p)),
        ],
        out_specs=pl.BlockSpec((n_ctx, HEAD_PAIR), lambda b, p: (b, p)),
        out_shape=jax.ShapeDtypeStruct((n_batch * n_ctx, NA_WIDTH), BF16),
        compiler_params=_cparams("parallel", "parallel"),
        name="na_ctx",
    )(qkv, qkv, qkv)


HG_CHUNK = 64
HG_LEVELS = (32, 16)
HG_DIAG = 16
HG_BATCH = 2


def _hgrn_tables():
    c = HG_CHUNK
    tri, masks, refs, last = [], [], [], []
    for reverse in (False, True):
        u = np.arange(c)[::-1] if reverse else np.arange(c)
        row_of = {int(uu): t for t, uu in enumerate(u)}
        ut, uj = u[:, None], u[None, :]
        tri.append(uj <= ut)
        lv, rf = [], []
        for h in HG_LEVELS:
            same = (ut // (2 * h)) == (uj // (2 * h))
            up_t, up_j = (ut % (2 * h)) >= h, (uj % (2 * h)) >= h
            lv.append(same & up_t & ~up_j)
            rf.append([row_of[int(uu) // (2 * h) * (2 * h) + h - 1] for uu in u])
        same_d = (ut // HG_DIAG) == (uj // HG_DIAG)
        lv.append(same_d & (uj <= ut))
        rf.append([row_of.get(int(uu) // HG_DIAG * HG_DIAG - 1, -1) for uu in u])
        masks.append(np.stack([np.tile(x, (1, HG_HEADS)) for x in lv]).astype(np.float32))
        refs.append(rf)
        last.append(row_of[c - 1])
    hm = (np.arange(HG_WIDTH)[:, None] // HG_HEAD_DIM) == (np.arange(HG_WIDTH)[None, :] // HG_HEAD_DIM)
    arrays = (jnp.asarray(np.stack(tri), BF16), jnp.asarray(np.stack(masks), F32), jnp.asarray(hm, F32))
    return arrays, refs, last


def _ref_rows(b, ref):
    pieces, t = [], 0
    while t < len(ref):
        t1 = t
        while t1 < len(ref) and ref[t1] == ref[t]:
            t1 += 1
        shape = (t1 - t, b.shape[1])
        pieces.append(jnp.zeros(shape, b.dtype) if ref[t] < 0 else jnp.broadcast_to(b[ref[t]:ref[t] + 1, :], shape))
        t = t1
    return jnp.concatenate(pieces, axis=0)


def _split3(x):
    hi = x.astype(BF16)
    r = x - hi.astype(F32)
    mid = r.astype(BF16)
    lo = (r - mid.astype(F32)).astype(BF16)
    return hi, mid, lo


def _hgrn_chunk(zq, zv, zf, lbp, tri, masks, hm, st, refs, last_row):
    q = _silu(zq)
    e = jnp.exp(-jnp.abs(zf))
    inv = 1.0 / (1.0 + e)
    log_sig = jnp.minimum(zf, 0.0) - jnp.log(1.0 + e)
    sig_neg = jnp.where(zf >= 0, e * inv, inv)
    a = lbp[0:1, :]
    cc = lbp[1:2, :] + log_sig
    log_f = jnp.maximum(a, cc) + jnp.log(1.0 + jnp.exp(-jnp.abs(a - cc)))
    k = lbp[2:3, :] * sig_neg

    hi, mid, lo = _split3(log_f)
    b = _dot(tri, hi) + _dot(tri, mid) + _dot(tri, lo)
    b_last = b[last_row:last_row + 1]

    def heads_bd(x):
        return (jnp.concatenate([x] * HG_HEADS, axis=0) * hm).astype(BF16)

    a_all = None
    for i, ref in enumerate(refs):
        rel = b - _ref_rows(b, ref)
        if i < len(HG_LEVELS):
            qx, kx = q * jnp.exp(jnp.minimum(rel, 0.0)), k * jnp.exp(jnp.minimum(-rel, 0.0))
        else:
            qx, kx = q * jnp.exp(rel), k * jnp.exp(-rel)
        ai = _dot_nt(qx.astype(BF16), heads_bd(kx)) * masks[i]
        a_all = ai if a_all is None else a_all + ai
    o = _dot(a_all.astype(BF16), heads_bd(zv)) + _dot_nt((q * jnp.exp(b)).astype(BF16), st.astype(BF16))
    kl = (k * jnp.exp(b_last - b)).astype(BF16)
    upd = lax.dot_general(zv.astype(BF16), kl, (((0,), (0,)), ((), ())), preferred_element_type=F32)
    st_new = st * jnp.exp(b_last) + upd * hm
    return o, st_new


def _hgrn_kernel(refs, last, *args):
    nb = HG_BATCH
    z_refs = args[:3 * nb]
    s0_ref, lbp_ref, tri_ref, mask_ref, hm_ref = args[3 * nb:3 * nb + 5]
    of_ref, ob_ref, sfin_ref, st = args[3 * nb + 5:]
    i = pl.program_id(1)

    @pl.when(i == 0)
    def _():
        st[...] = s0_ref[...]

    w = HG_WIDTH
    hm = hm_ref[...]
    for bb in range(nb):
        zf_ref, zb_ref, zbg_ref = z_refs[3 * bb:3 * bb + 3]
        o, s = _hgrn_chunk(zf_ref[:, 0:w], zf_ref[:, w:2 * w], zf_ref[:, 2 * w:3 * w], lbp_ref[0],
                           tri_ref[0], mask_ref[0], hm, st[bb, 0], refs[0], last[0])
        of_ref[bb] = o
        st[bb, 0] = s
        o, s = _hgrn_chunk(zb_ref[:, 0:w], zb_ref[:, w:2 * w], zbg_ref[...], lbp_ref[1],
                           tri_ref[1], mask_ref[1], hm, st[bb, 1], refs[1], last[1])
        ob_ref[bb] = o
        st[bb, 1] = s

    @pl.when(i == pl.num_programs(1) - 1)
    def _():
        sfin_ref[...] = st[...]


def _hgrn(z_hg, s0, lbp, tables, n_batch, n_tok, row0):
    (tri, masks, hm), refs, last = tables
    c, w, nb = HG_CHUNK, HG_WIDTH, HG_BATCH
    nch = n_tok // c
    base = row0 // c
    const3 = lambda b, i: (0, 0, 0)
    in_specs = []
    for bb in range(nb):
        fwd = lambda b, i, bb=bb: (base + (b * nb + bb) * nch + i, 0)
        bwd = lambda b, i, bb=bb: (base + (b * nb + bb) * nch + (nch - 1 - i), 0)
        bwd_gate = lambda b, i, bb=bb: (base + (b * nb + bb) * nch + (nch - 1 - i), 3)
        in_specs += [pl.BlockSpec((c, 3 * w), fwd), pl.BlockSpec((c, 2 * w), bwd), pl.BlockSpec((c, w), bwd_gate)]
    in_specs += [
        pl.BlockSpec((nb, 2, w, w), lambda b, i: (b, 0, 0, 0)),
        pl.BlockSpec(lbp.shape, const3),
        pl.BlockSpec(tri.shape, const3),
        pl.BlockSpec(masks.shape, lambda b, i: (0, 0, 0, 0)),
        pl.BlockSpec(hm.shape, lambda b, i: (0, 0)),
    ]
    o_f, o_b, s_fin = pl.pallas_call(
        functools.partial(_hgrn_kernel, refs, last),
        grid=(n_batch // nb, nch),
        in_specs=in_specs,
        out_specs=[
            pl.BlockSpec((nb, c, w), lambda b, i: (b, i, 0)),
            pl.BlockSpec((nb, c, w), lambda b, i: (b, nch - 1 - i, 0)),
            pl.BlockSpec((nb, 2, w, w), lambda b, i: (b, 0, 0, 0)),
        ],
        out_shape=[
            jax.ShapeDtypeStruct((n_batch, n_tok, w), F32),
            jax.ShapeDtypeStruct((n_batch, n_tok, w), F32),
            jax.ShapeDtypeStruct((n_batch, 2, w, w), F32),
        ],
        scratch_shapes=[pltpu.VMEM((nb, 2, w, w), F32)],
        compiler_params=_cparams("parallel", "arbitrary"),
        name="hgrn",
    )(*([z_hg] * (3 * nb)), s0, lbp, tri, masks, hm)
    return o_f.reshape(n_batch * n_tok, w), o_b.reshape(n_batch * n_tok, w), s_fin


def _hgrn_lb_params(lower_l):
    rows = jnp.stack([jnp.log(lower_l), jnp.log1p(-lower_l), 1.0 - lower_l], axis=1)
    return jnp.pad(rows, ((0, 0), (0, 5), (0, 0)))


def _merge_kernel(x_ref, mod_ref, g1_ref, g2_ref, fn_ref, na_ref, of_ref, ob_ref, hgz_ref, hgain_ref,
                  hmean_ref, wfn_ref, wna_ref, whg_ref, wbg_ref, wout_ref, x1_ref, h2_ref, h2t_ref):
    d = x_ref.shape[1]
    x = x_ref[...]
    h = _norm_mod(x, g1_ref[...], mod_ref[0, 0:1, :], mod_ref[0, 1:2, :]).astype(BF16)
    o = of_ref[...] + ob_ref[...]
    hi, mid, lo = _split3(o * o)
    ms = _dot(hi, hmean_ref[...]) + _dot(mid, hmean_ref[...]) + _dot(lo, hmean_ref[...])
    y_hg_in = o * lax.rsqrt(ms + RMS_EPS) * hgain_ref[...] * _silu(hgz_ref[...])
    y_fn = _dot(jnp.concatenate([fn_ref[0], fn_ref[1]], axis=1).astype(BF16), wfn_ref[...])
    y_na = _dot(na_ref[...], wna_ref[...])
    y_hg = _dot(y_hg_in.astype(BF16), whg_ref[...])
    m = (_sigmoid(_dot(h, wbg_ref[:, 0:d])) * y_fn
         + _sigmoid(_dot(h, wbg_ref[:, d:2 * d])) * y_na
         + _sigmoid(_dot(h, wbg_ref[:, 2 * d:3 * d])) * y_hg)
    x1 = x + mod_ref[0, 2:3, :] * _dot(m.astype(BF16), wout_ref[...])
    x1_ref[...] = x1
    h2 = _norm_mod(x1, g2_ref[...], mod_ref[0, 3:4, :], mod_ref[0, 4:5, :])
    h2_ref[...] = h2
    tm = x.shape[0]
    for s in range(SUBLANES):
        h2t_ref[pl.ds(s, tm, stride=SUBLANES), :] = h2[:, s * LANES:(s + 1) * LANES]


def _merge(x_all, mod, gain1, gain2, fn, na, o_f, o_b, z_hg, hgain, w_fnb, w_nab, w_hgb, w_bg, w_out,
           n_rows, n_lat, seq, tm=256):
    t, d = n_rows, x_all.shape[1]
    n_lat_tiles = n_lat // tm
    per_batch = seq // tm
    ctx_row = mod.shape[0] - 1
    w = HG_WIDTH
    hmean = jnp.asarray(np.kron(np.eye(HG_HEADS), np.full((HG_HEAD_DIM, HG_HEAD_DIM), 1.0 / HG_HEAD_DIM)), BF16)

    def mod_map(i):
        return (jnp.where(i < n_lat_tiles, i // per_batch, ctx_row), 0, 0)

    const = lambda i: (0, 0)
    row = lambda i: (i, 0)
    full = lambda a: pl.BlockSpec(a.shape, const)
    return pl.pallas_call(
        _merge_kernel,
        grid=(t // tm,),
        in_specs=[
            pl.BlockSpec((tm, d), row),
            pl.BlockSpec((1, 6, d), mod_map),
            pl.BlockSpec((1, d), const),
            pl.BlockSpec((1, d), const),
            pl.BlockSpec((2, tm, LANES), lambda i: (0, i, 0)),
            pl.BlockSpec((tm, NA_WIDTH), row),
            pl.BlockSpec((tm, w), row),
            pl.BlockSpec((tm, w), row),
            pl.BlockSpec((tm, w), lambda i: (i, 4)),
            pl.BlockSpec((1, w), const),
            full(hmean), full(w_fnb), full(w_nab), full(w_hgb), full(w_bg), full(w_out),
        ],
        out_specs=[pl.BlockSpec((tm, d), row), pl.BlockSpec((tm, d), row), pl.BlockSpec((tm * SUBLANES, LANES), row)],
        out_shape=[jax.ShapeDtypeStruct((t, d), F32), jax.ShapeDtypeStruct((t, d), F32),
                   jax.ShapeDtypeStruct((t * SUBLANES, LANES), F32)],
        compiler_params=_cparams("parallel"),
        name="merge",
    )(x_all, mod, gain1.reshape(1, d), gain2.reshape(1, d), fn, na, o_f, o_b, z_hg,
      jnp.tile(hgain, HG_HEADS).reshape(1, w), hmean, w_fnb, w_nab, w_hgb, w_bg, w_out)


def _router_kernel(h_ref, whi_ref, wlo_ref, b_ref, idx_ref, gate_ref):
    h = h_ref[...]
    hi = h.astype(BF16)
    lo = (h - hi.astype(F32)).astype(BF16)
    logits = _dot(hi, whi_ref[...]) + (_dot(hi, wlo_ref[...]) + _dot(lo, whi_ref[...]))
    scores = _sigmoid(logits)
    sel = scores + b_ref[...]
    tm, ne = sel.shape
    col = lax.broadcasted_iota(jnp.int32, (tm, ne), 1)
    out_lane = lax.broadcasted_iota(jnp.int32, (tm, LANES), 1)
    idx_out = jnp.zeros((tm, LANES), jnp.int32)
    gate_out = jnp.zeros((tm, LANES), F32)
    total = jnp.zeros((tm, 1), F32)
    for k in range(TOP_K):
        m = sel.max(axis=-1, keepdims=True)
        idx = jnp.where(sel == m, col, ne).min(axis=-1, keepdims=True)
        hit = col == idx
        g = jnp.where(hit, scores, 0.0).sum(axis=-1, keepdims=True)
        sel = jnp.where(hit, -jnp.inf, sel)
        idx_out = jnp.where(out_lane == k, idx, idx_out)
        gate_out = jnp.where(out_lane == k, g, gate_out)
        total = total + g
    idx_ref[...] = idx_out
    gate_ref[...] = gate_out * (ROUTE_SCALE / total)


def _router(h2, w_router, b_router, tm=512):
    t, d = h2.shape
    ne = w_router.shape[1]
    w_hi = w_router.astype(BF16)
    w_lo = (w_router - w_hi.astype(F32)).astype(BF16)
    const = lambda i: (0, 0)
    row = lambda i: (i, 0)
    return pl.pallas_call(
        _router_kernel,
        grid=(t // tm,),
        in_specs=[pl.BlockSpec((tm, d), row), pl.BlockSpec((d, ne), const), pl.BlockSpec((d, ne), const),
                  pl.BlockSpec((1, ne), const)],
        out_specs=[pl.BlockSpec((tm, LANES), row), pl.BlockSpec((tm, LANES), row)],
        out_shape=[jax.ShapeDtypeStruct((t, LANES), jnp.int32), jax.ShapeDtypeStruct((t, LANES), F32)],
        compiler_params=_cparams("parallel"),
        name="router",
    )(h2, w_hi, w_lo, b_router.reshape(1, ne).astype(F32))


MOE_TILE = 4096
MOE_ROWS = 160


def _moe_dispatch(idx, gate, n_tiles):
    t, k = idx.shape
    pad = n_tiles * MOE_TILE - t
    slots = MOE_TILE * k
    e = jnp.pad(idx, ((0, pad), (0, 0)), constant_values=N_EXPERTS).reshape(n_tiles, slots)
    g = jnp.pad(gate, ((0, pad), (0, 0))).reshape(n_tiles, slots)
    row8 = jnp.broadcast_to((jnp.arange(slots, dtype=jnp.int32) // k) * SUBLANES, (n_tiles, slots))
    e_sorted, g_sorted, row8_sorted = lax.sort((e, g, row8), dimension=1, num_keys=1)
    offs = jnp.sum(e_sorted[:, None, :] < jnp.arange(N_EXPERTS + 1, dtype=e.dtype)[None, :, None], axis=2)
    offs = jnp.pad(offs.astype(jnp.int32).reshape(-1), (0, SUBLANES))
    return row8_sorted.reshape(-1), g_sorted.reshape(-1), offs


def _moe_kernel(offs_ref, x_ref, row_ref, gate_ref, wg_ref, wu_ref, wd_ref, o_ref, g_a, g_b, y_a, y_b):
    i = pl.program_id(0)
    j = pl.program_id(1)
    nj = pl.num_programs(1)
    r, sl = MOE_ROWS, SUBLANES
    n_slots = row_ref.shape[0]
    all_rows = list(range(r))

    def segment(ee):
        start = offs_ref[i * (N_EXPERTS + 1) + ee]
        n = offs_ref[i * (N_EXPERTS + 1) + ee + 1] - start
        n_fast = jnp.where(start <= n_slots - r, jnp.minimum(n, r), 0)
        return start, jnp.minimum(start, n_slots - r), n_fast, n

    def gather(buf, base):
        for row in all_rows:
            buf[pl.ds(row * sl, sl), :] = x_ref[pl.ds(pl.multiple_of(row_ref[base + row], sl), sl), :]

    def experts(k, buf_in, buf_out, n_valid):
        x = jnp.concatenate([buf_in[pl.ds(s, r, stride=sl), :] for s in range(sl)], axis=1).astype(BF16)
        hid = (_silu(_dot(x, wg_ref[k])) * _dot(x, wu_ref[k])).astype(BF16)
        y = _dot(hid, wd_ref[k])
        y = jnp.where(lax.broadcasted_iota(jnp.int32, (r, 1), 0) < n_valid, y, 0.0)
        for s in range(sl):
            buf_out[pl.ds(s, r, stride=sl), :] = y[:, s * LANES:(s + 1) * LANES]

    def scatter(buf, base):
        for g0 in range(0, r, sl):
            dsts, news = [], []
            for row in all_rows[g0:g0 + sl]:
                dst = pl.ds(pl.multiple_of(row_ref[base + row], sl), sl)
                dsts.append(dst)
                news.append(o_ref[dst, :] + gate_ref[base + row] * buf[pl.ds(row * sl, sl), :])
            for dst, new in reversed(list(zip(dsts, news))):
                o_ref[dst, :] = new

    def slow_path(k, start, n_fast, n, g_free, y_free):
        def chunk(c, carry):
            base = start + n_fast + c * r
            n_valid = jnp.minimum(n - n_fast - c * r, r)

            def slot(row):
                return jnp.minimum(base + row, n_slots - 1)

            def gather_group(gi, carry):
                for row in [gi * sl + q for q in range(sl)]:
                    src = pl.ds(pl.multiple_of(row_ref[slot(row)], sl), sl)
                    g_free[pl.ds(pl.multiple_of(row * sl, sl), sl), :] = x_ref[src, :]
                return carry

            def scatter_group(gi, carry):
                dsts, news = [], []
                for row in [gi * sl + q for q in range(sl)]:
                    dst = pl.ds(pl.multiple_of(row_ref[slot(row)], sl), sl)
                    dsts.append(dst)
                    news.append(o_ref[dst, :] + gate_ref[slot(row)] * y_free[pl.ds(pl.multiple_of(row * sl, sl), sl), :])
                for dst, new in reversed(list(zip(dsts, news))):
                    o_ref[dst, :] = new
                return carry

            lax.fori_loop(0, r // sl, gather_group, 0)
            experts(k, g_free, y_free, n_valid)
            lax.fori_loop(0, r // sl, scatter_group, 0)
            return carry

        lax.fori_loop(0, (n - n_fast + r - 1) // r, chunk, 0)

    ea = 2 * j
    start_a, base_a, fast_a, n_a = segment(ea)
    start_b, base_b, fast_b, n_b = segment(ea + 1)
    _, base_next, _, _ = segment(jnp.minimum(ea + 2, N_EXPERTS - 1))
    _, base_prev, _, _ = segment(jnp.maximum(ea - 1, 0))

    @pl.when(j == 0)
    def _():
        o_ref[...] = jnp.zeros_like(o_ref)
        y_b[...] = jnp.zeros_like(y_b)
        gather(g_a, base_a)

    experts(0, g_a, y_a, fast_a)
    gather(g_b, base_b)
    scatter(y_b, base_prev)
    experts(1, g_b, y_b, fast_b)
    gather(g_a, base_next)
    scatter(y_a, base_a)

    @pl.when(j == nj - 1)
    def _():
        scatter(y_b, base_b)

    @pl.when(n_a > fast_a)
    def _():
        slow_path(0, start_a, fast_a, n_a, g_b, y_a)

    @pl.when(n_b > fast_b)
    def _():
        slow_path(1, start_b, fast_b, n_b, g_b, y_a)


def _moe_routed(x_tiles, tok, gate, offs, w_gate, w_up, w_down, n_tiles):
    ne, d, f = w_gate.shape
    assert ne % 2 == 0
    rows = MOE_TILE * SUBLANES
    slots = MOE_TILE * TOP_K
    grid_spec = pltpu.PrefetchScalarGridSpec(
        num_scalar_prefetch=1,
        grid=(n_tiles, ne // 2),
        in_specs=[
            pl.BlockSpec((rows, LANES), lambda i, j, offs: (i, 0), pipeline_mode=pl.Buffered(1)),
            pl.BlockSpec((slots,), lambda i, j, offs: (i,), memory_space=pltpu.SMEM),
            pl.BlockSpec((slots,), lambda i, j, offs: (i,), memory_space=pltpu.SMEM),
            pl.BlockSpec((2, d, f), lambda i, j, offs: (j, 0, 0)),
            pl.BlockSpec((2, d, f), lambda i, j, offs: (j, 0, 0)),
            pl.BlockSpec((2, f, d), lambda i, j, offs: (j, 0, 0)),
        ],
        out_specs=pl.BlockSpec((rows, LANES), lambda i, j, offs: (i, 0), pipeline_mode=pl.Buffered(1)),
        scratch_shapes=[pltpu.VMEM((MOE_ROWS * SUBLANES, LANES), F32)] * 4,
    )
    return pl.pallas_call(
        _moe_kernel,
        grid_spec=grid_spec,
        out_shape=jax.ShapeDtypeStruct(x_tiles.shape, F32),
        compiler_params=_cparams("arbitrary", "arbitrary"),
        name="moe_routed",
    )(offs, x_tiles, tok, gate, w_gate, w_up, w_down)


def _shared_kernel(final, h_ref, routed_ref, x1_ref, mod_ref, wg_ref, wu_ref, wd_ref, fg_ref, o_ref):
    h = h_ref[...].astype(BF16)
    hid = (_silu(_dot(h, wg_ref[...])) * _dot(h, wu_ref[...])).astype(BF16)
    tm = h.shape[0]
    routed = jnp.concatenate([routed_ref[pl.ds(s, tm, stride=SUBLANES), :] for s in range(SUBLANES)], axis=1)
    x2 = x1_ref[...] + mod_ref[0, 5:6, :] * (routed + _dot(hid, wd_ref[...]))
    if final:
        x2 = x2 * lax.rsqrt(jnp.mean(x2 * x2, axis=-1, keepdims=True) + RMS_EPS) * fg_ref[...]
    o_ref[...] = x2


def _shared_residual(h2, routed, x1, mod, ws_gate, ws_up, ws_down, final_gain, final, n_lat, seq, tm=512):
    t, d = x1.shape
    n_lat_tiles = n_lat // tm
    per_batch = seq // tm
    ctx_row = mod.shape[0] - 1

    def mod_map(i):
        return (jnp.where(i < n_lat_tiles, i // per_batch, ctx_row), 0, 0)

    const = lambda i: (0, 0)
    row = lambda i: (i, 0)
    full = lambda a: pl.BlockSpec(a.shape, const)
    return pl.pallas_call(
        functools.partial(_shared_kernel, final),
        grid=(t // tm,),
        in_specs=[pl.BlockSpec((tm, d), row), pl.BlockSpec((tm * SUBLANES, LANES), row), pl.BlockSpec((tm, d), row),
                  pl.BlockSpec((1, 6, d), mod_map), full(ws_gate), full(ws_up), full(ws_down),
                  pl.BlockSpec((1, d), const)],
        out_specs=pl.BlockSpec((tm, d), row),
        out_shape=jax.ShapeDtypeStruct((t, d), F32),
        compiler_params=_cparams("parallel"),
        name="shared_residual",
    )(h2, routed, x1, mod, ws_gate, ws_up, ws_down, final_gain.reshape(1, d))


MOD_ROWS = 16


def kernel(x, c, ctx, c_ctx, w_ada, b_ada, norm1_gain, norm2_gain, w_in, na_rpb, hg_lower_bounds, hg_norm_gain,
           w_fn_branch, w_na_branch, w_hg_branch, w_out, w_router, b_router, w_exp_gate, w_exp_up, w_exp_down,
           w_sh_gate, w_sh_up, w_sh_down, final_gain):
    b, n, d = x.shape
    nc = ctx.shape[1]
    depth = w_ada.shape[0]
    n_lat, n_ctx = b * n, b * nc

    sm = jax.nn.softmax(hg_lower_bounds.astype(F32), axis=0)
    cs = jnp.cumsum(sm, axis=0)
    lower = cs - cs[0]

    cin = jnp.concatenate([c, c_ctx[None, :], jnp.zeros((MOD_ROWS - b - 1, d), F32)], axis=0)
    mod = _ada_mod(cin, w_ada, b_ada)[:, :b + 1].reshape(depth, b + 1, 6, d)
    hg_consts = _hgrn_tables()
    zero_state = jnp.zeros((b, 2, HG_WIDTH, HG_WIDTH), F32)

    x_all = jnp.concatenate([x.reshape(n_lat, d), ctx.reshape(n_ctx, d)], axis=0)
    for l in range(depth):
        last = l == depth - 1
        w = w_in[l].astype(BF16)
        c0, c1, c2 = FN_WIDTH, FN_WIDTH + 3 * NA_WIDTH, FN_WIDTH + 3 * NA_WIDTH + 5 * HG_WIDTH
        z_fn, z_qkv, z_hg = _proj_in(x_all, mod[l], norm1_gain[l], w[:, :c0], w[:, c0:c1], w[:, c1:c2], n_lat, n)

        lbp = _hgrn_lb_params(lower[l])
        of_c, ob_c, s_ctx = _hgrn(z_hg, zero_state, lbp, hg_consts, b, nc, n_lat)
        of_l, ob_l, _ = _hgrn(z_hg, s_ctx, lbp, hg_consts, b, n, 0)
        fn_l = _fourier_lat(z_fn, b, n)
        na_l = _na_lat(z_qkv, _na_bias_table(na_rpb[l]), b, n, nc)
        if last:
            n_rows = n_lat
            fn_a, na_a, of_a, ob_a = fn_l, na_l, of_l, ob_l
        else:
            n_rows = n_lat + n_ctx
            cat = lambda p, q: jnp.concatenate([p, q], axis=0)
            fn_a = jnp.concatenate([fn_l, _fourier_ctx(z_fn, b, nc, n_lat)], axis=1)
            na_a = cat(na_l, _na_ctx(z_qkv, b, n, nc))
            of_a, ob_a = cat(of_l, of_c), cat(ob_l, ob_c)
        x1, h2, h2_tiles = _merge(x_all, mod[l], norm1_gain[l], norm2_gain[l], fn_a, na_a, of_a, ob_a, z_hg, hg_norm_gain[l],
                        w_fn_branch[l].astype(BF16), w_na_branch[l].astype(BF16), w_hg_branch[l].astype(BF16),
                        w[:, c2:], w_out[l].astype(BF16), n_rows, n_lat, n)

        idx, gate = _router(h2, w_router[l], b_router[l])
        n_tiles = -(-n_rows // MOE_TILE)
        tok, gate_sorted, offs = _moe_dispatch(idx[:, :TOP_K], gate[:, :TOP_K], n_tiles)
        routed = _moe_routed(h2_tiles, tok, gate_sorted, offs, w_exp_gate[l].astype(BF16), w_exp_up[l].astype(BF16),
                             w_exp_down[l].astype(BF16), n_tiles)
        x_all = _shared_residual(h2, routed, x1, mod[l], w_sh_gate[l].astype(BF16),
                                 w_sh_up[l].astype(BF16), w_sh_down[l].astype(BF16), final_gain, last, n_lat, n)
    return x_all.reshape(b, n, d)
```

```python
import functools

import numpy as np
import jax
import jax.numpy as jnp
from jax import lax
from jax.experimental import pallas as pl
from jax.experimental.pallas import tpu as pltpu

D_MODEL = 1024
GRID_W = 64
RMS_EPS = 1e-6
FN_GROUPS = 4
FN_GROUP_DIM = 64
FN_WIDTH = 256
NA_HEADS = 8
NA_HEAD_DIM = 64
NA_WIDTH = 512
NA_WIN_ROWS = 8
NA_WIN_COLS = 16
HG_HEADS = 4
HG_HEAD_DIM = 64
HG_WIDTH = 256
N_EXPERTS = 256
TOP_K = 8
EXPERT_DIM = 256
ROUTE_SCALE = 2.5

LANES = 128
SUBLANES = 8

F32 = jnp.float32
BF16 = jnp.bfloat16
VMEM_LIMIT = 56 * 1024 * 1024


def _cparams(*sem):
    return pltpu.CompilerParams(dimension_semantics=sem, vmem_limit_bytes=VMEM_LIMIT)


def _dot(a, b):
    return jnp.dot(a, b, preferred_element_type=F32)


def _dot_nt(a, b):
    return lax.dot_general(a, b, (((1,), (1,)), ((), ())), preferred_element_type=F32)


def _sigmoid(z):
    e = jnp.exp(-jnp.abs(z))
    r = 1.0 / (1.0 + e)
    return jnp.where(z >= 0, r, e * r)


def _silu(z):
    return z * _sigmoid(z)


def _norm_mod(x, gain, shift, scale):
    y = x * lax.rsqrt(jnp.mean(x * x, axis=-1, keepdims=True) + RMS_EPS)
    return (y * gain) * (1.0 + scale) + shift


def _ada_kernel(c_ref, w_ref, b_ref, o_ref):
    s = _silu(c_ref[...]).astype(BF16)
    o_ref[0] = _dot(s, w_ref[0].astype(BF16)) + b_ref[0]


def _ada_mod(cin, w_ada, b_ada):
    depth, d, n = w_ada.shape
    r = cin.shape[0]
    tn = 1536
    return pl.pallas_call(
        _ada_kernel,
        grid=(depth, n // tn),
        in_specs=[
            pl.BlockSpec((r, d), lambda l, j: (0, 0)),
            pl.BlockSpec((1, d, tn), lambda l, j: (l, 0, j)),
            pl.BlockSpec((1, 1, tn), lambda l, j: (l, 0, j)),
        ],
        out_specs=pl.BlockSpec((1, r, tn), lambda l, j: (l, 0, j)),
        out_shape=jax.ShapeDtypeStruct((depth, r, n), F32),
        compiler_params=_cparams("parallel", "parallel"),
        name="ada_mod",
    )(cin, w_ada, b_ada.reshape(depth, 1, n))


def _proj_in_kernel(x_ref, mod_ref, gain_ref, wfn_ref, wqkv_ref, whg_ref, fn_ref, qkv_ref, hg_ref):
    h = _norm_mod(x_ref[...], gain_ref[...], mod_ref[0, 0:1, :], mod_ref[0, 1:2, :]).astype(BF16)
    fn_ref[...] = _dot(h, wfn_ref[...])
    qkv_ref[...] = _dot(h, wqkv_ref[...]).astype(BF16)
    hg_ref[...] = _dot(h, whg_ref[...])


def _proj_in(x_all, mod, gain, w_fn, w_qkv, w_hg, n_lat, seq, tm=512):
    t, d = x_all.shape
    n_lat_tiles = n_lat // tm
    per_batch = seq // tm
    ctx_row = mod.shape[0] - 1

    def mod_map(i):
        return (jnp.where(i < n_lat_tiles, i // per_batch, ctx_row), 0, 0)

    const = lambda i: (0, 0)
    row = lambda i: (i, 0)
    return pl.pallas_call(
        _proj_in_kernel,
        grid=(t // tm,),
        in_specs=[
            pl.BlockSpec((tm, d), row),
            pl.BlockSpec((1, 6, d), mod_map),
            pl.BlockSpec((1, d), const),
            pl.BlockSpec(w_fn.shape, const),
            pl.BlockSpec(w_qkv.shape, const),
            pl.BlockSpec(w_hg.shape, const),
        ],
        out_specs=[
            pl.BlockSpec((tm, w_fn.shape[1]), row),
            pl.BlockSpec((tm, w_qkv.shape[1]), row),
            pl.BlockSpec((tm, w_hg.shape[1]), row),
        ],
        out_shape=[
            jax.ShapeDtypeStruct((t, w_fn.shape[1]), F32),
            jax.ShapeDtypeStruct((t, w_qkv.shape[1]), BF16),
            jax.ShapeDtypeStruct((t, w_hg.shape[1]), F32),
        ],
        compiler_params=_cparams("parallel"),
        name="proj_in",
    )(x_all, mod, gain.reshape(1, d), w_fn, w_qkv, w_hg)


def _dft_cos_sin(n):
    k = np.arange(n)
    ang = 2.0 * np.pi * ((k[:, None] * k[None, :]) % n) / n
    return np.cos(ang), np.sin(ang)


def _channel_dft_mats(scale):
    c, s = _dft_cos_sin(FN_GROUP_DIM)
    eye = np.eye(FN_GROUPS)
    return (jnp.asarray(np.kron(eye, c) * scale, BF16), jnp.asarray(np.kron(eye, -s) * scale, BF16))


FFT_R = 64


def _store_halves(ref, lead, rows, val):
    ref[lead + (0, rows, slice(None))] = val[:, :LANES]
    ref[lead + (1, rows, slice(None))] = val[:, LANES:]


def _load_halves(ref, lead, rows):
    return jnp.concatenate([ref[lead + (0, rows, slice(None))], ref[lead + (1, rows, slice(None))]], axis=1)


def _fourier_lat_kernel(u_ref, cc_ref, sc_ref, ga_ref, gb_ref, tc_ref, ts_ref, o_ref, u_s, z_s):
    ub = u_ref[...].astype(BF16)
    every = slice(None)
    _store_halves(u_s, (0,), every, _dot(ub, cc_ref[...]))
    _store_halves(u_s, (1,), every, _dot(ub, sc_ref[...]))

    def stage_a(n2, carry):
        rows = pl.ds(n2, FFT_R, stride=FFT_R)
        x = jnp.concatenate([_load_halves(u_s, (0,), rows), _load_halves(u_s, (1,), rows)], axis=0).astype(BF16)
        a = _dot(ga_ref[...], x)
        ar, ai = a[:FFT_R], a[FFT_R:]
        tc = jnp.concatenate([tc_ref[n2], tc_ref[n2]], axis=1)
        ts = jnp.concatenate([ts_ref[n2], ts_ref[n2]], axis=1)
        dst = pl.ds(pl.multiple_of(n2 * FFT_R, FFT_R), FFT_R)
        _store_halves(z_s, (0,), dst, ar * tc + ai * ts)
        _store_halves(z_s, (1,), dst, ai * tc - ar * ts)
        return carry

    lax.fori_loop(0, FFT_R, stage_a, 0)

    def stage_b(k1, carry):
        rows = pl.ds(k1, FFT_R, stride=FFT_R)
        z = jnp.concatenate([_load_halves(z_s, (0,), rows), _load_halves(z_s, (1,), rows)], axis=0).astype(BF16)
        _store_halves(o_ref, (), rows, _dot(gb_ref[...], z))
        return carry

    lax.fori_loop(0, FFT_R, stage_b, 0)


def _fourier_lat(z_fn, n_batch, seq):
    assert seq == FFT_R * FFT_R
    c, s = _dft_cos_sin(FFT_R)
    cc, sc = _channel_dft_mats((seq * FN_GROUP_DIM) ** -0.5)
    ga = jnp.asarray(np.block([[c, s], [-s, c]]), BF16)
    gb = jnp.asarray(np.concatenate([c, s], axis=1), BF16)
    k = np.arange(FFT_R)
    ang = 2.0 * np.pi * (k[:, None] * k[None, :]) / seq
    tc = jnp.asarray(np.broadcast_to(np.cos(ang)[:, :, None], (FFT_R, FFT_R, 128)), F32)
    ts = jnp.asarray(np.broadcast_to(np.sin(ang)[:, :, None], (FFT_R, FFT_R, 128)), F32)
    const2 = lambda b: (0, 0)
    const3 = lambda b: (0, 0, 0)
    return pl.pallas_call(
        _fourier_lat_kernel,
        grid=(n_batch,),
        in_specs=[
            pl.BlockSpec((seq, FN_WIDTH), lambda b: (b, 0)),
            pl.BlockSpec(cc.shape, const2),
            pl.BlockSpec(sc.shape, const2),
            pl.BlockSpec(ga.shape, const2),
            pl.BlockSpec(gb.shape, const2),
            pl.BlockSpec(tc.shape, const3),
            pl.BlockSpec(ts.shape, const3),
        ],
        out_specs=pl.BlockSpec((2, seq, LANES), lambda b: (0, b, 0)),
        out_shape=jax.ShapeDtypeStruct((2, n_batch * seq, LANES), F32),
        scratch_shapes=[pltpu.VMEM((2, 2, seq, LANES), F32), pltpu.VMEM((2, 2, seq, LANES), F32)],
        compiler_params=_cparams("parallel"),
        name="fourier_lat",
    )(z_fn, cc, sc, ga, gb, tc, ts)


def _fourier_ctx_kernel(u_ref, cc_ref, sc_ref, g_ref, o_ref):
    ub = u_ref[...].astype(BF16)
    x = jnp.concatenate([_dot(ub, cc_ref[...]), _dot(ub, sc_ref[...])], axis=0).astype(BF16)
    _store_halves(o_ref, (), slice(None), _dot(g_ref[...], x))


def _fourier_ctx(z_fn, n_batch, n_ctx, row0):
    blk0 = row0 // n_ctx
    c, s = _dft_cos_sin(n_ctx)
    cc, sc = _channel_dft_mats((n_ctx * FN_GROUP_DIM) ** -0.5)
    g = jnp.asarray(np.concatenate([c, s], axis=1), BF16)
    const2 = lambda b: (0, 0)
    return pl.pallas_call(
        _fourier_ctx_kernel,
        grid=(n_batch,),
        in_specs=[
            pl.BlockSpec((n_ctx, FN_WIDTH), lambda b: (blk0 + b, 0)),
            pl.BlockSpec(cc.shape, const2),
            pl.BlockSpec(sc.shape, const2),
            pl.BlockSpec(g.shape, const2),
        ],
        out_specs=pl.BlockSpec((2, n_ctx, LANES), lambda b: (0, b, 0)),
        out_shape=jax.ShapeDtypeStruct((2, n_batch * n_ctx, LANES), F32),
        compiler_params=_cparams("parallel"),
        name="fourier_ctx",
    )(z_fn, cc, sc, g)


NEG_BIG = -1e30
HEAD_PAIR = 2 * NA_HEAD_DIM


def _na_bias_table(rpb):
    wr, wc, w = NA_WIN_ROWS, NA_WIN_COLS, GRID_W
    col = np.arange(w)
    col_start = np.clip(col - wc // 2, 0, w - wc)
    in_win = (col[None, :] >= col_start[:, None]) & (col[None, :] < col_start[:, None] + wc)
    dc_idx = np.clip(col[None, :] - col[:, None], 1 - wc, wc - 1) + (wc - 1)
    h, n_dr, n_dc = rpb.shape
    onehot = jnp.asarray(dc_idx.reshape(-1)[None, :] == np.arange(n_dc)[:, None], F32)
    t = jnp.dot(rpb.astype(F32).reshape(h * n_dr, n_dc), onehot, precision=lax.Precision.HIGHEST)
    t = jnp.where(in_win.reshape(-1)[None, :], t, NEG_BIG).reshape(h, n_dr, w, w)
    per_s = [jnp.concatenate([t[:, s + j] for j in range(wr)], axis=-1) for s in range(wr)]
    return jnp.stack(per_s, axis=1)


def _softmax_pv(q, keys, vals, biases):
    s = []
    for k, b in zip(keys, biases):
        si = _dot_nt(q, k)
        s.append(si if b is None else si + b)
    m = s[0].max(axis=-1, keepdims=True)
    for si in s[1:]:
        m = jnp.maximum(m, si.max(axis=-1, keepdims=True))
    acc = None
    l = None
    for si, v in zip(s, vals):
        p = jnp.exp(si - m)
        li = p.sum(axis=-1, keepdims=True)
        oi = _dot(p.astype(BF16), v)
        acc = oi if acc is None else acc + oi
        l = li if l is None else l + li
    return acc / l


def _na_lat_kernel(q_ref, k_ref, v_ref, kc_ref, vc_ref, bias_ref, o_ref):
    w = GRID_W
    n_loc = NA_WIN_ROWS * w
    rows = q_ref.shape[0] // w
    lane = lax.broadcasted_iota(jnp.int32, (w, HEAD_PAIR), 1)
    first = lane < NA_HEAD_DIM
    kc = kc_ref[...]
    vc = vc_ref[...]

    def body(r, carry):
        kr0 = jnp.clip(r - NA_WIN_ROWS // 2, 0, rows - NA_WIN_ROWS)
        s = kr0 - r + (NA_WIN_ROWS - 1)
        q = q_ref[pl.ds(pl.multiple_of(r * w, w), w), :] * jnp.asarray(NA_HEAD_DIM ** -0.5, BF16)
        ks = k_ref[pl.ds(pl.multiple_of(kr0 * w, w), n_loc), :]
        vs = v_ref[pl.ds(pl.multiple_of(kr0 * w, w), n_loc), :]
        zero = jnp.zeros_like(q)
        q2 = jnp.concatenate([jnp.where(first, q, zero), jnp.where(first, zero, q)], axis=0)
        bias = jnp.concatenate([bias_ref[0, s], bias_ref[1, s]], axis=0)
        o2 = _softmax_pv(q2, [ks, kc], [vs, vc], [bias, None])
        o_ref[pl.ds(pl.multiple_of(r * w, w), w), :] = jnp.where(first, o2[:w], o2[w:]).astype(o_ref.dtype)
        return carry

    lax.fori_loop(0, rows, body, 0, unroll=4)


def _na_lat(qkv, bias, n_batch, seq, n_ctx):
    n_pairs = NA_WIDTH // HEAD_PAIR
    ctx0 = n_batch * seq // n_ctx
    wr, w = NA_WIN_ROWS, GRID_W
    return pl.pallas_call(
        _na_lat_kernel,
        grid=(n_batch, n_pairs),
        in_specs=[
            pl.BlockSpec((seq, HEAD_PAIR), lambda b, p: (b, p)),
            pl.BlockSpec((seq, HEAD_PAIR), lambda b, p: (b, n_pairs + p)),
            pl.BlockSpec((seq, HEAD_PAIR), lambda b, p: (b, 2 * n_pairs + p)),
            pl.BlockSpec((n_ctx, HEAD_PAIR), lambda b, p: (ctx0 + b, n_pairs + p)),
            pl.BlockSpec((n_ctx, HEAD_PAIR), lambda b, p: (ctx0 + b, 2 * n_pairs + p)),
            pl.BlockSpec((2, wr, w, wr * w), lambda b, p: (p, 0, 0, 0)),
        ],
        out_specs=pl.BlockSpec((seq, HEAD_PAIR), lambda b, p: (b, p)),
        out_shape=jax.ShapeDtypeStruct((n_batch * seq, NA_WIDTH), BF16),
        compiler_params=_cparams("parallel", "parallel"),
        name="na_lat",
    )(qkv, qkv, qkv, qkv, qkv, bias)


def _na_ctx_kernel(q_ref, k_ref, v_ref, o_ref):
    lane = lax.broadcasted_iota(jnp.int32, q_ref.shape, 1)
    first = lane < NA_HEAD_DIM
    q = q_ref[...] * jnp.asarray(NA_HEAD_DIM ** -0.5, BF16)
    k = k_ref[...]
    v = v_ref[...]
    zero = jnp.zeros_like(q)
    o0 = _softmax_pv(jnp.where(first, q, zero), [k], [v], [None])
    o1 = _softmax_pv(jnp.where(first, zero, q), [k], [v], [None])
    o_ref[...] = jnp.where(first, o0, o1).astype(o_ref.dtype)


def _na_ctx(qkv, n_batch, seq, n_ctx):
    n_pairs = NA_WIDTH // HEAD_PAIR
    ctx0 = n_batch * seq // n_ctx
    return pl.pallas_call(
        _na_ctx_kernel,
        grid=(n_batch, n_pairs),
        in_specs=[
            pl.BlockSpec((n_ctx, HEAD_PAIR), lambda b, p: (ctx0 + b, p)),
            pl.BlockSpec((n_ctx, HEAD_PAIR), lambda b, p: (ctx0 + b, n_pairs + p)),
            pl.BlockSpec((n_ctx, HEAD_PAIR), lambda b, p: (ctx0 + b, 2 * n_pairs + p)),
        ],
        out_specs=pl.BlockSpec((n_ctx, HEAD_PAIR), lambda b, p: (b, p)),
        out_shape=jax.ShapeDtypeStruct((n_batch * n_ctx, NA_WIDTH), BF16),
        compiler_params=_cparams("parallel", "parallel"),
        name="na_ctx",
    )(qkv, qkv, qkv)


HG_CHUNK = 64
HG_LEVELS = (32, 16)
HG_DIAG = 16
HG_BATCH = 2


def _hgrn_tables():
    c = HG_CHUNK
    tri, masks, refs, last = [], [], [], []
    for reverse in (False, True):
        u = np.arange(c)[::-1] if reverse else np.arange(c)
        row_of = {int(uu): t for t, uu in enumerate(u)}
        ut, uj = u[:, None], u[None, :]
        tri.append(uj <= ut)
        lv, rf = [], []
        for h in HG_LEVELS:
            same = (ut // (2 * h)) == (uj // (2 * h))
            up_t, up_j = (ut % (2 * h)) >= h, (uj % (2 * h)) >= h
            lv.append(same & up_t & ~up_j)
            rf.append([row_of[int(uu) // (2 * h) * (2 * h) + h - 1] for uu in u])
        same_d = (ut // HG_DIAG) == (uj // HG_DIAG)
        lv.append(same_d & (uj <= ut))
        rf.append([row_of.get(int(uu) // HG_DIAG * HG_DIAG - 1, -1) for uu in u])
        masks.append(np.stack([np.tile(x, (1, HG_HEADS)) for x in lv]).astype(np.float32))
        refs.append(rf)
        last.append(row_of[c - 1])
    hm = (np.arange(HG_WIDTH)[:, None] // HG_HEAD_DIM) == (np.arange(HG_WIDTH)[None, :] // HG_HEAD_DIM)
    arrays = (jnp.asarray(np.stack(tri), BF16), jnp.asarray(np.stack(masks), F32), jnp.asarray(hm, F32))
    return arrays, refs, last


def _ref_rows(b, ref):
    pieces, t = [], 0
    while t < len(ref):
        t1 = t
        while t1 < len(ref) and ref[t1] == ref[t]:
            t1 += 1
        shape = (t1 - t, b.shape[1])
        pieces.append(jnp.zeros(shape, b.dtype) if ref[t] < 0 else jnp.broadcast_to(b[ref[t]:ref[t] + 1, :], shape))
        t = t1
    return jnp.concatenate(pieces, axis=0)


def _split3(x):
    hi = x.astype(BF16)
    r = x - hi.astype(F32)
    mid = r.astype(BF16)
    lo = (r - mid.astype(F32)).astype(BF16)
    return hi, mid, lo


def _hgrn_chunk(zq, zv, zf, lbp, tri, masks, hm, st, refs, last_row):
    q = _silu(zq)
    e = jnp.exp(-jnp.abs(zf))
    inv = 1.0 / (1.0 + e)
    log_sig = jnp.minimum(zf, 0.0) - jnp.log(1.0 + e)
    sig_neg = jnp.where(zf >= 0, e * inv, inv)
    a = lbp[0:1, :]
    cc = lbp[1:2, :] + log_sig
    log_f = jnp.maximum(a, cc) + jnp.log(1.0 + jnp.exp(-jnp.abs(a - cc)))
    k = lbp[2:3, :] * sig_neg

    hi, mid, lo = _split3(log_f)
    b = _dot(tri, hi) + _dot(tri, mid) + _dot(tri, lo)
    b_last = b[last_row:last_row + 1]

    def heads_bd(x):
        return (jnp.concatenate([x] * HG_HEADS, axis=0) * hm).astype(BF16)

    a_all = None
    for i, ref in enumerate(refs):
        rel = b - _ref_rows(b, ref)
        if i < len(HG_LEVELS):
            qx, kx = q * jnp.exp(jnp.minimum(rel, 0.0)), k * jnp.exp(jnp.minimum(-rel, 0.0))
        else:
            qx, kx = q * jnp.exp(rel), k * jnp.exp(-rel)
        ai = _dot_nt(qx.astype(BF16), heads_bd(kx)) * masks[i]
        a_all = ai if a_all is None else a_all + ai
    o = _dot(a_all.astype(BF16), heads_bd(zv)) + _dot_nt((q * jnp.exp(b)).astype(BF16), st.astype(BF16))
    kl = (k * jnp.exp(b_last - b)).astype(BF16)
    upd = lax.dot_general(zv.astype(BF16), kl, (((0,), (0,)), ((), ())), preferred_element_type=F32)
    st_new = st * jnp.exp(b_last) + upd * hm
    return o, st_new


def _hgrn_kernel(refs, last, *args):
    nb = HG_BATCH
    z_refs = args[:3 * nb]
    s0_ref, lbp_ref, tri_ref, mask_ref, hm_ref = args[3 * nb:3 * nb + 5]
    of_ref, ob_ref, sfin_ref, st = args[3 * nb + 5:]
    i = pl.program_id(1)

    @pl.when(i == 0)
    def _():
        st[...] = s0_ref[...]

    w = HG_WIDTH
    hm = hm_ref[...]
    for bb in range(nb):
        zf_ref, zb_ref, zbg_ref = z_refs[3 * bb:3 * bb + 3]
        o, s = _hgrn_chunk(zf_ref[:, 0:w], zf_ref[:, w:2 * w], zf_ref[:, 2 * w:3 * w], lbp_ref[0],
                           tri_ref[0], mask_ref[0], hm, st[bb, 0], refs[0], last[0])
        of_ref[bb] = o
        st[bb, 0] = s
        o, s = _hgrn_chunk(zb_ref[:, 0:w], zb_ref[:, w:2 * w], zbg_ref[...], lbp_ref[1],
                           tri_ref[1], mask_ref[1], hm, st[bb, 1], refs[1], last[1])
        ob_ref[bb] = o
        st[bb, 1] = s

    @pl.when(i == pl.num_programs(1) - 1)
    def _():
        sfin_ref[...] = st[...]


def _hgrn(z_hg, s0, lbp, tables, n_batch, n_tok, row0):
    (tri, masks, hm), refs, last = tables
    c, w, nb = HG_CHUNK, HG_WIDTH, HG_BATCH
    nch = n_tok // c
    base = row0 // c
    const3 = lambda b, i: (0, 0, 0)
    in_specs = []
    for bb in range(nb):
        fwd = lambda b, i, bb=bb: (base + (b * nb + bb) * nch + i, 0)
        bwd = lambda b, i, bb=bb: (base + (b * nb + bb) * nch + (nch - 1 - i), 0)
        bwd_gate = lambda b, i, bb=bb: (base + (b * nb + bb) * nch + (nch - 1 - i), 3)
        in_specs += [pl.BlockSpec((c, 3 * w), fwd), pl.BlockSpec((c, 2 * w), bwd), pl.BlockSpec((c, w), bwd_gate)]
    in_specs += [
        pl.BlockSpec((nb, 2, w, w), lambda b, i: (b, 0, 0, 0)),
        pl.BlockSpec(lbp.shape, const3),
        pl.BlockSpec(tri.shape, const3),
        pl.BlockSpec(masks.shape, lambda b, i: (0, 0, 0, 0)),
        pl.BlockSpec(hm.shape, lambda b, i: (0, 0)),
    ]
    o_f, o_b, s_fin = pl.pallas_call(
        functools.partial(_hgrn_kernel, refs, last),
        grid=(n_batch // nb, nch),
        in_specs=in_specs,
        out_specs=[
            pl.BlockSpec((nb, c, w), lambda b, i: (b, i, 0)),
            pl.BlockSpec((nb, c, w), lambda b, i: (b, nch - 1 - i, 0)),
            pl.BlockSpec((nb, 2, w, w), lambda b, i: (b, 0, 0, 0)),
        ],
        out_shape=[
            jax.ShapeDtypeStruct((n_batch, n_tok, w), F32),
            jax.ShapeDtypeStruct((n_batch, n_tok, w), F32),
            jax.ShapeDtypeStruct((n_batch, 2, w, w), F32),
        ],
        scratch_shapes=[pltpu.VMEM((nb, 2, w, w), F32)],
        compiler_params=_cparams("parallel", "arbitrary"),
        name="hgrn",
    )(*([z_hg] * (3 * nb)), s0, lbp, tri, masks, hm)
    return o_f.reshape(n_batch * n_tok, w), o_b.reshape(n_batch * n_tok, w), s_fin


def _hgrn_lb_params(lower_l):
    rows = jnp.stack([jnp.log(lower_l), jnp.log1p(-lower_l), 1.0 - lower_l], axis=1)
    return jnp.pad(rows, ((0, 0), (0, 5), (0, 0)))


def _merge_kernel(x_ref, mod_ref, g1_ref, g2_ref, fn_ref, na_ref, of_ref, ob_ref, hgz_ref, hgain_ref,
                  hmean_ref, wfn_ref, wna_ref, whg_ref, wbg_ref, wout_ref, x1_ref, h2_ref, h2t_ref):
    d = x_ref.shape[1]
    x = x_ref[...]
    h = _norm_mod(x, g1_ref[...], mod_ref[0, 0:1, :], mod_ref[0, 1:2, :]).astype(BF16)
    o = of_ref[...] + ob_ref[...]
    hi, mid, lo = _split3(o * o)
    ms = _dot(hi, hmean_ref[...]) + _dot(mid, hmean_ref[...]) + _dot(lo, hmean_ref[...])
    y_hg_in = o * lax.rsqrt(ms + RMS_EPS) * hgain_ref[...] * _silu(hgz_ref[...])
    y_fn = _dot(jnp.concatenate([fn_ref[0], fn_ref[1]], axis=1).astype(BF16), wfn_ref[...])
    y_na = _dot(na_ref[...], wna_ref[...])
    y_hg = _dot(y_hg_in.astype(BF16), whg_ref[...])
    m = (_sigmoid(_dot(h, wbg_ref[:, 0:d])) * y_fn
         + _sigmoid(_dot(h, wbg_ref[:, d:2 * d])) * y_na
         + _sigmoid(_dot(h, wbg_ref[:, 2 * d:3 * d])) * y_hg)
    x1 = x + mod_ref[0, 2:3, :] * _dot(m.astype(BF16), wout_ref[...])
    x1_ref[...] = x1
    h2 = _norm_mod(x1, g2_ref[...], mod_ref[0, 3:4, :], mod_ref[0, 4:5, :])
    h2_ref[...] = h2
    tm = x.shape[0]
    for s in range(SUBLANES):
        h2t_ref[pl.ds(s, tm, stride=SUBLANES), :] = h2[:, s * LANES:(s + 1) * LANES]


def _merge(x_all, mod, gain1, gain2, fn, na, o_f, o_b, z_hg, hgain, w_fnb, w_nab, w_hgb, w_bg, w_out,
           n_rows, n_lat, seq, tm=256):
    t, d = n_rows, x_all.shape[1]
    n_lat_tiles = n_lat // tm
    per_batch = seq // tm
    ctx_row = mod.shape[0] - 1
    w = HG_WIDTH
    hmean = jnp.asarray(np.kron(np.eye(HG_HEADS), np.full((HG_HEAD_DIM, HG_HEAD_DIM), 1.0 / HG_HEAD_DIM)), BF16)

    def mod_map(i):
        return (jnp.where(i < n_lat_tiles, i // per_batch, ctx_row), 0, 0)

    const = lambda i: (0, 0)
    row = lambda i: (i, 0)
    full = lambda a: pl.BlockSpec(a.shape, const)
    return pl.pallas_call(
        _merge_kernel,
        grid=(t // tm,),
        in_specs=[
            pl.BlockSpec((tm, d), row),
            pl.BlockSpec((1, 6, d), mod_map),
            pl.BlockSpec((1, d), const),
            pl.BlockSpec((1, d), const),
            pl.BlockSpec((2, tm, LANES), lambda i: (0, i, 0)),
            pl.BlockSpec((tm, NA_WIDTH), row),
            pl.BlockSpec((tm, w), row),
            pl.BlockSpec((tm, w), row),
            pl.BlockSpec((tm, w), lambda i: (i, 4)),
            pl.BlockSpec((1, w), const),
            full(hmean), full(w_fnb), full(w_nab), full(w_hgb), full(w_bg), full(w_out),
        ],
        out_specs=[pl.BlockSpec((tm, d), row), pl.BlockSpec((tm, d), row), pl.BlockSpec((tm * SUBLANES, LANES), row)],
        out_shape=[jax.ShapeDtypeStruct((t, d), F32), jax.ShapeDtypeStruct((t, d), F32),
                   jax.ShapeDtypeStruct((t * SUBLANES, LANES), F32)],
        compiler_params=_cparams("parallel"),
        name="merge",
    )(x_all, mod, gain1.reshape(1, d), gain2.reshape(1, d), fn, na, o_f, o_b, z_hg,
      jnp.tile(hgain, HG_HEADS).reshape(1, w), hmean, w_fnb, w_nab, w_hgb, w_bg, w_out)


def _router_kernel(h_ref, whi_ref, wlo_ref, b_ref, idx_ref, gate_ref, cnt_ref):
    h = h_ref[...]
    hi = h.astype(BF16)
    lo = (h - hi.astype(F32)).astype(BF16)
    logits = _dot(hi, whi_ref[...]) + (_dot(hi, wlo_ref[...]) + _dot(lo, whi_ref[...]))
    scores = _sigmoid(logits)
    sel = scores + b_ref[...]
    tm, ne = sel.shape
    col = lax.broadcasted_iota(jnp.int32, (tm, ne), 1)
    out_lane = lax.broadcasted_iota(jnp.int32, (tm, LANES), 1)
    idx_out = jnp.zeros((tm, LANES), jnp.int32)
    gate_out = jnp.zeros((tm, LANES), F32)
    total = jnp.zeros((tm, 1), F32)
    chosen = jnp.zeros((tm, ne), F32)
    for k in range(TOP_K):
        m = sel.max(axis=-1, keepdims=True)
        idx = jnp.where(sel == m, col, ne).min(axis=-1, keepdims=True)
        hit = col == idx
        g = jnp.where(hit, scores, 0.0).sum(axis=-1, keepdims=True)
        sel = jnp.where(hit, -jnp.inf, sel)
        chosen = jnp.where(hit, 1.0, chosen)
        idx_out = jnp.where(out_lane == k, idx, idx_out)
        gate_out = jnp.where(out_lane == k, g, gate_out)
        total = total + g
    idx_ref[...] = idx_out
    gate_ref[...] = gate_out * (ROUTE_SCALE / total)
    cnt_ref[...] = jnp.broadcast_to(chosen.sum(axis=0, keepdims=True), cnt_ref.shape).astype(jnp.int32)


def _router(h2, w_router, b_router, tm=512):
    t, d = h2.shape
    ne = w_router.shape[1]
    w_hi = w_router.astype(BF16)
    w_lo = (w_router - w_hi.astype(F32)).astype(BF16)
    const = lambda i: (0, 0)
    row = lambda i: (i, 0)
    return pl.pallas_call(
        _router_kernel,
        grid=(t // tm,),
        in_specs=[pl.BlockSpec((tm, d), row), pl.BlockSpec((d, ne), const), pl.BlockSpec((d, ne), const),
                  pl.BlockSpec((1, ne), const)],
        out_specs=[pl.BlockSpec((tm, LANES), row), pl.BlockSpec((tm, LANES), row), pl.BlockSpec((SUBLANES, ne), row)],
        out_shape=[jax.ShapeDtypeStruct((t, LANES), jnp.int32), jax.ShapeDtypeStruct((t, LANES), F32),
                   jax.ShapeDtypeStruct((t // tm * SUBLANES, ne), jnp.int32)],
        compiler_params=_cparams("parallel"),
        name="router",
    )(h2, w_hi, w_lo, b_router.reshape(1, ne).astype(F32))


MOE_TILE = 4096
MOE_ROWS = 160
MOE_GROUP = 4


def _moe_dispatch(idx, gate, counts, n_tiles):
    t, k = idx.shape
    pad = n_tiles * MOE_TILE - t
    slots = MOE_TILE * k
    e = jnp.pad(idx, ((0, pad), (0, 0)), constant_values=N_EXPERTS).reshape(n_tiles, slots)
    g = jnp.pad(gate, ((0, pad), (0, 0))).reshape(n_tiles, slots)
    key = e * slots + jnp.arange(slots, dtype=jnp.int32)[None, :]
    key_sorted, g_sorted = lax.sort((key, g), dimension=1, num_keys=1)
    row8_sorted = (key_sorted % slots) // k * SUBLANES
    group = MOE_TILE // (t // counts.shape[0])
    c = jnp.pad(counts, ((0, n_tiles * group - counts.shape[0]), (0, 0))).reshape(n_tiles, group, -1).sum(axis=1)
    offs = jnp.concatenate([jnp.zeros((n_tiles, 1), jnp.int32), jnp.cumsum(c, axis=1)], axis=1)
    offs = jnp.pad(offs.astype(jnp.int32).reshape(-1), (0, SUBLANES))
    return row8_sorted.reshape(-1), g_sorted.reshape(-1), offs


def _moe_kernel(offs_ref, x_ref, row_ref, gate_ref, wg_ref, wu_ref, wd_ref, o_ref, g_a, g_b, y_a, y_b):
    i = pl.program_id(0)
    j = pl.program_id(1)
    nj = pl.num_programs(1)
    r, sl = MOE_ROWS, SUBLANES
    n_slots = row_ref.shape[0]
    all_rows = list(range(r))

    def segment(ee):
        start = offs_ref[i * (N_EXPERTS + 1) + ee]
        n = offs_ref[i * (N_EXPERTS + 1) + ee + 1] - start
        n_fast = jnp.where(start <= n_slots - r, jnp.minimum(n, r), 0)
        return start, jnp.minimum(start, n_slots - r), n_fast, n

    def gather(buf, base):
        for row in all_rows:
            buf[pl.ds(row * sl, sl), :] = x_ref[pl.ds(pl.multiple_of(row_ref[base + row], sl), sl), :]

    def experts(k, buf_in, buf_out, n_valid):
        x = jnp.concatenate([buf_in[pl.ds(s, r, stride=sl), :] for s in range(sl)], axis=1).astype(BF16)
        hid = (_silu(_dot(x, wg_ref[k])) * _dot(x, wu_ref[k])).astype(BF16)
        y = _dot(hid, wd_ref[k])
        y = jnp.where(lax.broadcasted_iota(jnp.int32, (r, 1), 0) < n_valid, y, 0.0)
        for s in range(sl):
            buf_out[pl.ds(s, r, stride=sl), :] = y[:, s * LANES:(s + 1) * LANES]

    def scatter(buf, base):
        for g0 in range(0, r, sl):
            dsts, news = [], []
            for row in all_rows[g0:g0 + sl]:
                dst = pl.ds(pl.multiple_of(row_ref[base + row], sl), sl)
                dsts.append(dst)
                news.append(o_ref[dst, :] + gate_ref[base + row] * buf[pl.ds(row * sl, sl), :])
            for dst, new in reversed(list(zip(dsts, news))):
                o_ref[dst, :] = new

    def slow_path(k, start, n_fast, n, g_free, y_free):
        def chunk(c, carry):
            base = start + n_fast + c * r
            n_valid = jnp.minimum(n - n_fast - c * r, r)

            def slot(row):
                return jnp.minimum(base + row, n_slots - 1)

            def gather_group(gi, carry):
                for row in [gi * sl + q for q in range(sl)]:
                    src = pl.ds(pl.multiple_of(row_ref[slot(row)], sl), sl)
                    g_free[pl.ds(pl.multiple_of(row * sl, sl), sl), :] = x_ref[src, :]
                return carry

            def scatter_group(gi, carry):
                dsts, news = [], []
                for row in [gi * sl + q for q in range(sl)]:
                    dst = pl.ds(pl.multiple_of(row_ref[slot(row)], sl), sl)
                    dsts.append(dst)
                    news.append(o_ref[dst, :] + gate_ref[slot(row)] * y_free[pl.ds(pl.multiple_of(row * sl, sl), sl), :])
                for dst, new in reversed(list(zip(dsts, news))):
                    o_ref[dst, :] = new
                return carry

            lax.fori_loop(0, r // sl, gather_group, 0)
            experts(k, g_free, y_free, n_valid)
            lax.fori_loop(0, r // sl, scatter_group, 0)
            return carry

        lax.fori_loop(0, (n - n_fast + r - 1) // r, chunk, 0)

    grp = MOE_GROUP
    e0 = grp * j
    segs = [segment(e0 + k) for k in range(grp)]
    _, base_next, _, _ = segment(jnp.minimum(e0 + grp, N_EXPERTS - 1))
    _, base_prev, _, _ = segment(jnp.maximum(e0 - 1, 0))
    bases = [base_prev] + [s[1] for s in segs] + [base_next]
    bufs = [(g_a, y_a), (g_b, y_b)]

    @pl.when(j == 0)
    def _():
        o_ref[...] = jnp.zeros_like(o_ref)
        y_b[...] = jnp.zeros_like(y_b)
        gather(g_a, bases[1])

    for k in range(grp):
        g_cur, y_cur = bufs[k % 2]
        g_other, y_other = bufs[(k + 1) % 2]
        experts(k, g_cur, y_cur, segs[k][2])
        gather(g_other, bases[k + 2])
        scatter(y_other, bases[k])

    @pl.when(j == nj - 1)
    def _():
        scatter(y_b, bases[grp])

    for k in range(grp):
        start, _, n_fast, n = segs[k]

        @pl.when(n > n_fast)
        def _(k=k, start=start, n_fast=n_fast, n=n):
            slow_path(k, start, n_fast, n, g_b, y_a)


def _moe_routed(x_tiles, tok, gate, offs, w_gate, w_up, w_down, layer, n_tiles):
    _, ne, d, f = w_gate.shape
    grp = MOE_GROUP
    assert grp % 2 == 0 and ne % grp == 0
    rows = MOE_TILE * SUBLANES
    slots = MOE_TILE * TOP_K
    grid_spec = pltpu.PrefetchScalarGridSpec(
        num_scalar_prefetch=1,
        grid=(n_tiles, ne // grp),
        in_specs=[
            pl.BlockSpec((rows, LANES), lambda i, j, offs: (i, 0), pipeline_mode=pl.Buffered(1)),
            pl.BlockSpec((slots,), lambda i, j, offs: (i,), memory_space=pltpu.SMEM),
            pl.BlockSpec((slots,), lambda i, j, offs: (i,), memory_space=pltpu.SMEM),
            pl.BlockSpec((None, grp, d, f), lambda i, j, offs: (layer, j, 0, 0)),
            pl.BlockSpec((None, grp, d, f), lambda i, j, offs: (layer, j, 0, 0)),
            pl.BlockSpec((None, grp, f, d), lambda i, j, offs: (layer, j, 0, 0)),
        ],
        out_specs=pl.BlockSpec((rows, LANES), lambda i, j, offs: (i, 0), pipeline_mode=pl.Buffered(1)),
        scratch_shapes=[pltpu.VMEM((MOE_ROWS * SUBLANES, LANES), F32)] * 4,
    )
    return pl.pallas_call(
        _moe_kernel,
        grid_spec=grid_spec,
        out_shape=jax.ShapeDtypeStruct(x_tiles.shape, F32),
        compiler_params=_cparams("arbitrary", "arbitrary"),
        name="moe_routed",
    )(offs, x_tiles, tok, gate, w_gate, w_up, w_down)


def _shared_kernel(final, h_ref, routed_ref, x1_ref, mod_ref, wg_ref, wu_ref, wd_ref, fg_ref, o_ref):
    h = h_ref[...].astype(BF16)
    hid = (_silu(_dot(h, wg_ref[...])) * _dot(h, wu_ref[...])).astype(BF16)
    tm = h.shape[0]
    routed = jnp.concatenate([routed_ref[pl.ds(s, tm, stride=SUBLANES), :] for s in range(SUBLANES)], axis=1)
    x2 = x1_ref[...] + mod_ref[0, 5:6, :] * (routed + _dot(hid, wd_ref[...]))
    if final:
        x2 = x2 * lax.rsqrt(jnp.mean(x2 * x2, axis=-1, keepdims=True) + RMS_EPS) * fg_ref[...]
    o_ref[...] = x2


def _shared_residual(h2, routed, x1, mod, ws_gate, ws_up, ws_down, final_gain, final, n_lat, seq, tm=512):
    t, d = x1.shape
    n_lat_tiles = n_lat // tm
    per_batch = seq // tm
    ctx_row = mod.shape[0] - 1

    def mod_map(i):
        return (jnp.where(i < n_lat_tiles, i // per_batch, ctx_row), 0, 0)

    const = lambda i: (0, 0)
    row = lambda i: (i, 0)
    full = lambda a: pl.BlockSpec(a.shape, const)
    return pl.pallas_call(
        functools.partial(_shared_kernel, final),
        grid=(t // tm,),
        in_specs=[pl.BlockSpec((tm, d), row), pl.BlockSpec((tm * SUBLANES, LANES), row), pl.BlockSpec((tm, d), row),
                  pl.BlockSpec((1, 6, d), mod_map), full(ws_gate), full(ws_up), full(ws_down),
                  pl.BlockSpec((1, d), const)],
        out_specs=pl.BlockSpec((tm, d), row),
        out_shape=jax.ShapeDtypeStruct((t, d), F32),
        compiler_params=_cparams("parallel"),
        name="shared_residual",
    )(h2, routed, x1, mod, ws_gate, ws_up, ws_down, final_gain.reshape(1, d))


MOD_ROWS = 16


def kernel(x, c, ctx, c_ctx, w_ada, b_ada, norm1_gain, norm2_gain, w_in, na_rpb, hg_lower_bounds, hg_norm_gain,
           w_fn_branch, w_na_branch, w_hg_branch, w_out, w_router, b_router, w_exp_gate, w_exp_up, w_exp_down,
           w_sh_gate, w_sh_up, w_sh_down, final_gain):
    b, n, d = x.shape
    nc = ctx.shape[1]
    depth = w_ada.shape[0]
    n_lat, n_ctx = b * n, b * nc

    sm = jax.nn.softmax(hg_lower_bounds.astype(F32), axis=0)
    cs = jnp.cumsum(sm, axis=0)
    lower = cs - cs[0]

    cin = jnp.concatenate([c, c_ctx[None, :], jnp.zeros((MOD_ROWS - b - 1, d), F32)], axis=0)
    mod = _ada_mod(cin, w_ada, b_ada)[:, :b + 1].reshape(depth, b + 1, 6, d)
    hg_consts = _hgrn_tables()
    zero_state = jnp.zeros((b, 2, HG_WIDTH, HG_WIDTH), F32)
    wx_gate, wx_up, wx_down = w_exp_gate.astype(BF16), w_exp_up.astype(BF16), w_exp_down.astype(BF16)

    x_all = jnp.concatenate([x.reshape(n_lat, d), ctx.reshape(n_ctx, d)], axis=0)
    for l in range(depth):
        last = l == depth - 1
        w = w_in[l].astype(BF16)
        c0, c1, c2 = FN_WIDTH, FN_WIDTH + 3 * NA_WIDTH, FN_WIDTH + 3 * NA_WIDTH + 5 * HG_WIDTH
        z_fn, z_qkv, z_hg = _proj_in(x_all, mod[l], norm1_gain[l], w[:, :c0], w[:, c0:c1], w[:, c1:c2], n_lat, n)

        lbp = _hgrn_lb_params(lower[l])
        of_c, ob_c, s_ctx = _hgrn(z_hg, zero_state, lbp, hg_consts, b, nc, n_lat)
        of_l, ob_l, _ = _hgrn(z_hg, s_ctx, lbp, hg_consts, b, n, 0)
        fn_l = _fourier_lat(z_fn, b, n)
        na_l = _na_lat(z_qkv, _na_bias_table(na_rpb[l]), b, n, nc)
        if last:
            n_rows = n_lat
            fn_a, na_a, of_a, ob_a = fn_l, na_l, of_l, ob_l
        else:
            n_rows = n_lat + n_ctx
            cat = lambda p, q: jnp.concatenate([p, q], axis=0)
            fn_a = jnp.concatenate([fn_l, _fourier_ctx(z_fn, b, nc, n_lat)], axis=1)
            na_a = cat(na_l, _na_ctx(z_qkv, b, n, nc))
            of_a, ob_a = cat(of_l, of_c), cat(ob_l, ob_c)
        x1, h2, h2_tiles = _merge(x_all, mod[l], norm1_gain[l], norm2_gain[l], fn_a, na_a, of_a, ob_a, z_hg, hg_norm_gain[l],
                        w_fn_branch[l].astype(BF16), w_na_branch[l].astype(BF16), w_hg_branch[l].astype(BF16),
                        w[:, c2:], w_out[l].astype(BF16), n_rows, n_lat, n)

        idx, gate, counts = _router(h2, w_router[l], b_router[l])
        n_tiles = -(-n_rows // MOE_TILE)
        tok, gate_sorted, offs = _moe_dispatch(idx[:, :TOP_K], gate[:, :TOP_K], counts[::SUBLANES], n_tiles)
        routed = _moe_routed(h2_tiles, tok, gate_sorted, offs, wx_gate, wx_up, wx_down, l, n_tiles)
        x_all = _shared_residual(h2, routed, x1, mod[l], w_sh_gate[l].astype(BF16),
                                 w_sh_up[l].astype(BF16), w_sh_down[l].astype(BF16), final_gain, last, n_lat, n)
    return x_all.reshape(b, n, d)
```

```python
import functools

import numpy as np
import jax
import jax.numpy as jnp
from jax import lax
from jax.experimental import pallas as pl
from jax.experimental.pallas import tpu as pltpu

D_MODEL = 1024
GRID_W = 64
RMS_EPS = 1e-6
FN_GROUPS = 4
FN_GROUP_DIM = 64
FN_WIDTH = 256
NA_HEADS = 8
NA_HEAD_DIM = 64
NA_WIDTH = 512
NA_WIN_ROWS = 8
NA_WIN_COLS = 16
HG_HEADS = 4
HG_HEAD_DIM = 64
HG_WIDTH = 256
N_EXPERTS = 256
TOP_K = 8
EXPERT_DIM = 256
ROUTE_SCALE = 2.5

LANES = 128
SUBLANES = 8

F32 = jnp.float32
BF16 = jnp.bfloat16
VMEM_LIMIT = 56 * 1024 * 1024


def _cparams(*sem):
    return pltpu.CompilerParams(dimension_semantics=sem, vmem_limit_bytes=VMEM_LIMIT)


def _dot(a, b):
    return jnp.dot(a, b, preferred_element_type=F32)


def _dot_nt(a, b):
    return lax.dot_general(a, b, (((1,), (1,)), ((), ())), preferred_element_type=F32)


def _sigmoid(z):
    e = jnp.exp(-jnp.abs(z))
    r = 1.0 / (1.0 + e)
    return jnp.where(z >= 0, r, e * r)


def _silu(z):
    return z * _sigmoid(z)


def _norm_mod(x, gain, shift, scale):
    y = x * lax.rsqrt(jnp.mean(x * x, axis=-1, keepdims=True) + RMS_EPS)
    return (y * gain) * (1.0 + scale) + shift


def _ada_kernel(c_ref, w_ref, b_ref, o_ref):
    s = _silu(c_ref[...]).astype(BF16)
    o_ref[0] = _dot(s, w_ref[0].astype(BF16)) + b_ref[0]


def _ada_mod(cin, w_ada, b_ada):
    depth, d, n = w_ada.shape
    r = cin.shape[0]
    tn = 1536
    return pl.pallas_call(
        _ada_kernel,
        grid=(depth, n // tn),
        in_specs=[
            pl.BlockSpec((r, d), lambda l, j: (0, 0)),
            pl.BlockSpec((1, d, tn), lambda l, j: (l, 0, j)),
            pl.BlockSpec((1, 1, tn), lambda l, j: (l, 0, j)),
        ],
        out_specs=pl.BlockSpec((1, r, tn), lambda l, j: (l, 0, j)),
        out_shape=jax.ShapeDtypeStruct((depth, r, n), F32),
        compiler_params=_cparams("parallel", "parallel"),
        name="ada_mod",
    )(cin, w_ada, b_ada.reshape(depth, 1, n))


def _proj_in_kernel(x_ref, mod_ref, gain_ref, wfn_ref, wqkv_ref, whg_ref, fn_ref, qkv_ref, hg_ref):
    h = _norm_mod(x_ref[...], gain_ref[...], mod_ref[0, 0:1, :], mod_ref[0, 1:2, :]).astype(BF16)
    fn_ref[...] = _dot(h, wfn_ref[...])
    qkv_ref[...] = _dot(h, wqkv_ref[...]).astype(BF16)
    hg_ref[...] = _dot(h, whg_ref[...])


def _proj_in(x_all, mod, gain, w_fn, w_qkv, w_hg, n_lat, seq, tm=512):
    t, d = x_all.shape
    n_lat_tiles = n_lat // tm
    per_batch = seq // tm
    ctx_row = mod.shape[0] - 1

    def mod_map(i):
        return (jnp.where(i < n_lat_tiles, i // per_batch, ctx_row), 0, 0)

    const = lambda i: (0, 0)
    row = lambda i: (i, 0)
    return pl.pallas_call(
        _proj_in_kernel,
        grid=(t // tm,),
        in_specs=[
            pl.BlockSpec((tm, d), row),
            pl.BlockSpec((1, 6, d), mod_map),
            pl.BlockSpec((1, d), const),
            pl.BlockSpec(w_fn.shape, const),
            pl.BlockSpec(w_qkv.shape, const),
            pl.BlockSpec(w_hg.shape, const),
        ],
        out_specs=[
            pl.BlockSpec((tm, w_fn.shape[1]), row),
            pl.BlockSpec((tm, w_qkv.shape[1]), row),
            pl.BlockSpec((tm, w_hg.shape[1]), row),
        ],
        out_shape=[
            jax.ShapeDtypeStruct((t, w_fn.shape[1]), F32),
            jax.ShapeDtypeStruct((t, w_qkv.shape[1]), BF16),
            jax.ShapeDtypeStruct((t, w_hg.shape[1]), F32),
        ],
        compiler_params=_cparams("parallel"),
        name="proj_in",
    )(x_all, mod, gain.reshape(1, d), w_fn, w_qkv, w_hg)


def _dft_cos_sin(n):
    k = np.arange(n)
    ang = 2.0 * np.pi * ((k[:, None] * k[None, :]) % n) / n
    return np.cos(ang), np.sin(ang)


def _channel_dft_mats(scale):
    c, s = _dft_cos_sin(FN_GROUP_DIM)
    eye = np.eye(FN_GROUPS)
    return (jnp.asarray(np.kron(eye, c) * scale, BF16), jnp.asarray(np.kron(eye, -s) * scale, BF16))


FFT_R = 64


def _store_halves(ref, lead, rows, val):
    ref[lead + (0, rows, slice(None))] = val[:, :LANES]
    ref[lead + (1, rows, slice(None))] = val[:, LANES:]


def _load_halves(ref, lead, rows):
    return jnp.concatenate([ref[lead + (0, rows, slice(None))], ref[lead + (1, rows, slice(None))]], axis=1)


def _fourier_lat_kernel(u_ref, cc_ref, sc_ref, ga_ref, gb_ref, tc_ref, ts_ref, o_ref, u_s, z_s):
    ub = u_ref[...].astype(BF16)
    every = slice(None)
    _store_halves(u_s, (0,), every, _dot(ub, cc_ref[...]))
    _store_halves(u_s, (1,), every, _dot(ub, sc_ref[...]))

    def stage_a(n2, carry):
        rows = pl.ds(n2, FFT_R, stride=FFT_R)
        x = jnp.concatenate([_load_halves(u_s, (0,), rows), _load_halves(u_s, (1,), rows)], axis=0).astype(BF16)
        a = _dot(ga_ref[...], x)
        ar, ai = a[:FFT_R], a[FFT_R:]
        tc = jnp.concatenate([tc_ref[n2], tc_ref[n2]], axis=1)
        ts = jnp.concatenate([ts_ref[n2], ts_ref[n2]], axis=1)
        dst = pl.ds(pl.multiple_of(n2 * FFT_R, FFT_R), FFT_R)
        _store_halves(z_s, (0,), dst, ar * tc + ai * ts)
        _store_halves(z_s, (1,), dst, ai * tc - ar * ts)
        return carry

    lax.fori_loop(0, FFT_R, stage_a, 0)

    def stage_b(k1, carry):
        rows = pl.ds(k1, FFT_R, stride=FFT_R)
        z = jnp.concatenate([_load_halves(z_s, (0,), rows), _load_halves(z_s, (1,), rows)], axis=0).astype(BF16)
        _store_halves(o_ref, (), rows, _dot(gb_ref[...], z))
        return carry

    lax.fori_loop(0, FFT_R, stage_b, 0)


def _fourier_lat(z_fn, n_batch, seq):
    assert seq == FFT_R * FFT_R
    c, s = _dft_cos_sin(FFT_R)
    cc, sc = _channel_dft_mats((seq * FN_GROUP_DIM) ** -0.5)
    ga = jnp.asarray(np.block([[c, s], [-s, c]]), BF16)
    gb = jnp.asarray(np.concatenate([c, s], axis=1), BF16)
    k = np.arange(FFT_R)
    ang = 2.0 * np.pi * (k[:, None] * k[None, :]) / seq
    tc = jnp.asarray(np.broadcast_to(np.cos(ang)[:, :, None], (FFT_R, FFT_R, 128)), F32)
    ts = jnp.asarray(np.broadcast_to(np.sin(ang)[:, :, None], (FFT_R, FFT_R, 128)), F32)
    const2 = lambda b: (0, 0)
    const3 = lambda b: (0, 0, 0)
    return pl.pallas_call(
        _fourier_lat_kernel,
        grid=(n_batch,),
        in_specs=[
            pl.BlockSpec((seq, FN_WIDTH), lambda b: (b, 0)),
            pl.BlockSpec(cc.shape, const2),
            pl.BlockSpec(sc.shape, const2),
            pl.BlockSpec(ga.shape, const2),
            pl.BlockSpec(gb.shape, const2),
            pl.BlockSpec(tc.shape, const3),
            pl.BlockSpec(ts.shape, const3),
        ],
        out_specs=pl.BlockSpec((2, seq, LANES), lambda b: (0, b, 0)),
        out_shape=jax.ShapeDtypeStruct((2, n_batch * seq, LANES), F32),
        scratch_shapes=[pltpu.VMEM((2, 2, seq, LANES), F32), pltpu.VMEM((2, 2, seq, LANES), F32)],
        compiler_params=_cparams("parallel"),
        name="fourier_lat",
    )(z_fn, cc, sc, ga, gb, tc, ts)


def _fourier_ctx_kernel(u_ref, cc_ref, sc_ref, g_ref, o_ref):
    ub = u_ref[...].astype(BF16)
    x = jnp.concatenate([_dot(ub, cc_ref[...]), _dot(ub, sc_ref[...])], axis=0).astype(BF16)
    _store_halves(o_ref, (), slice(None), _dot(g_ref[...], x))


def _fourier_ctx(z_fn, n_batch, n_ctx, row0):
    blk0 = row0 // n_ctx
    c, s = _dft_cos_sin(n_ctx)
    cc, sc = _channel_dft_mats((n_ctx * FN_GROUP_DIM) ** -0.5)
    g = jnp.asarray(np.concatenate([c, s], axis=1), BF16)
    const2 = lambda b: (0, 0)
    return pl.pallas_call(
        _fourier_ctx_kernel,
        grid=(n_batch,),
        in_specs=[
            pl.BlockSpec((n_ctx, FN_WIDTH), lambda b: (blk0 + b, 0)),
            pl.BlockSpec(cc.shape, const2),
            pl.BlockSpec(sc.shape, const2),
            pl.BlockSpec(g.shape, const2),
        ],
        out_specs=pl.BlockSpec((2, n_ctx, LANES), lambda b: (0, b, 0)),
        out_shape=jax.ShapeDtypeStruct((2, n_batch * n_ctx, LANES), F32),
        compiler_params=_cparams("parallel"),
        name="fourier_ctx",
    )(z_fn, cc, sc, g)


NEG_BIG = -1e30
HEAD_PAIR = 2 * NA_HEAD_DIM


def _na_bias_table(rpb):
    wr, wc, w = NA_WIN_ROWS, NA_WIN_COLS, GRID_W
    col = np.arange(w)
    col_start = np.clip(col - wc // 2, 0, w - wc)
    in_win = (col[None, :] >= col_start[:, None]) & (col[None, :] < col_start[:, None] + wc)
    dc_idx = np.clip(col[None, :] - col[:, None], 1 - wc, wc - 1) + (wc - 1)
    h, n_dr, n_dc = rpb.shape
    onehot = jnp.asarray(dc_idx.reshape(-1)[None, :] == np.arange(n_dc)[:, None], F32)
    t = jnp.dot(rpb.astype(F32).reshape(h * n_dr, n_dc), onehot, precision=lax.Precision.HIGHEST)
    t = jnp.where(in_win.reshape(-1)[None, :], t, NEG_BIG).reshape(h, n_dr, w, w)
    per_s = [jnp.concatenate([t[:, s + j] for j in range(wr)], axis=-1) for s in range(wr)]
    return jnp.stack(per_s, axis=1)


def _softmax_pv(q, keys, vals, biases):
    s = []
    for k, b in zip(keys, biases):
        si = _dot_nt(q, k)
        s.append(si if b is None else si + b)
    m = s[0].max(axis=-1, keepdims=True)
    for si in s[1:]:
        m = jnp.maximum(m, si.max(axis=-1, keepdims=True))
    acc = None
    l = None
    for si, v in zip(s, vals):
        p = jnp.exp(si - m)
        li = p.sum(axis=-1, keepdims=True)
        oi = _dot(p.astype(BF16), v)
        acc = oi if acc is None else acc + oi
        l = li if l is None else l + li
    return acc / l


def _na_lat_kernel(q_ref, k_ref, v_ref, kc_ref, vc_ref, bias_ref, o_ref):
    w = GRID_W
    n_loc = NA_WIN_ROWS * w
    rows = q_ref.shape[0] // w
    lane = lax.broadcasted_iota(jnp.int32, (w, HEAD_PAIR), 1)
    first = lane < NA_HEAD_DIM
    kc = kc_ref[...]
    vc = vc_ref[...]

    def body(r, carry):
        kr0 = jnp.clip(r - NA_WIN_ROWS // 2, 0, rows - NA_WIN_ROWS)
        s = kr0 - r + (NA_WIN_ROWS - 1)
        q = q_ref[pl.ds(pl.multiple_of(r * w, w), w), :] * jnp.asarray(NA_HEAD_DIM ** -0.5, BF16)
        ks = k_ref[pl.ds(pl.multiple_of(kr0 * w, w), n_loc), :]
        vs = v_ref[pl.ds(pl.multiple_of(kr0 * w, w), n_loc), :]
        zero = jnp.zeros_like(q)
        q2 = jnp.concatenate([jnp.where(first, q, zero), jnp.where(first, zero, q)], axis=0)
        bias = jnp.concatenate([bias_ref[0, s], bias_ref[1, s]], axis=0)
        o2 = _softmax_pv(q2, [ks, kc], [vs, vc], [bias, None])
        o_ref[pl.ds(pl.multiple_of(r * w, w), w), :] = jnp.where(first, o2[:w], o2[w:]).astype(o_ref.dtype)
        return carry

    lax.fori_loop(0, rows, body, 0, unroll=4)


def _na_lat(qkv, bias, n_batch, seq, n_ctx):
    n_pairs = NA_WIDTH // HEAD_PAIR
    ctx0 = n_batch * seq // n_ctx
    wr, w = NA_WIN_ROWS, GRID_W
    return pl.pallas_call(
        _na_lat_kernel,
        grid=(n_batch, n_pairs),
        in_specs=[
            pl.BlockSpec((seq, HEAD_PAIR), lambda b, p: (b, p)),
            pl.BlockSpec((seq, HEAD_PAIR), lambda b, p: (b, n_pairs + p)),
            pl.BlockSpec((seq, HEAD_PAIR), lambda b, p: (b, 2 * n_pairs + p)),
            pl.BlockSpec((n_ctx, HEAD_PAIR), lambda b, p: (ctx0 + b, n_pairs + p)),
            pl.BlockSpec((n_ctx, HEAD_PAIR), lambda b, p: (ctx0 + b, 2 * n_pairs + p)),
            pl.BlockSpec((2, wr, w, wr * w), lambda b, p: (p, 0, 0, 0)),
        ],
        out_specs=pl.BlockSpec((seq, HEAD_PAIR), lambda b, p: (b, p)),
        out_shape=jax.ShapeDtypeStruct((n_batch * seq, NA_WIDTH), BF16),
        compiler_params=_cparams("parallel", "parallel"),
        name="na_lat",
    )(qkv, qkv, qkv, qkv, qkv, bias)


def _na_ctx_kernel(q_ref, k_ref, v_ref, o_ref):
    lane = lax.broadcasted_iota(jnp.int32, q_ref.shape, 1)
    first = lane < NA_HEAD_DIM
    q = q_ref[...] * jnp.asarray(NA_HEAD_DIM ** -0.5, BF16)
    k = k_ref[...]
    v = v_ref[...]
    zero = jnp.zeros_like(q)
    o0 = _softmax_pv(jnp.where(first, q, zero), [k], [v], [None])
    o1 = _softmax_pv(jnp.where(first, zero, q), [k], [v], [None])
    o_ref[...] = jnp.where(first, o0, o1).astype(o_ref.dtype)


def _na_ctx(qkv, n_batch, seq, n_ctx):
    n_pairs = NA_WIDTH // HEAD_PAIR
    ctx0 = n_batch * seq // n_ctx
    return pl.pallas_call(
        _na_ctx_kernel,
        grid=(n_batch, n_pairs),
        in_specs=[
            pl.BlockSpec((n_ctx, HEAD_PAIR), lambda b, p: (ctx0 + b, p)),
            pl.BlockSpec((n_ctx, HEAD_PAIR), lambda b, p: (ctx0 + b, n_pairs + p)),
            pl.BlockSpec((n_ctx, HEAD_PAIR), lambda b, p: (ctx0 + b, 2 * n_pairs + p)),
        ],
        out_specs=pl.BlockSpec((n_ctx, HEAD_PAIR), lambda b, p: (b, p)),
        out_shape=jax.ShapeDtypeStruct((n_batch * n_ctx, NA_WIDTH), BF16),
        compiler_params=_cparams("parallel", "parallel"),
        name="na_ctx",
    )(qkv, qkv, qkv)


HG_CHUNK = 64
HG_LEVELS = (32, 16)
HG_DIAG = 16
HG_BATCH = 2


def _hgrn_tables():
    c = HG_CHUNK
    tri, masks, refs, last = [], [], [], []
    for reverse in (False, True):
        u = np.arange(c)[::-1] if reverse else np.arange(c)
        row_of = {int(uu): t for t, uu in enumerate(u)}
        ut, uj = u[:, None], u[None, :]
        tri.append(uj <= ut)
        lv, rf = [], []
        for h in HG_LEVELS:
            same = (ut // (2 * h)) == (uj // (2 * h))
            up_t, up_j = (ut % (2 * h)) >= h, (uj % (2 * h)) >= h
            lv.append(same & up_t & ~up_j)
            rf.append([row_of[int(uu) // (2 * h) * (2 * h) + h - 1] for uu in u])
        same_d = (ut // HG_DIAG) == (uj // HG_DIAG)
        lv.append(same_d & (uj <= ut))
        rf.append([row_of.get(int(uu) // HG_DIAG * HG_DIAG - 1, -1) for uu in u])
        masks.append(np.stack([np.tile(x, (1, HG_HEADS)) for x in lv]).astype(np.float32))
        refs.append(rf)
        last.append(row_of[c - 1])
    hm = (np.arange(HG_WIDTH)[:, None] // HG_HEAD_DIM) == (np.arange(HG_WIDTH)[None, :] // HG_HEAD_DIM)
    arrays = (jnp.asarray(np.stack(tri), BF16), jnp.asarray(np.stack(masks), F32), jnp.asarray(hm, F32))
    return arrays, refs, last


def _ref_rows(b, ref):
    pieces, t = [], 0
    while t < len(ref):
        t1 = t
        while t1 < len(ref) and ref[t1] == ref[t]:
            t1 += 1
        shape = (t1 - t, b.shape[1])
        pieces.append(jnp.zeros(shape, b.dtype) if ref[t] < 0 else jnp.broadcast_to(b[ref[t]:ref[t] + 1, :], shape))
        t = t1
    return jnp.concatenate(pieces, axis=0)


def _split3(x):
    hi = x.astype(BF16)
    r = x - hi.astype(F32)
    mid = r.astype(BF16)
    lo = (r - mid.astype(F32)).astype(BF16)
    return hi, mid, lo


def _hgrn_chunk(zq, zv, zf, lbp, tri, masks, hm, st, refs, last_row):
    q = _silu(zq)
    e = jnp.exp(-jnp.abs(zf))
    inv = 1.0 / (1.0 + e)
    log_sig = jnp.minimum(zf, 0.0) - jnp.log(1.0 + e)
    sig_neg = jnp.where(zf >= 0, e * inv, inv)
    a = lbp[0:1, :]
    cc = lbp[1:2, :] + log_sig
    log_f = jnp.maximum(a, cc) + jnp.log(1.0 + jnp.exp(-jnp.abs(a - cc)))
    k = lbp[2:3, :] * sig_neg

    hi, mid, lo = _split3(log_f)
    b = _dot(tri, hi) + _dot(tri, mid) + _dot(tri, lo)
    b_last = b[last_row:last_row + 1]

    def heads_bd(x):
        return (jnp.concatenate([x] * HG_HEADS, axis=0) * hm).astype(BF16)

    a_all = None
    for i, ref in enumerate(refs):
        rel = b - _ref_rows(b, ref)
        if i < len(HG_LEVELS):
            qx, kx = q * jnp.exp(jnp.minimum(rel, 0.0)), k * jnp.exp(jnp.minimum(-rel, 0.0))
        else:
            qx, kx = q * jnp.exp(rel), k * jnp.exp(-rel)
        ai = _dot_nt(qx.astype(BF16), heads_bd(kx)) * masks[i]
        a_all = ai if a_all is None else a_all + ai
    o = _dot(a_all.astype(BF16), heads_bd(zv)) + _dot_nt((q * jnp.exp(b)).astype(BF16), st.astype(BF16))
    kl = (k * jnp.exp(b_last - b)).astype(BF16)
    upd = lax.dot_general(zv.astype(BF16), kl, (((0,), (0,)), ((), ())), preferred_element_type=F32)
    st_new = st * jnp.exp(b_last) + upd * hm
    return o, st_new


def _hgrn_kernel(refs, last, *args):
    nb = HG_BATCH
    z_refs = args[:3 * nb]
    s0_ref, lbp_ref, tri_ref, mask_ref, hm_ref = args[3 * nb:3 * nb + 5]
    of_ref, ob_ref, sfin_ref, st = args[3 * nb + 5:]
    i = pl.program_id(1)

    @pl.when(i == 0)
    def _():
        st[...] = s0_ref[...]

    w = HG_WIDTH
    hm = hm_ref[...]
    for bb in range(nb):
        zf_ref, zb_ref, zbg_ref = z_refs[3 * bb:3 * bb + 3]
        o, s = _hgrn_chunk(zf_ref[:, 0:w], zf_ref[:, w:2 * w], zf_ref[:, 2 * w:3 * w], lbp_ref[0],
                           tri_ref[0], mask_ref[0], hm, st[bb, 0], refs[0], last[0])
        of_ref[bb] = o
        st[bb, 0] = s
        o, s = _hgrn_chunk(zb_ref[:, 0:w], zb_ref[:, w:2 * w], zbg_ref[...], lbp_ref[1],
                           tri_ref[1], mask_ref[1], hm, st[bb, 1], refs[1], last[1])
        ob_ref[bb] = o
        st[bb, 1] = s

    @pl.when(i == pl.num_programs(1) - 1)
    def _():
        sfin_ref[...] = st[...]


def _hgrn(z_hg, s0, lbp, tables, n_batch, n_tok, row0):
    (tri, masks, hm), refs, last = tables
    c, w, nb = HG_CHUNK, HG_WIDTH, HG_BATCH
    nch = n_tok // c
    base = row0 // c
    const3 = lambda b, i: (0, 0, 0)
    in_specs = []
    for bb in range(nb):
        fwd = lambda b, i, bb=bb: (base + (b * nb + bb) * nch + i, 0)
        bwd = lambda b, i, bb=bb: (base + (b * nb + bb) * nch + (nch - 1 - i), 0)
        bwd_gate = lambda b, i, bb=bb: (base + (b * nb + bb) * nch + (nch - 1 - i), 3)
        in_specs += [pl.BlockSpec((c, 3 * w), fwd), pl.BlockSpec((c, 2 * w), bwd), pl.BlockSpec((c, w), bwd_gate)]
    in_specs += [
        pl.BlockSpec((nb, 2, w, w), lambda b, i: (b, 0, 0, 0)),
        pl.BlockSpec(lbp.shape, const3),
        pl.BlockSpec(tri.shape, const3),
        pl.BlockSpec(masks.shape, lambda b, i: (0, 0, 0, 0)),
        pl.BlockSpec(hm.shape, lambda b, i: (0, 0)),
    ]
    o_f, o_b, s_fin = pl.pallas_call(
        functools.partial(_hgrn_kernel, refs, last),
        grid=(n_batch // nb, nch),
        in_specs=in_specs,
        out_specs=[
            pl.BlockSpec((nb, c, w), lambda b, i: (b, i, 0)),
            pl.BlockSpec((nb, c, w), lambda b, i: (b, nch - 1 - i, 0)),
            pl.BlockSpec((nb, 2, w, w), lambda b, i: (b, 0, 0, 0)),
        ],
        out_shape=[
            jax.ShapeDtypeStruct((n_batch, n_tok, w), F32),
            jax.ShapeDtypeStruct((n_batch, n_tok, w), F32),
            jax.ShapeDtypeStruct((n_batch, 2, w, w), F32),
        ],
        scratch_shapes=[pltpu.VMEM((nb, 2, w, w), F32)],
        compiler_params=_cparams("parallel", "arbitrary"),
        name="hgrn",
    )(*([z_hg] * (3 * nb)), s0, lbp, tri, masks, hm)
    return o_f.reshape(n_batch * n_tok, w), o_b.reshape(n_batch * n_tok, w), s_fin


def _hgrn_lb_params(lower_l):
    rows = jnp.stack([jnp.log(lower_l), jnp.log1p(-lower_l), 1.0 - lower_l], axis=1)
    return jnp.pad(rows, ((0, 0), (0, 5), (0, 0)))


def _merge_kernel(x_ref, mod_ref, g1_ref, g2_ref, fn_ref, na_ref, of_ref, ob_ref, hgz_ref, hgain_ref,
                  hmean_ref, wfn_ref, wna_ref, whg_ref, wbg_ref, wout_ref, x1_ref, h2_ref, h2t_ref):
    d = x_ref.shape[1]
    x = x_ref[...]
    h = _norm_mod(x, g1_ref[...], mod_ref[0, 0:1, :], mod_ref[0, 1:2, :]).astype(BF16)
    o = of_ref[...] + ob_ref[...]
    hi, mid, lo = _split3(o * o)
    ms = _dot(hi, hmean_ref[...]) + _dot(mid, hmean_ref[...]) + _dot(lo, hmean_ref[...])
    y_hg_in = o * lax.rsqrt(ms + RMS_EPS) * hgain_ref[...] * _silu(hgz_ref[...])
    y_fn = _dot(jnp.concatenate([fn_ref[0], fn_ref[1]], axis=1).astype(BF16), wfn_ref[...])
    y_na = _dot(na_ref[...], wna_ref[...])
    y_hg = _dot(y_hg_in.astype(BF16), whg_ref[...])
    m = (_sigmoid(_dot(h, wbg_ref[:, 0:d])) * y_fn
         + _sigmoid(_dot(h, wbg_ref[:, d:2 * d])) * y_na
         + _sigmoid(_dot(h, wbg_ref[:, 2 * d:3 * d])) * y_hg)
    x1 = x + mod_ref[0, 2:3, :] * _dot(m.astype(BF16), wout_ref[...])
    x1_ref[...] = x1
    h2 = _norm_mod(x1, g2_ref[...], mod_ref[0, 3:4, :], mod_ref[0, 4:5, :])
    h2_ref[...] = h2
    tm = x.shape[0]
    for s in range(SUBLANES):
        h2t_ref[pl.ds(s, tm, stride=SUBLANES), :] = h2[:, s * LANES:(s + 1) * LANES]


def _merge(x_all, mod, gain1, gain2, fn, na, o_f, o_b, z_hg, hgain, w_fnb, w_nab, w_hgb, w_bg, w_out,
           n_rows, n_lat, seq, tm=256):
    t, d = n_rows, x_all.shape[1]
    n_lat_tiles = n_lat // tm
    per_batch = seq // tm
    ctx_row = mod.shape[0] - 1
    w = HG_WIDTH
    hmean = jnp.asarray(np.kron(np.eye(HG_HEADS), np.full((HG_HEAD_DIM, HG_HEAD_DIM), 1.0 / HG_HEAD_DIM)), BF16)

    def mod_map(i):
        return (jnp.where(i < n_lat_tiles, i // per_batch, ctx_row), 0, 0)

    const = lambda i: (0, 0)
    row = lambda i: (i, 0)
    full = lambda a: pl.BlockSpec(a.shape, const)
    return pl.pallas_call(
        _merge_kernel,
        grid=(t // tm,),
        in_specs=[
            pl.BlockSpec((tm, d), row),
            pl.BlockSpec((1, 6, d), mod_map),
            pl.BlockSpec((1, d), const),
            pl.BlockSpec((1, d), const),
            pl.BlockSpec((2, tm, LANES), lambda i: (0, i, 0)),
            pl.BlockSpec((tm, NA_WIDTH), row),
            pl.BlockSpec((tm, w), row),
            pl.BlockSpec((tm, w), row),
            pl.BlockSpec((tm, w), lambda i: (i, 4)),
            pl.BlockSpec((1, w), const),
            full(hmean), full(w_fnb), full(w_nab), full(w_hgb), full(w_bg), full(w_out),
        ],
        out_specs=[pl.BlockSpec((tm, d), row), pl.BlockSpec((tm, d), row), pl.BlockSpec((tm * SUBLANES, LANES), row)],
        out_shape=[jax.ShapeDtypeStruct((t, d), F32), jax.ShapeDtypeStruct((t, d), F32),
                   jax.ShapeDtypeStruct((t * SUBLANES, LANES), F32)],
        compiler_params=_cparams("parallel"),
        name="merge",
    )(x_all, mod, gain1.reshape(1, d), gain2.reshape(1, d), fn, na, o_f, o_b, z_hg,
      jnp.tile(hgain, HG_HEADS).reshape(1, w), hmean, w_fnb, w_nab, w_hgb, w_bg, w_out)


def _router_kernel(h_ref, whi_ref, wlo_ref, b_ref, idx_ref, gate_ref, cnt_ref):
    h = h_ref[...]
    hi = h.astype(BF16)
    lo = (h - hi.astype(F32)).astype(BF16)
    logits = _dot(hi, whi_ref[...]) + (_dot(hi, wlo_ref[...]) + _dot(lo, whi_ref[...]))
    scores = _sigmoid(logits)
    sel = scores + b_ref[...]
    tm, ne = sel.shape
    col = lax.broadcasted_iota(jnp.int32, (tm, ne), 1)
    out_lane = lax.broadcasted_iota(jnp.int32, (tm, LANES), 1)
    idx_out = jnp.zeros((tm, LANES), jnp.int32)
    gate_out = jnp.zeros((tm, LANES), F32)
    total = jnp.zeros((tm, 1), F32)
    chosen = jnp.zeros((tm, ne), F32)
    for k in range(TOP_K):
        m = sel.max(axis=-1, keepdims=True)
        idx = jnp.where(sel == m, col, ne).min(axis=-1, keepdims=True)
        hit = col == idx
        g = jnp.where(hit, scores, 0.0).sum(axis=-1, keepdims=True)
        sel = jnp.where(hit, -jnp.inf, sel)
        chosen = jnp.where(hit, 1.0, chosen)
        idx_out = jnp.where(out_lane == k, idx, idx_out)
        gate_out = jnp.where(out_lane == k, g, gate_out)
        total = total + g
    idx_ref[...] = idx_out
    gate_ref[...] = gate_out * (ROUTE_SCALE / total)
    cnt_ref[...] = jnp.broadcast_to(chosen.sum(axis=0, keepdims=True), cnt_ref.shape).astype(jnp.int32)


def _router(h2, w_router, b_router, tm=512):
    t, d = h2.shape
    ne = w_router.shape[1]
    w_hi = w_router.astype(BF16)
    w_lo = (w_router - w_hi.astype(F32)).astype(BF16)
    const = lambda i: (0, 0)
    row = lambda i: (i, 0)
    return pl.pallas_call(
        _router_kernel,
        grid=(t // tm,),
        in_specs=[pl.BlockSpec((tm, d), row), pl.BlockSpec((d, ne), const), pl.BlockSpec((d, ne), const),
                  pl.BlockSpec((1, ne), const)],
        out_specs=[pl.BlockSpec((tm, LANES), row), pl.BlockSpec((tm, LANES), row), pl.BlockSpec((SUBLANES, ne), row)],
        out_shape=[jax.ShapeDtypeStruct((t, LANES), jnp.int32), jax.ShapeDtypeStruct((t, LANES), F32),
                   jax.ShapeDtypeStruct((t // tm * SUBLANES, ne), jnp.int32)],
        compiler_params=_cparams("parallel"),
        name="router",
    )(h2, w_hi, w_lo, b_router.reshape(1, ne).astype(F32))


MOE_TILE = 4096
MOE_ROWS = 160
MOE_GROUP = 4
MXU_DIM = 256


def _moe_dispatch(idx, gate, counts, n_tiles):
    t, k = idx.shape
    pad = n_tiles * MOE_TILE - t
    slots = MOE_TILE * k
    e = jnp.pad(idx, ((0, pad), (0, 0)), constant_values=N_EXPERTS).reshape(n_tiles, slots)
    g = jnp.pad(gate, ((0, pad), (0, 0))).reshape(n_tiles, slots)
    key = e * slots + jnp.arange(slots, dtype=jnp.int32)[None, :]
    key_sorted, g_sorted = lax.sort((key, g), dimension=1, num_keys=1)
    row8_sorted = (key_sorted % slots) // k * SUBLANES
    group = MOE_TILE // (t // counts.shape[0])
    c = jnp.pad(counts, ((0, n_tiles * group - counts.shape[0]), (0, 0))).reshape(n_tiles, group, -1).sum(axis=1)
    offs = jnp.concatenate([jnp.zeros((n_tiles, 1), jnp.int32), jnp.cumsum(c, axis=1)], axis=1)
    offs = jnp.pad(offs.astype(jnp.int32).reshape(-1), (0, SUBLANES))
    return row8_sorted.reshape(-1), g_sorted.reshape(-1), offs


def _moe_kernel(offs_ref, x_ref, row_ref, gate_ref, wg_ref, wu_ref, wd_ref, o_ref, g_a, g_b, y_a, y_b):
    i = pl.program_id(0)
    j = pl.program_id(1)
    nj = pl.num_programs(1)
    r, sl = MOE_ROWS, SUBLANES
    n_slots = row_ref.shape[0]
    all_rows = list(range(r))

    def segment(ee):
        start = offs_ref[i * (N_EXPERTS + 1) + ee]
        n = offs_ref[i * (N_EXPERTS + 1) + ee + 1] - start
        n_fast = jnp.where(start <= n_slots - r, jnp.minimum(n, r), 0)
        return start, jnp.minimum(start, n_slots - r), n_fast, n

    def gather(buf, base):
        for row in all_rows:
            buf[pl.ds(row * sl, sl), :] = x_ref[pl.ds(pl.multiple_of(row_ref[base + row], sl), sl), :]

    def experts(k, buf_in, buf_out, n_valid):
        x = jnp.concatenate([buf_in[pl.ds(s, r, stride=sl), :] for s in range(sl)], axis=1).astype(BF16)
        x = jnp.concatenate([x, jnp.zeros((MXU_DIM - r, x.shape[1]), BF16)], axis=0)
        hid_t = (_silu(_dot_nt(wg_ref[k], x)) * _dot_nt(wu_ref[k], x)).astype(BF16)
        y = _dot(wd_ref[k], hid_t).T[:r]
        y = jnp.where(lax.broadcasted_iota(jnp.int32, (r, 1), 0) < n_valid, y, 0.0)
        for s in range(sl):
            buf_out[pl.ds(s, r, stride=sl), :] = y[:, s * LANES:(s + 1) * LANES]

    def scatter(buf, base):
        for g0 in range(0, r, sl):
            dsts, news = [], []
            for row in all_rows[g0:g0 + sl]:
                dst = pl.ds(pl.multiple_of(row_ref[base + row], sl), sl)
                dsts.append(dst)
                news.append(o_ref[dst, :] + gate_ref[base + row] * buf[pl.ds(row * sl, sl), :])
            for dst, new in reversed(list(zip(dsts, news))):
                o_ref[dst, :] = new

    def slow_path(k, start, n_fast, n, g_free, y_free):
        def chunk(c, carry):
            base = start + n_fast + c * r
            n_valid = jnp.minimum(n - n_fast - c * r, r)

            def slot(row):
                return jnp.minimum(base + row, n_slots - 1)

            def gather_group(gi, carry):
                for row in [gi * sl + q for q in range(sl)]:
                    src = pl.ds(pl.multiple_of(row_ref[slot(row)], sl), sl)
                    g_free[pl.ds(pl.multiple_of(row * sl, sl), sl), :] = x_ref[src, :]
                return carry

            def scatter_group(gi, carry):
                dsts, news = [], []
                for row in [gi * sl + q for q in range(sl)]:
                    dst = pl.ds(pl.multiple_of(row_ref[slot(row)], sl), sl)
                    dsts.append(dst)
                    news.append(o_ref[dst, :] + gate_ref[slot(row)] * y_free[pl.ds(pl.multiple_of(row * sl, sl), sl), :])
                for dst, new in reversed(list(zip(dsts, news))):
                    o_ref[dst, :] = new
                return carry

            lax.fori_loop(0, r // sl, gather_group, 0)
            experts(k, g_free, y_free, n_valid)
            lax.fori_loop(0, r // sl, scatter_group, 0)
            return carry

        lax.fori_loop(0, (n - n_fast + r - 1) // r, chunk, 0)

    grp = MOE_GROUP
    e0 = grp * j
    segs = [segment(e0 + k) for k in range(grp)]
    _, base_next, _, _ = segment(jnp.minimum(e0 + grp, N_EXPERTS - 1))
    _, base_prev, _, _ = segment(jnp.maximum(e0 - 1, 0))
    bases = [base_prev] + [s[1] for s in segs] + [base_next]
    bufs = [(g_a, y_a), (g_b, y_b)]

    @pl.when(j == 0)
    def _():
        o_ref[...] = jnp.zeros_like(o_ref)
        y_b[...] = jnp.zeros_like(y_b)
        gather(g_a, bases[1])

    for k in range(grp):
        g_cur, y_cur = bufs[k % 2]
        g_other, y_other = bufs[(k + 1) % 2]
        experts(k, g_cur, y_cur, segs[k][2])
        gather(g_other, bases[k + 2])
        scatter(y_other, bases[k])

    @pl.when(j == nj - 1)
    def _():
        scatter(y_b, bases[grp])

    for k in range(grp):
        start, _, n_fast, n = segs[k]

        @pl.when(n > n_fast)
        def _(k=k, start=start, n_fast=n_fast, n=n):
            slow_path(k, start, n_fast, n, g_b, y_a)


def _moe_routed(x_tiles, tok, gate, offs, w_gate, w_up, w_down, layer, n_tiles):
    _, ne, f, d = w_gate.shape
    grp = MOE_GROUP
    assert grp % 2 == 0 and ne % grp == 0
    rows = MOE_TILE * SUBLANES
    slots = MOE_TILE * TOP_K
    grid_spec = pltpu.PrefetchScalarGridSpec(
        num_scalar_prefetch=1,
        grid=(n_tiles, ne // grp),
        in_specs=[
            pl.BlockSpec((rows, LANES), lambda i, j, offs: (i, 0), pipeline_mode=pl.Buffered(1)),
            pl.BlockSpec((slots,), lambda i, j, offs: (i,), memory_space=pltpu.SMEM),
            pl.BlockSpec((slots,), lambda i, j, offs: (i,), memory_space=pltpu.SMEM),
            pl.BlockSpec((None, grp, f, d), lambda i, j, offs: (layer, j, 0, 0)),
            pl.BlockSpec((None, grp, f, d), lambda i, j, offs: (layer, j, 0, 0)),
            pl.BlockSpec((None, grp, d, f), lambda i, j, offs: (layer, j, 0, 0)),
        ],
        out_specs=pl.BlockSpec((rows, LANES), lambda i, j, offs: (i, 0), pipeline_mode=pl.Buffered(1)),
        scratch_shapes=[pltpu.VMEM((MOE_ROWS * SUBLANES, LANES), F32)] * 4,
    )
    return pl.pallas_call(
        _moe_kernel,
        grid_spec=grid_spec,
        out_shape=jax.ShapeDtypeStruct(x_tiles.shape, F32),
        compiler_params=_cparams("arbitrary", "arbitrary"),
        name="moe_routed",
    )(offs, x_tiles, tok, gate, w_gate, w_up, w_down)


def _shared_kernel(final, h_ref, routed_ref, x1_ref, mod_ref, wg_ref, wu_ref, wd_ref, fg_ref, o_ref):
    h = h_ref[...].astype(BF16)
    hid = (_silu(_dot(h, wg_ref[...])) * _dot(h, wu_ref[...])).astype(BF16)
    tm = h.shape[0]
    routed = jnp.concatenate([routed_ref[pl.ds(s, tm, stride=SUBLANES), :] for s in range(SUBLANES)], axis=1)
    x2 = x1_ref[...] + mod_ref[0, 5:6, :] * (routed + _dot(hid, wd_ref[...]))
    if final:
        x2 = x2 * lax.rsqrt(jnp.mean(x2 * x2, axis=-1, keepdims=True) + RMS_EPS) * fg_ref[...]
    o_ref[...] = x2


def _shared_residual(h2, routed, x1, mod, ws_gate, ws_up, ws_down, final_gain, final, n_lat, seq, tm=512):
    t, d = x1.shape
    n_lat_tiles = n_lat // tm
    per_batch = seq // tm
    ctx_row = mod.shape[0] - 1

    def mod_map(i):
        return (jnp.where(i < n_lat_tiles, i // per_batch, ctx_row), 0, 0)

    const = lambda i: (0, 0)
    row = lambda i: (i, 0)
    full = lambda a: pl.BlockSpec(a.shape, const)
    return pl.pallas_call(
        functools.partial(_shared_kernel, final),
        grid=(t // tm,),
        in_specs=[pl.BlockSpec((tm, d), row), pl.BlockSpec((tm * SUBLANES, LANES), row), pl.BlockSpec((tm, d), row),
                  pl.BlockSpec((1, 6, d), mod_map), full(ws_gate), full(ws_up), full(ws_down),
                  pl.BlockSpec((1, d), const)],
        out_specs=pl.BlockSpec((tm, d), row),
        out_shape=jax.ShapeDtypeStruct((t, d), F32),
        compiler_params=_cparams("parallel"),
        name="shared_residual",
    )(h2, routed, x1, mod, ws_gate, ws_up, ws_down, final_gain.reshape(1, d))


MOD_ROWS = 16


def kernel(x, c, ctx, c_ctx, w_ada, b_ada, norm1_gain, norm2_gain, w_in, na_rpb, hg_lower_bounds, hg_norm_gain,
           w_fn_branch, w_na_branch, w_hg_branch, w_out, w_router, b_router, w_exp_gate, w_exp_up, w_exp_down,
           w_sh_gate, w_sh_up, w_sh_down, final_gain):
    b, n, d = x.shape
    nc = ctx.shape[1]
    depth = w_ada.shape[0]
    n_lat, n_ctx = b * n, b * nc

    sm = jax.nn.softmax(hg_lower_bounds.astype(F32), axis=0)
    cs = jnp.cumsum(sm, axis=0)
    lower = cs - cs[0]

    cin = jnp.concatenate([c, c_ctx[None, :], jnp.zeros((MOD_ROWS - b - 1, d), F32)], axis=0)
    mod = _ada_mod(cin, w_ada, b_ada)[:, :b + 1].reshape(depth, b + 1, 6, d)
    hg_consts = _hgrn_tables()
    zero_state = jnp.zeros((b, 2, HG_WIDTH, HG_WIDTH), F32)
    swap = lambda w: jnp.swapaxes(w, 2, 3).astype(BF16)
    wx_gate, wx_up, wx_down = swap(w_exp_gate), swap(w_exp_up), swap(w_exp_down)

    x_all = jnp.concatenate([x.reshape(n_lat, d), ctx.reshape(n_ctx, d)], axis=0)
    for l in range(depth):
        last = l == depth - 1
        w = w_in[l].astype(BF16)
        c0, c1, c2 = FN_WIDTH, FN_WIDTH + 3 * NA_WIDTH, FN_WIDTH + 3 * NA_WIDTH + 5 * HG_WIDTH
        z_fn, z_qkv, z_hg = _proj_in(x_all, mod[l], norm1_gain[l], w[:, :c0], w[:, c0:c1], w[:, c1:c2], n_lat, n)

        lbp = _hgrn_lb_params(lower[l])
        of_c, ob_c, s_ctx = _hgrn(z_hg, zero_state, lbp, hg_consts, b, nc, n_lat)
        of_l, ob_l, _ = _hgrn(z_hg, s_ctx, lbp, hg_consts, b, n, 0)
        fn_l = _fourier_lat(z_fn, b, n)
        na_l = _na_lat(z_qkv, _na_bias_table(na_rpb[l]), b, n, nc)
        if last:
            n_rows = n_lat
            fn_a, na_a, of_a, ob_a = fn_l, na_l, of_l, ob_l
        else:
            n_rows = n_lat + n_ctx
            cat = lambda p, q: jnp.concatenate([p, q], axis=0)
            fn_a = jnp.concatenate([fn_l, _fourier_ctx(z_fn, b, nc, n_lat)], axis=1)
            na_a = cat(na_l, _na_ctx(z_qkv, b, n, nc))
            of_a, ob_a = cat(of_l, of_c), cat(ob_l, ob_c)
        x1, h2, h2_tiles = _merge(x_all, mod[l], norm1_gain[l], norm2_gain[l], fn_a, na_a, of_a, ob_a, z_hg, hg_norm_gain[l],
                        w_fn_branch[l].astype(BF16), w_na_branch[l].astype(BF16), w_hg_branch[l].astype(BF16),
                        w[:, c2:], w_out[l].astype(BF16), n_rows, n_lat, n)

        idx, gate, counts = _router(h2, w_router[l], b_router[l])
        n_tiles = -(-n_rows // MOE_TILE)
        tok, gate_sorted, offs = _moe_dispatch(idx[:, :TOP_K], gate[:, :TOP_K], counts[::SUBLANES], n_tiles)
        routed = _moe_routed(h2_tiles, tok, gate_sorted, offs, wx_gate, wx_up, wx_down, l, n_tiles)
        x_all = _shared_residual(h2, routed, x1, mod[l], w_sh_gate[l].astype(BF16),
                                 w_sh_up[l].astype(BF16), w_sh_down[l].astype(BF16), final_gain, last, n_lat, n)
    return x_all.reshape(b, n, d)
```

```python
import functools

import numpy as np
import jax
import jax.numpy as jnp
from jax import lax
from jax.experimental import pallas as pl
from jax.experimental.pallas import tpu as pltpu

D_MODEL = 1024
GRID_W = 64
RMS_EPS = 1e-6
FN_GROUPS = 4
FN_GROUP_DIM = 64
FN_WIDTH = 256
NA_HEADS = 8
NA_HEAD_DIM = 64
NA_WIDTH = 512
NA_WIN_ROWS = 8
NA_WIN_COLS = 16
HG_HEADS = 4
HG_HEAD_DIM = 64
HG_WIDTH = 256
N_EXPERTS = 256
TOP_K = 8
EXPERT_DIM = 256
ROUTE_SCALE = 2.5

LANES = 128
SUBLANES = 8

F32 = jnp.float32
BF16 = jnp.bfloat16
VMEM_LIMIT = 56 * 1024 * 1024


def _cparams(*sem):
    return pltpu.CompilerParams(dimension_semantics=sem, vmem_limit_bytes=VMEM_LIMIT)


def _dot(a, b):
    return jnp.dot(a, b, preferred_element_type=F32)


def _dot_nt(a, b):
    return lax.dot_general(a, b, (((1,), (1,)), ((), ())), preferred_element_type=F32)


def _sigmoid(z):
    e = jnp.exp(-jnp.abs(z))
    r = 1.0 / (1.0 + e)
    return jnp.where(z >= 0, r, e * r)


def _silu(z):
    return z * _sigmoid(z)


def _norm_mod(x, gain, shift, scale):
    y = x * lax.rsqrt(jnp.mean(x * x, axis=-1, keepdims=True) + RMS_EPS)
    return (y * gain) * (1.0 + scale) + shift


def _ada_kernel(c_ref, w_ref, b_ref, o_ref):
    s = _silu(c_ref[...]).astype(BF16)
    o_ref[0] = _dot(s, w_ref[0].astype(BF16)) + b_ref[0]


def _ada_mod(cin, w_ada, b_ada):
    depth, d, n = w_ada.shape
    r = cin.shape[0]
    tn = 1536
    return pl.pallas_call(
        _ada_kernel,
        grid=(depth, n // tn),
        in_specs=[
            pl.BlockSpec((r, d), lambda l, j: (0, 0)),
            pl.BlockSpec((1, d, tn), lambda l, j: (l, 0, j)),
            pl.BlockSpec((1, 1, tn), lambda l, j: (l, 0, j)),
        ],
        out_specs=pl.BlockSpec((1, r, tn), lambda l, j: (l, 0, j)),
        out_shape=jax.ShapeDtypeStruct((depth, r, n), F32),
        compiler_params=_cparams("parallel", "parallel"),
        name="ada_mod",
    )(cin, w_ada, b_ada.reshape(depth, 1, n))


def _proj_in_kernel(x_ref, mod_ref, gain_ref, wfn_ref, wqkv_ref, whg_ref, fn_ref, qkv_ref, hg_ref):
    h = _norm_mod(x_ref[...], gain_ref[...], mod_ref[0, 0:1, :], mod_ref[0, 1:2, :]).astype(BF16)
    fn_ref[...] = _dot(h, wfn_ref[...])
    qkv_ref[...] = _dot(h, wqkv_ref[...]).astype(BF16)
    hg_ref[...] = _dot(h, whg_ref[...])


def _proj_in(x_all, mod, gain, w_fn, w_qkv, w_hg, n_lat, seq, tm=512):
    t, d = x_all.shape
    n_lat_tiles = n_lat // tm
    per_batch = seq // tm
    ctx_row = mod.shape[0] - 1

    def mod_map(i):
        return (jnp.where(i < n_lat_tiles, i // per_batch, ctx_row), 0, 0)

    const = lambda i: (0, 0)
    row = lambda i: (i, 0)
    return pl.pallas_call(
        _proj_in_kernel,
        grid=(t // tm,),
        in_specs=[
            pl.BlockSpec((tm, d), row),
            pl.BlockSpec((1, 6, d), mod_map),
            pl.BlockSpec((1, d), const),
            pl.BlockSpec(w_fn.shape, const),
            pl.BlockSpec(w_qkv.shape, const),
            pl.BlockSpec(w_hg.shape, const),
        ],
        out_specs=[
            pl.BlockSpec((tm, w_fn.shape[1]), row),
            pl.BlockSpec((tm, w_qkv.shape[1]), row),
            pl.BlockSpec((tm, w_hg.shape[1]), row),
        ],
        out_shape=[
            jax.ShapeDtypeStruct((t, w_fn.shape[1]), F32),
            jax.ShapeDtypeStruct((t, w_qkv.shape[1]), BF16),
            jax.ShapeDtypeStruct((t, w_hg.shape[1]), F32),
        ],
        compiler_params=_cparams("parallel"),
        name="proj_in",
    )(x_all, mod, gain.reshape(1, d), w_fn, w_qkv, w_hg)


def _dft_cos_sin(n):
    k = np.arange(n)
    ang = 2.0 * np.pi * ((k[:, None] * k[None, :]) % n) / n
    return np.cos(ang), np.sin(ang)


def _channel_dft_mats(scale):
    c, s = _dft_cos_sin(FN_GROUP_DIM)
    eye = np.eye(FN_GROUPS)
    return (jnp.asarray(np.kron(eye, c) * scale, BF16), jnp.asarray(np.kron(eye, -s) * scale, BF16))


FFT_R = 64
FFT_GROUP = 4


def _store_halves(ref, lead, rows, val):
    ref[lead + (0, rows, slice(None))] = val[:, :LANES]
    ref[lead + (1, rows, slice(None))] = val[:, LANES:]


def _load_halves(ref, lead, rows):
    return jnp.concatenate([ref[lead + (0, rows, slice(None))], ref[lead + (1, rows, slice(None))]], axis=1)


def _fourier_lat_kernel(u_ref, cc_ref, sc_ref, ga_ref, gb_ref, tc_ref, ts_ref, o_ref, u_s, z_s):
    ub = u_ref[...].astype(BF16)
    every = slice(None)
    _store_halves(u_s, (0,), every, _dot(ub, cc_ref[...]))
    _store_halves(u_s, (1,), every, _dot(ub, sc_ref[...]))

    def stage_a(g, carry):
        n2s = [g * FFT_GROUP + i for i in range(FFT_GROUP)]
        xs = []
        for n2 in n2s:
            rows = pl.ds(n2, FFT_R, stride=FFT_R)
            xs.append(jnp.concatenate([_load_halves(u_s, (0,), rows), _load_halves(u_s, (1,), rows)],
                                      axis=0).astype(BF16))
        prods = [_dot(ga_ref[...], x) for x in xs]
        for n2, a in zip(n2s, prods):
            ar, ai = a[:FFT_R], a[FFT_R:]
            tc = jnp.concatenate([tc_ref[n2], tc_ref[n2]], axis=1)
            ts = jnp.concatenate([ts_ref[n2], ts_ref[n2]], axis=1)
            dst = pl.ds(pl.multiple_of(n2 * FFT_R, FFT_R), FFT_R)
            _store_halves(z_s, (0,), dst, ar * tc + ai * ts)
            _store_halves(z_s, (1,), dst, ai * tc - ar * ts)
        return carry

    lax.fori_loop(0, FFT_R // FFT_GROUP, stage_a, 0)

    def stage_b(g, carry):
        k1s = [g * FFT_GROUP + i for i in range(FFT_GROUP)]
        zs = []
        for k1 in k1s:
            rows = pl.ds(k1, FFT_R, stride=FFT_R)
            zs.append(jnp.concatenate([_load_halves(z_s, (0,), rows), _load_halves(z_s, (1,), rows)],
                                      axis=0).astype(BF16))
        prods = [_dot(gb_ref[...], z) for z in zs]
        for k1, p in zip(k1s, prods):
            _store_halves(o_ref, (), pl.ds(k1, FFT_R, stride=FFT_R), p)
        return carry

    lax.fori_loop(0, FFT_R // FFT_GROUP, stage_b, 0)


def _fourier_lat(z_fn, n_batch, seq):
    assert seq == FFT_R * FFT_R
    c, s = _dft_cos_sin(FFT_R)
    cc, sc = _channel_dft_mats((seq * FN_GROUP_DIM) ** -0.5)
    ga = jnp.asarray(np.block([[c, s], [-s, c]]), BF16)
    gb = jnp.asarray(np.concatenate([c, s], axis=1), BF16)
    k = np.arange(FFT_R)
    ang = 2.0 * np.pi * (k[:, None] * k[None, :]) / seq
    tc = jnp.asarray(np.broadcast_to(np.cos(ang)[:, :, None], (FFT_R, FFT_R, 128)), F32)
    ts = jnp.asarray(np.broadcast_to(np.sin(ang)[:, :, None], (FFT_R, FFT_R, 128)), F32)
    const2 = lambda b: (0, 0)
    const3 = lambda b: (0, 0, 0)
    return pl.pallas_call(
        _fourier_lat_kernel,
        grid=(n_batch,),
        in_specs=[
            pl.BlockSpec((seq, FN_WIDTH), lambda b: (b, 0)),
            pl.BlockSpec(cc.shape, const2),
            pl.BlockSpec(sc.shape, const2),
            pl.BlockSpec(ga.shape, const2),
            pl.BlockSpec(gb.shape, const2),
            pl.BlockSpec(tc.shape, const3),
            pl.BlockSpec(ts.shape, const3),
        ],
        out_specs=pl.BlockSpec((2, seq, LANES), lambda b: (0, b, 0)),
        out_shape=jax.ShapeDtypeStruct((2, n_batch * seq, LANES), F32),
        scratch_shapes=[pltpu.VMEM((2, 2, seq, LANES), F32), pltpu.VMEM((2, 2, seq, LANES), F32)],
        compiler_params=_cparams("parallel"),
        name="fourier_lat",
    )(z_fn, cc, sc, ga, gb, tc, ts)


def _fourier_ctx_kernel(u_ref, cc_ref, sc_ref, g_ref, o_ref):
    ub = u_ref[...].astype(BF16)
    x = jnp.concatenate([_dot(ub, cc_ref[...]), _dot(ub, sc_ref[...])], axis=0).astype(BF16)
    _store_halves(o_ref, (), slice(None), _dot(g_ref[...], x))


def _fourier_ctx(z_fn, n_batch, n_ctx, row0):
    blk0 = row0 // n_ctx
    c, s = _dft_cos_sin(n_ctx)
    cc, sc = _channel_dft_mats((n_ctx * FN_GROUP_DIM) ** -0.5)
    g = jnp.asarray(np.concatenate([c, s], axis=1), BF16)
    const2 = lambda b: (0, 0)
    return pl.pallas_call(
        _fourier_ctx_kernel,
        grid=(n_batch,),
        in_specs=[
            pl.BlockSpec((n_ctx, FN_WIDTH), lambda b: (blk0 + b, 0)),
            pl.BlockSpec(cc.shape, const2),
            pl.BlockSpec(sc.shape, const2),
            pl.BlockSpec(g.shape, const2),
        ],
        out_specs=pl.BlockSpec((2, n_ctx, LANES), lambda b: (0, b, 0)),
        out_shape=jax.ShapeDtypeStruct((2, n_batch * n_ctx, LANES), F32),
        compiler_params=_cparams("parallel"),
        name="fourier_ctx",
    )(z_fn, cc, sc, g)


NEG_BIG = -1e30
HEAD_PAIR = 2 * NA_HEAD_DIM


def _na_bias_table(rpb):
    wr, wc, w = NA_WIN_ROWS, NA_WIN_COLS, GRID_W
    col = np.arange(w)
    col_start = np.clip(col - wc // 2, 0, w - wc)
    in_win = (col[None, :] >= col_start[:, None]) & (col[None, :] < col_start[:, None] + wc)
    dc_idx = np.clip(col[None, :] - col[:, None], 1 - wc, wc - 1) + (wc - 1)
    h, n_dr, n_dc = rpb.shape
    onehot = jnp.asarray(dc_idx.reshape(-1)[None, :] == np.arange(n_dc)[:, None], F32)
    t = jnp.dot(rpb.astype(F32).reshape(h * n_dr, n_dc), onehot, precision=lax.Precision.HIGHEST)
    t = jnp.where(in_win.reshape(-1)[None, :], t, NEG_BIG).reshape(h, n_dr, w, w)
    per_s = [jnp.concatenate([t[:, s + j] for j in range(wr)], axis=-1) for s in range(wr)]
    return jnp.stack(per_s, axis=1)


def _softmax_pv_many(problems):
    scores = []
    for q, keys, vals, biases in problems:
        s = []
        for k, b in zip(keys, biases):
            si = _dot_nt(q, k)
            s.append(si if b is None else si + b)
        scores.append(s)
    maxes = []
    for s in scores:
        m = s[0].max(axis=-1, keepdims=True)
        for si in s[1:]:
            m = jnp.maximum(m, si.max(axis=-1, keepdims=True))
        maxes.append(m)
    probs = [[jnp.exp(si - m) for si in s] for s, m in zip(scores, maxes)]
    outs = []
    for (q, keys, vals, biases), ps in zip(problems, probs):
        acc, l = None, None
        for p, v in zip(ps, vals):
            li = p.sum(axis=-1, keepdims=True)
            oi = _dot(p.astype(BF16), v)
            acc = oi if acc is None else acc + oi
            l = li if l is None else l + li
        outs.append(acc / l)
    return outs


def _softmax_pv(q, keys, vals, biases):
    return _softmax_pv_many([(q, keys, vals, biases)])[0]


NA_ROW_GROUP = 4


def _na_lat_kernel(q_ref, k_ref, v_ref, kc_ref, vc_ref, bias_ref, o_ref):
    w = GRID_W
    n_loc = NA_WIN_ROWS * w
    rows = q_ref.shape[0] // w
    lane = lax.broadcasted_iota(jnp.int32, (w, HEAD_PAIR), 1)
    first = lane < NA_HEAD_DIM
    kc = kc_ref[...]
    vc = vc_ref[...]

    def body(g, carry):
        problems = []
        for i in range(NA_ROW_GROUP):
            r = g * NA_ROW_GROUP + i
            kr0 = jnp.clip(r - NA_WIN_ROWS // 2, 0, rows - NA_WIN_ROWS)
            s = kr0 - r + (NA_WIN_ROWS - 1)
            q = q_ref[pl.ds(pl.multiple_of(r * w, w), w), :] * jnp.asarray(NA_HEAD_DIM ** -0.5, BF16)
            ks = k_ref[pl.ds(pl.multiple_of(kr0 * w, w), n_loc), :]
            vs = v_ref[pl.ds(pl.multiple_of(kr0 * w, w), n_loc), :]
            zero = jnp.zeros_like(q)
            q2 = jnp.concatenate([jnp.where(first, q, zero), jnp.where(first, zero, q)], axis=0)
            bias = jnp.concatenate([bias_ref[0, s], bias_ref[1, s]], axis=0)
            problems.append((q2, [ks, kc], [vs, vc], [bias, None]))
        for i, o2 in enumerate(_softmax_pv_many(problems)):
            r = g * NA_ROW_GROUP + i
            o_ref[pl.ds(pl.multiple_of(r * w, w), w), :] = jnp.where(first, o2[:w], o2[w:]).astype(o_ref.dtype)
        return carry

    lax.fori_loop(0, rows // NA_ROW_GROUP, body, 0)


def _na_lat(qkv, bias, n_batch, seq, n_ctx):
    n_pairs = NA_WIDTH // HEAD_PAIR
    ctx0 = n_batch * seq // n_ctx
    wr, w = NA_WIN_ROWS, GRID_W
    return pl.pallas_call(
        _na_lat_kernel,
        grid=(n_batch, n_pairs),
        in_specs=[
            pl.BlockSpec((seq, HEAD_PAIR), lambda b, p: (b, p)),
            pl.BlockSpec((seq, HEAD_PAIR), lambda b, p: (b, n_pairs + p)),
            pl.BlockSpec((seq, HEAD_PAIR), lambda b, p: (b, 2 * n_pairs + p)),
            pl.BlockSpec((n_ctx, HEAD_PAIR), lambda b, p: (ctx0 + b, n_pairs + p)),
            pl.BlockSpec((n_ctx, HEAD_PAIR), lambda b, p: (ctx0 + b, 2 * n_pairs + p)),
            pl.BlockSpec((2, wr, w, wr * w), lambda b, p: (p, 0, 0, 0)),
        ],
        out_specs=pl.BlockSpec((seq, HEAD_PAIR), lambda b, p: (b, p)),
        out_shape=jax.ShapeDtypeStruct((n_batch * seq, NA_WIDTH), BF16),
        compiler_params=_cparams("parallel", "parallel"),
        name="na_lat",
    )(qkv, qkv, qkv, qkv, qkv, bias)


def _na_ctx_kernel(q_ref, k_ref, v_ref, o_ref):
    lane = lax.broadcasted_iota(jnp.int32, q_ref.shape, 1)
    first = lane < NA_HEAD_DIM
    q = q_ref[...] * jnp.asarray(NA_HEAD_DIM ** -0.5, BF16)
    k = k_ref[...]
    v = v_ref[...]
    zero = jnp.zeros_like(q)
    o0 = _softmax_pv(jnp.where(first, q, zero), [k], [v], [None])
    o1 = _softmax_pv(jnp.where(first, zero, q), [k], [v], [None])
    o_ref[...] = jnp.where(first, o0, o1).astype(o_ref.dtype)


def _na_ctx(qkv, n_batch, seq, n_ctx):
    n_pairs = NA_WIDTH // HEAD_PAIR
    ctx0 = n_batch * seq // n_ctx
    return pl.pallas_call(
        _na_ctx_kernel,
        grid=(n_batch, n_pairs),
        in_specs=[
            pl.BlockSpec((n_ctx, HEAD_PAIR), lambda b, p: (ctx0 + b, p)),
            pl.BlockSpec((n_ctx, HEAD_PAIR), lambda b, p: (ctx0 + b, n_pairs + p)),
            pl.BlockSpec((n_ctx, HEAD_PAIR), lambda b, p: (ctx0 + b, 2 * n_pairs + p)),
        ],
        out_specs=pl.BlockSpec((n_ctx, HEAD_PAIR), lambda b, p: (b, p)),
        out_shape=jax.ShapeDtypeStruct((n_batch * n_ctx, NA_WIDTH), BF16),
        compiler_params=_cparams("parallel", "parallel"),
        name="na_ctx",
    )(qkv, qkv, qkv)


HG_CHUNK = 64
HG_LEVELS = (32, 16)
HG_DIAG = 16
HG_BATCH = 4


def _hgrn_tables():
    c = HG_CHUNK
    tri, masks, refs, last = [], [], [], []
    for reverse in (False, True):
        u = np.arange(c)[::-1] if reverse else np.arange(c)
        row_of = {int(uu): t for t, uu in enumerate(u)}
        ut, uj = u[:, None], u[None, :]
        tri.append(uj <= ut)
        lv, rf = [], []
        for h in HG_LEVELS:
            same = (ut // (2 * h)) == (uj // (2 * h))
            up_t, up_j = (ut % (2 * h)) >= h, (uj % (2 * h)) >= h
            lv.append(same & up_t & ~up_j)
            rf.append([row_of[int(uu) // (2 * h) * (2 * h) + h - 1] for uu in u])
        same_d = (ut // HG_DIAG) == (uj // HG_DIAG)
        lv.append(same_d & (uj <= ut))
        rf.append([row_of.get(int(uu) // HG_DIAG * HG_DIAG - 1, -1) for uu in u])
        masks.append(np.stack([np.tile(x, (1, HG_HEADS)) for x in lv]).astype(np.float32))
        refs.append(rf)
        last.append(row_of[c - 1])
    hm = (np.arange(HG_WIDTH)[:, None] // HG_HEAD_DIM) == (np.arange(HG_WIDTH)[None, :] // HG_HEAD_DIM)
    arrays = (jnp.asarray(np.stack(tri), BF16), jnp.asarray(np.stack(masks), F32), jnp.asarray(hm, F32))
    return arrays, refs, last


def _ref_rows(b, ref):
    pieces, t = [], 0
    while t < len(ref):
        t1 = t
        while t1 < len(ref) and ref[t1] == ref[t]:
            t1 += 1
        shape = (t1 - t, b.shape[1])
        pieces.append(jnp.zeros(shape, b.dtype) if ref[t] < 0 else jnp.broadcast_to(b[ref[t]:ref[t] + 1, :], shape))
        t = t1
    return jnp.concatenate(pieces, axis=0)


def _split3(x):
    hi = x.astype(BF16)
    r = x - hi.astype(F32)
    mid = r.astype(BF16)
    lo = (r - mid.astype(F32)).astype(BF16)
    return hi, mid, lo


def _hgrn_chunks(chains, hm, refs, last):
    hm_b = hm.astype(BF16)

    def heads_bd(x):
        return jnp.concatenate([x.astype(BF16)] * HG_HEADS, axis=0) * hm_b

    qs, ks, splits = [], [], []
    for zq, zv, zf, lbp, tri, masks, st, d in chains:
        e = jnp.exp(-jnp.abs(zf))
        inv = 1.0 / (1.0 + e)
        log_sig = jnp.minimum(zf, 0.0) - jnp.log(1.0 + e)
        sig_neg = jnp.where(zf >= 0, e * inv, inv)
        a = lbp[0:1, :]
        cc = lbp[1:2, :] + log_sig
        log_f = jnp.maximum(a, cc) + jnp.log(1.0 + jnp.exp(-jnp.abs(a - cc)))
        qs.append(_silu(zq))
        ks.append(lbp[2:3, :] * sig_neg)
        splits.append(_split3(log_f))
    bs = [_dot(c[4], hi) + _dot(c[4], mid) + _dot(c[4], lo) for c, (hi, mid, lo) in zip(chains, splits)]
    b_lasts = [b[last[c[7]]:last[c[7]] + 1] for c, b in zip(chains, bs)]
    a_alls = [None] * len(chains)
    for i in range(len(HG_LEVELS) + 1):
        pairs = []
        for c, q, k, b in zip(chains, qs, ks, bs):
            rel = b - _ref_rows(b, refs[c[7]][i])
            if i < len(HG_LEVELS):
                pairs.append((q * jnp.exp(jnp.minimum(rel, 0.0)), k * jnp.exp(jnp.minimum(-rel, 0.0))))
            else:
                pairs.append((q * jnp.exp(rel), k * jnp.exp(-rel)))
        for n, (c, (qx, kx)) in enumerate(zip(chains, pairs)):
            ai = _dot_nt(qx.astype(BF16), heads_bd(kx)) * c[5][i]
            a_alls[n] = ai if a_alls[n] is None else a_alls[n] + ai
    outs = [_dot(a_all.astype(BF16), heads_bd(c[1])) + _dot_nt((q * jnp.exp(b)).astype(BF16), c[6].astype(BF16))
            for c, q, b, a_all in zip(chains, qs, bs, a_alls)]
    states = []
    for c, k, b, b_last in zip(chains, ks, bs, b_lasts):
        kl = (k * jnp.exp(b_last - b)).astype(BF16)
        upd = lax.dot_general(c[1].astype(BF16), kl, (((0,), (0,)), ((), ())), preferred_element_type=F32)
        states.append(c[6] * jnp.exp(b_last) + upd * hm)
    return outs, states


def _hgrn_kernel(refs, last, *args):
    nb = HG_BATCH
    z_refs = args[:3 * nb]
    s0_ref, lbp_ref, tri_ref, mask_ref, hm_ref = args[3 * nb:3 * nb + 5]
    of_ref, ob_ref, sfin_ref, st = args[3 * nb + 5:]
    i = pl.program_id(1)

    @pl.when(i == 0)
    def _():
        st[...] = s0_ref[...]

    w = HG_WIDTH
    hm = hm_ref[...]
    chains = []
    for bb in range(nb):
        zf_ref, zb_ref, zbg_ref = z_refs[3 * bb:3 * bb + 3]
        chains.append((zf_ref[:, 0:w], zf_ref[:, w:2 * w], zf_ref[:, 2 * w:3 * w], lbp_ref[0], tri_ref[0],
                       mask_ref[0], st[bb, 0], 0))
        chains.append((zb_ref[:, 0:w], zb_ref[:, w:2 * w], zbg_ref[...], lbp_ref[1], tri_ref[1],
                       mask_ref[1], st[bb, 1], 1))
    outs, states = _hgrn_chunks(chains, hm, refs, last)
    for bb in range(nb):
        of_ref[bb] = outs[2 * bb]
        ob_ref[bb] = outs[2 * bb + 1]
        st[bb, 0] = states[2 * bb]
        st[bb, 1] = states[2 * bb + 1]

    @pl.when(i == pl.num_programs(1) - 1)
    def _():
        sfin_ref[...] = st[...]


def _hgrn(z_hg, s0, lbp, tables, n_batch, n_tok, row0):
    (tri, masks, hm), refs, last = tables
    c, w, nb = HG_CHUNK, HG_WIDTH, HG_BATCH
    assert n_batch % nb == 0 and n_tok % c == 0
    nch = n_tok // c
    base = row0 // c
    const3 = lambda b, i: (0, 0, 0)
    in_specs = []
    for bb in range(nb):
        fwd = lambda b, i, bb=bb: (base + (b * nb + bb) * nch + i, 0)
        bwd = lambda b, i, bb=bb: (base + (b * nb + bb) * nch + (nch - 1 - i), 0)
        bwd_gate = lambda b, i, bb=bb: (base + (b * nb + bb) * nch + (nch - 1 - i), 3)
        in_specs += [pl.BlockSpec((c, 3 * w), fwd), pl.BlockSpec((c, 2 * w), bwd), pl.BlockSpec((c, w), bwd_gate)]
    in_specs += [
        pl.BlockSpec((nb, 2, w, w), lambda b, i: (b, 0, 0, 0)),
        pl.BlockSpec(lbp.shape, const3),
        pl.BlockSpec(tri.shape, const3),
        pl.BlockSpec(masks.shape, lambda b, i: (0, 0, 0, 0)),
        pl.BlockSpec(hm.shape, lambda b, i: (0, 0)),
    ]
    o_f, o_b, s_fin = pl.pallas_call(
        functools.partial(_hgrn_kernel, refs, last),
        grid=(n_batch // nb, nch),
        in_specs=in_specs,
        out_specs=[
            pl.BlockSpec((nb, c, w), lambda b, i: (b, i, 0)),
            pl.BlockSpec((nb, c, w), lambda b, i: (b, nch - 1 - i, 0)),
            pl.BlockSpec((nb, 2, w, w), lambda b, i: (b, 0, 0, 0)),
        ],
        out_shape=[
            jax.ShapeDtypeStruct((n_batch, n_tok, w), F32),
            jax.ShapeDtypeStruct((n_batch, n_tok, w), F32),
            jax.ShapeDtypeStruct((n_batch, 2, w, w), F32),
        ],
        scratch_shapes=[pltpu.VMEM((nb, 2, w, w), F32)],
        compiler_params=_cparams("parallel", "arbitrary"),
        name="hgrn",
    )(*([z_hg] * (3 * nb)), s0, lbp, tri, masks, hm)
    return o_f.reshape(n_batch * n_tok, w), o_b.reshape(n_batch * n_tok, w), s_fin


def _hgrn_lb_params(lower_l):
    rows = jnp.stack([jnp.log(lower_l), jnp.log1p(-lower_l), 1.0 - lower_l], axis=1)
    return jnp.pad(rows, ((0, 0), (0, 5), (0, 0)))


def _merge_kernel(x_ref, mod_ref, g1_ref, g2_ref, fn_ref, na_ref, of_ref, ob_ref, hgz_ref, hgain_ref,
                  hmean_ref, wfn_ref, wna_ref, whg_ref, wbg_ref, wout_ref, x1_ref, h2_ref, h2t_ref):
    d = x_ref.shape[1]
    x = x_ref[...]
    h = _norm_mod(x, g1_ref[...], mod_ref[0, 0:1, :], mod_ref[0, 1:2, :]).astype(BF16)
    o = of_ref[...] + ob_ref[...]
    hi, mid, lo = _split3(o * o)
    ms = _dot(hi, hmean_ref[...]) + _dot(mid, hmean_ref[...]) + _dot(lo, hmean_ref[...])
    y_hg_in = o * lax.rsqrt(ms + RMS_EPS) * hgain_ref[...] * _silu(hgz_ref[...])
    y_fn = _dot(jnp.concatenate([fn_ref[0], fn_ref[1]], axis=1).astype(BF16), wfn_ref[...])
    y_na = _dot(na_ref[...], wna_ref[...])
    y_hg = _dot(y_hg_in.astype(BF16), whg_ref[...])
    m = (_sigmoid(_dot(h, wbg_ref[:, 0:d])) * y_fn
         + _sigmoid(_dot(h, wbg_ref[:, d:2 * d])) * y_na
         + _sigmoid(_dot(h, wbg_ref[:, 2 * d:3 * d])) * y_hg)
    x1 = x + mod_ref[0, 2:3, :] * _dot(m.astype(BF16), wout_ref[...])
    x1_ref[...] = x1
    h2 = _norm_mod(x1, g2_ref[...], mod_ref[0, 3:4, :], mod_ref[0, 4:5, :])
    h2_ref[...] = h2
    tm = x.shape[0]
    for s in range(SUBLANES):
        h2t_ref[pl.ds(s, tm, stride=SUBLANES), :] = h2[:, s * LANES:(s + 1) * LANES]


def _merge(x_all, mod, gain1, gain2, fn, na, o_f, o_b, z_hg, hgain, w_fnb, w_nab, w_hgb, w_bg, w_out,
           n_rows, n_lat, seq, tm=512):
    t, d = n_rows, x_all.shape[1]
    n_lat_tiles = n_lat // tm
    per_batch = seq // tm
    ctx_row = mod.shape[0] - 1
    w = HG_WIDTH
    hmean = jnp.asarray(np.kron(np.eye(HG_HEADS), np.full((HG_HEAD_DIM, HG_HEAD_DIM), 1.0 / HG_HEAD_DIM)), BF16)

    def mod_map(i):
        return (jnp.where(i < n_lat_tiles, i // per_batch, ctx_row), 0, 0)

    const = lambda i: (0, 0)
    row = lambda i: (i, 0)
    full = lambda a: pl.BlockSpec(a.shape, const)
    return pl.pallas_call(
        _merge_kernel,
        grid=(t // tm,),
        in_specs=[
            pl.BlockSpec((tm, d), row),
            pl.BlockSpec((1, 6, d), mod_map),
            pl.BlockSpec((1, d), const),
            pl.BlockSpec((1, d), const),
            pl.BlockSpec((2, tm, LANES), lambda i: (0, i, 0)),
            pl.BlockSpec((tm, NA_WIDTH), row),
            pl.BlockSpec((tm, w), row),
            pl.BlockSpec((tm, w), row),
            pl.BlockSpec((tm, w), lambda i: (i, 4)),
            pl.BlockSpec((1, w), const),
            full(hmean), full(w_fnb), full(w_nab), full(w_hgb), full(w_bg), full(w_out),
        ],
        out_specs=[pl.BlockSpec((tm, d), row), pl.BlockSpec((tm, d), row), pl.BlockSpec((tm * SUBLANES, LANES), row)],
        out_shape=[jax.ShapeDtypeStruct((t, d), F32), jax.ShapeDtypeStruct((t, d), F32),
                   jax.ShapeDtypeStruct((t * SUBLANES, LANES), F32)],
        compiler_params=_cparams("parallel"),
        name="merge",
    )(x_all, mod, gain1.reshape(1, d), gain2.reshape(1, d), fn, na, o_f, o_b, z_hg,
      jnp.tile(hgain, HG_HEADS).reshape(1, w), hmean, w_fnb, w_nab, w_hgb, w_bg, w_out)


def _router_kernel(h_ref, whi_ref, wlo_ref, b_ref, idx_ref, gate_ref, cnt_ref):
    h = h_ref[...]
    hi = h.astype(BF16)
    lo = (h - hi.astype(F32)).astype(BF16)
    logits = _dot(hi, whi_ref[...]) + (_dot(hi, wlo_ref[...]) + _dot(lo, whi_ref[...]))
    scores = _sigmoid(logits)
    sel = scores + b_ref[...]
    tm, ne = sel.shape
    col = lax.broadcasted_iota(jnp.int32, (tm, ne), 1)
    out_lane = lax.broadcasted_iota(jnp.int32, (tm, LANES), 1)
    idx_out = jnp.zeros((tm, LANES), jnp.int32)
    gate_out = jnp.zeros((tm, LANES), F32)
    total = jnp.zeros((tm, 1), F32)
    chosen = jnp.zeros((tm, ne), F32)
    for k in range(TOP_K):
        m = sel.max(axis=-1, keepdims=True)
        idx = jnp.where(sel == m, col, ne).min(axis=-1, keepdims=True)
        hit = col == idx
        g = jnp.where(hit, scores, 0.0).sum(axis=-1, keepdims=True)
        sel = jnp.where(hit, -jnp.inf, sel)
        chosen = jnp.where(hit, 1.0, chosen)
        idx_out = jnp.where(out_lane == k, idx, idx_out)
        gate_out = jnp.where(out_lane == k, g, gate_out)
        total = total + g
    idx_ref[...] = idx_out
    gate_ref[...] = gate_out * (ROUTE_SCALE / total)
    cnt_ref[...] = jnp.broadcast_to(chosen.sum(axis=0, keepdims=True), cnt_ref.shape).astype(jnp.int32)


def _router(h2, w_router, b_router, tm=512):
    t, d = h2.shape
    ne = w_router.shape[1]
    w_hi = w_router.astype(BF16)
    w_lo = (w_router - w_hi.astype(F32)).astype(BF16)
    const = lambda i: (0, 0)
    row = lambda i: (i, 0)
    return pl.pallas_call(
        _router_kernel,
        grid=(t // tm,),
        in_specs=[pl.BlockSpec((tm, d), row), pl.BlockSpec((d, ne), const), pl.BlockSpec((d, ne), const),
                  pl.BlockSpec((1, ne), const)],
        out_specs=[pl.BlockSpec((tm, LANES), row), pl.BlockSpec((tm, LANES), row), pl.BlockSpec((SUBLANES, ne), row)],
        out_shape=[jax.ShapeDtypeStruct((t, LANES), jnp.int32), jax.ShapeDtypeStruct((t, LANES), F32),
                   jax.ShapeDtypeStruct((t // tm * SUBLANES, ne), jnp.int32)],
        compiler_params=_cparams("parallel"),
        name="router",
    )(h2, w_hi, w_lo, b_router.reshape(1, ne).astype(F32))


MOE_TILE = 4096
MOE_ROWS = 160
MOE_GROUP = 4


def _moe_dispatch(idx, gate, counts, n_tiles):
    t, k = idx.shape
    pad = n_tiles * MOE_TILE - t
    slots = MOE_TILE * k
    e = jnp.pad(idx, ((0, pad), (0, 0)), constant_values=N_EXPERTS).reshape(n_tiles, slots)
    g = jnp.pad(gate, ((0, pad), (0, 0))).reshape(n_tiles, slots)
    key = e * slots + jnp.arange(slots, dtype=jnp.int32)[None, :]
    key_sorted, g_sorted = lax.sort((key, g), dimension=1, num_keys=1)
    row8_sorted = (key_sorted % slots) // k * SUBLANES
    group = MOE_TILE // (t // counts.shape[0])
    c = jnp.pad(counts, ((0, n_tiles * group - counts.shape[0]), (0, 0))).reshape(n_tiles, group, -1).sum(axis=1)
    offs = jnp.concatenate([jnp.zeros((n_tiles, 1), jnp.int32), jnp.cumsum(c, axis=1)], axis=1)
    offs = jnp.pad(offs.astype(jnp.int32).reshape(-1), (0, SUBLANES))
    return row8_sorted.reshape(-1), g_sorted.reshape(-1), offs


def _moe_kernel(offs_ref, x_ref, row_ref, gate_ref, wg_ref, wu_ref, wd_ref, o_ref, g_a, g_b, y_a, y_b):
    i = pl.program_id(0)
    j = pl.program_id(1)
    nj = pl.num_programs(1)
    r, sl = MOE_ROWS, SUBLANES
    n_slots = row_ref.shape[0]
    all_rows = list(range(r))

    def segment(ee):
        start = offs_ref[i * (N_EXPERTS + 1) + ee]
        n = offs_ref[i * (N_EXPERTS + 1) + ee + 1] - start
        n_fast = jnp.where(start <= n_slots - r, jnp.minimum(n, r), 0)
        return start, jnp.minimum(start, n_slots - r), n_fast, n

    def gather(buf, base):
        for row in all_rows:
            buf[pl.ds(row * sl, sl), :] = x_ref[pl.ds(pl.multiple_of(row_ref[base + row], sl), sl), :]

    def experts(k, buf_in, buf_out, n_valid):
        x = jnp.concatenate([buf_in[pl.ds(s, r, stride=sl), :] for s in range(sl)], axis=1).astype(BF16)
        hid = (_silu(_dot(x, wg_ref[k])) * _dot(x, wu_ref[k])).astype(BF16)
        y = _dot(hid, wd_ref[k])
        y = jnp.where(lax.broadcasted_iota(jnp.int32, (r, 1), 0) < n_valid, y, 0.0)
        for s in range(sl):
            buf_out[pl.ds(s, r, stride=sl), :] = y[:, s * LANES:(s + 1) * LANES]

    def scatter(buf, base):
        for g0 in range(0, r, sl):
            dsts, news = [], []
            for row in all_rows[g0:g0 + sl]:
                dst = pl.ds(pl.multiple_of(row_ref[base + row], sl), sl)
                dsts.append(dst)
                news.append(o_ref[dst, :] + gate_ref[base + row] * buf[pl.ds(row * sl, sl), :])
            for dst, new in reversed(list(zip(dsts, news))):
                o_ref[dst, :] = new

    def slow_path(k, start, n_fast, n, g_free, y_free):
        def chunk(c, carry):
            base = start + n_fast + c * r
            n_valid = jnp.minimum(n - n_fast - c * r, r)

            def slot(row):
                return jnp.minimum(base + row, n_slots - 1)

            def gather_group(gi, carry):
                for row in [gi * sl + q for q in range(sl)]:
                    src = pl.ds(pl.multiple_of(row_ref[slot(row)], sl), sl)
                    g_free[pl.ds(pl.multiple_of(row * sl, sl), sl), :] = x_ref[src, :]
                return carry

            def scatter_group(gi, carry):
                dsts, news = [], []
                for row in [gi * sl + q for q in range(sl)]:
                    dst = pl.ds(pl.multiple_of(row_ref[slot(row)], sl), sl)
                    dsts.append(dst)
                    news.append(o_ref[dst, :] + gate_ref[slot(row)] * y_free[pl.ds(pl.multiple_of(row * sl, sl), sl), :])
                for dst, new in reversed(list(zip(dsts, news))):
                    o_ref[dst, :] = new
                return carry

            lax.fori_loop(0, r // sl, gather_group, 0)
            experts(k, g_free, y_free, n_valid)
            lax.fori_loop(0, r // sl, scatter_group, 0)
            return carry

        lax.fori_loop(0, (n - n_fast + r - 1) // r, chunk, 0)

    grp = MOE_GROUP
    e0 = grp * j
    segs = [segment(e0 + k) for k in range(grp)]
    _, base_next, _, _ = segment(jnp.minimum(e0 + grp, N_EXPERTS - 1))
    _, base_prev, _, _ = segment(jnp.maximum(e0 - 1, 0))
    bases = [base_prev] + [s[1] for s in segs] + [base_next]
    bufs = [(g_a, y_a), (g_b, y_b)]

    @pl.when(j == 0)
    def _():
        o_ref[...] = jnp.zeros_like(o_ref)
        y_b[...] = jnp.zeros_like(y_b)
        gather(g_a, bases[1])

    for k in range(grp):
        g_cur, y_cur = bufs[k % 2]
        g_other, y_other = bufs[(k + 1) % 2]
        experts(k, g_cur, y_cur, segs[k][2])
        gather(g_other, bases[k + 2])
        scatter(y_other, bases[k])

    @pl.when(j == nj - 1)
    def _():
        scatter(y_b, bases[grp])

    for k in range(grp):
        start, _, n_fast, n = segs[k]

        @pl.when(n > n_fast)
        def _(k=k, start=start, n_fast=n_fast, n=n):
            slow_path(k, start, n_fast, n, g_b, y_a)


def _moe_routed(x_tiles, tok, gate, offs, w_gate, w_up, w_down, layer, n_tiles):
    _, ne, d, f = w_gate.shape
    grp = MOE_GROUP
    assert grp % 2 == 0 and ne % grp == 0
    rows = MOE_TILE * SUBLANES
    slots = MOE_TILE * TOP_K
    grid_spec = pltpu.PrefetchScalarGridSpec(
        num_scalar_prefetch=1,
        grid=(n_tiles, ne // grp),
        in_specs=[
            pl.BlockSpec((rows, LANES), lambda i, j, offs: (i, 0), pipeline_mode=pl.Buffered(1)),
            pl.BlockSpec((slots,), lambda i, j, offs: (i,), memory_space=pltpu.SMEM),
            pl.BlockSpec((slots,), lambda i, j, offs: (i,), memory_space=pltpu.SMEM),
            pl.BlockSpec((None, grp, d, f), lambda i, j, offs: (layer, j, 0, 0)),
            pl.BlockSpec((None, grp, d, f), lambda i, j, offs: (layer, j, 0, 0)),
            pl.BlockSpec((None, grp, f, d), lambda i, j, offs: (layer, j, 0, 0)),
        ],
        out_specs=pl.BlockSpec((rows, LANES), lambda i, j, offs: (i, 0), pipeline_mode=pl.Buffered(1)),
        scratch_shapes=[pltpu.VMEM((MOE_ROWS * SUBLANES, LANES), F32)] * 4,
    )
    return pl.pallas_call(
        _moe_kernel,
        grid_spec=grid_spec,
        out_shape=jax.ShapeDtypeStruct(x_tiles.shape, F32),
        compiler_params=_cparams("arbitrary", "arbitrary"),
        name="moe_routed",
    )(offs, x_tiles, tok, gate, w_gate, w_up, w_down)


def _shared_kernel(final, h_ref, routed_ref, x1_ref, mod_ref, wg_ref, wu_ref, wd_ref, fg_ref, o_ref):
    h = h_ref[...].astype(BF16)
    hid = (_silu(_dot(h, wg_ref[...])) * _dot(h, wu_ref[...])).astype(BF16)
    tm = h.shape[0]
    routed = jnp.concatenate([routed_ref[pl.ds(s, tm, stride=SUBLANES), :] for s in range(SUBLANES)], axis=1)
    x2 = x1_ref[...] + mod_ref[0, 5:6, :] * (routed + _dot(hid, wd_ref[...]))
    if final:
        x2 = x2 * lax.rsqrt(jnp.mean(x2 * x2, axis=-1, keepdims=True) + RMS_EPS) * fg_ref[...]
    o_ref[...] = x2


def _shared_residual(h2, routed, x1, mod, ws_gate, ws_up, ws_down, final_gain, final, n_lat, seq, tm=512):
    t, d = x1.shape
    n_lat_tiles = n_lat // tm
    per_batch = seq // tm
    ctx_row = mod.shape[0] - 1

    def mod_map(i):
        return (jnp.where(i < n_lat_tiles, i // per_batch, ctx_row), 0, 0)

    const = lambda i: (0, 0)
    row = lambda i: (i, 0)
    full = lambda a: pl.BlockSpec(a.shape, const)
    return pl.pallas_call(
        functools.partial(_shared_kernel, final),
        grid=(t // tm,),
        in_specs=[pl.BlockSpec((tm, d), row), pl.BlockSpec((tm * SUBLANES, LANES), row), pl.BlockSpec((tm, d), row),
                  pl.BlockSpec((1, 6, d), mod_map), full(ws_gate), full(ws_up), full(ws_down),
                  pl.BlockSpec((1, d), const)],
        out_specs=pl.BlockSpec((tm, d), row),
        out_shape=jax.ShapeDtypeStruct((t, d), F32),
        compiler_params=_cparams("parallel"),
        name="shared_residual",
    )(h2, routed, x1, mod, ws_gate, ws_up, ws_down, final_gain.reshape(1, d))


MOD_ROWS = 16


def kernel(x, c, ctx, c_ctx, w_ada, b_ada, norm1_gain, norm2_gain, w_in, na_rpb, hg_lower_bounds, hg_norm_gain,
           w_fn_branch, w_na_branch, w_hg_branch, w_out, w_router, b_router, w_exp_gate, w_exp_up, w_exp_down,
           w_sh_gate, w_sh_up, w_sh_down, final_gain):
    b, n, d = x.shape
    nc = ctx.shape[1]
    depth = w_ada.shape[0]
    n_lat, n_ctx = b * n, b * nc

    sm = jax.nn.softmax(hg_lower_bounds.astype(F32), axis=0)
    cs = jnp.cumsum(sm, axis=0)
    lower = cs - cs[0]

    cin = jnp.concatenate([c, c_ctx[None, :], jnp.zeros((MOD_ROWS - b - 1, d), F32)], axis=0)
    mod = _ada_mod(cin, w_ada, b_ada)[:, :b + 1].reshape(depth, b + 1, 6, d)
    hg_consts = _hgrn_tables()
    zero_state = jnp.zeros((b, 2, HG_WIDTH, HG_WIDTH), F32)
    wx_gate, wx_up, wx_down = w_exp_gate.astype(BF16), w_exp_up.astype(BF16), w_exp_down.astype(BF16)

    x_all = jnp.concatenate([x.reshape(n_lat, d), ctx.reshape(n_ctx, d)], axis=0)
    for l in range(depth):
        last = l == depth - 1
        w = w_in[l].astype(BF16)
        c0, c1, c2 = FN_WIDTH, FN_WIDTH + 3 * NA_WIDTH, FN_WIDTH + 3 * NA_WIDTH + 5 * HG_WIDTH
        z_fn, z_qkv, z_hg = _proj_in(x_all, mod[l], norm1_gain[l], w[:, :c0], w[:, c0:c1], w[:, c1:c2], n_lat, n)

        lbp = _hgrn_lb_params(lower[l])
        of_c, ob_c, s_ctx = _hgrn(z_hg, zero_state, lbp, hg_consts, b, nc, n_lat)
        of_l, ob_l, _ = _hgrn(z_hg, s_ctx, lbp, hg_consts, b, n, 0)
        fn_l = _fourier_lat(z_fn, b, n)
        na_l = _na_lat(z_qkv, _na_bias_table(na_rpb[l]), b, n, nc)
        if last:
            n_rows = n_lat
            fn_a, na_a, of_a, ob_a = fn_l, na_l, of_l, ob_l
        else:
            n_rows = n_lat + n_ctx
            cat = lambda p, q: jnp.concatenate([p, q], axis=0)
            fn_a = jnp.concatenate([fn_l, _fourier_ctx(z_fn, b, nc, n_lat)], axis=1)
            na_a = cat(na_l, _na_ctx(z_qkv, b, n, nc))
            of_a, ob_a = cat(of_l, of_c), cat(ob_l, ob_c)
        x1, h2, h2_tiles = _merge(x_all, mod[l], norm1_gain[l], norm2_gain[l], fn_a, na_a, of_a, ob_a, z_hg, hg_norm_gain[l],
                        w_fn_branch[l].astype(BF16), w_na_branch[l].astype(BF16), w_hg_branch[l].astype(BF16),
                        w[:, c2:], w_out[l].astype(BF16), n_rows, n_lat, n)

        idx, gate, counts = _router(h2, w_router[l], b_router[l])
        n_tiles = -(-n_rows // MOE_TILE)
        tok, gate_sorted, offs = _moe_dispatch(idx[:, :TOP_K], gate[:, :TOP_K], counts[::SUBLANES], n_tiles)
        routed = _moe_routed(h2_tiles, tok, gate_sorted, offs, wx_gate, wx_up, wx_down, l, n_tiles)
        x_all = _shared_residual(h2, routed, x1, mod[l], w_sh_gate[l].astype(BF16),
                                 w_sh_up[l].astype(BF16), w_sh_down[l].astype(BF16), final_gain, last, n_lat, n)
    return x_all.reshape(b, n, d)
```

```python
import functools

import numpy as np
import jax
import jax.numpy as jnp
from jax import lax
from jax.experimental import pallas as pl
from jax.experimental.pallas import tpu as pltpu

D_MODEL = 1024
GRID_W = 64
RMS_EPS = 1e-6
FN_GROUPS = 4
FN_GROUP_DIM = 64
FN_WIDTH = 256
NA_HEADS = 8
NA_HEAD_DIM = 64
NA_WIDTH = 512
NA_WIN_ROWS = 8
NA_WIN_COLS = 16
HG_HEADS = 4
HG_HEAD_DIM = 64
HG_WIDTH = 256
N_EXPERTS = 256
TOP_K = 8
EXPERT_DIM = 256
ROUTE_SCALE = 2.5

LANES = 128
SUBLANES = 8

F32 = jnp.float32
BF16 = jnp.bfloat16
VMEM_LIMIT = 56 * 1024 * 1024


def _cparams(*sem):
    return pltpu.CompilerParams(dimension_semantics=sem, vmem_limit_bytes=VMEM_LIMIT)


def _dot(a, b):
    return jnp.dot(a, b, preferred_element_type=F32)


def _dot_nt(a, b):
    return lax.dot_general(a, b, (((1,), (1,)), ((), ())), preferred_element_type=F32)


def _sigmoid(z):
    e = jnp.exp(-jnp.abs(z))
    r = 1.0 / (1.0 + e)
    return jnp.where(z >= 0, r, e * r)


def _silu(z):
    return z * _sigmoid(z)


def _norm_mod(x, gain, shift, scale):
    y = x * lax.rsqrt(jnp.mean(x * x, axis=-1, keepdims=True) + RMS_EPS)
    return (y * gain) * (1.0 + scale) + shift


def _ada_kernel(c_ref, w_ref, b_ref, o_ref):
    s = _silu(c_ref[...]).astype(BF16)
    o_ref[0] = _dot(s, w_ref[0].astype(BF16)) + b_ref[0]


def _ada_mod(cin, w_ada, b_ada):
    depth, d, n = w_ada.shape
    r = cin.shape[0]
    tn = 1536
    return pl.pallas_call(
        _ada_kernel,
        grid=(depth, n // tn),
        in_specs=[
            pl.BlockSpec((r, d), lambda l, j: (0, 0)),
            pl.BlockSpec((1, d, tn), lambda l, j: (l, 0, j)),
            pl.BlockSpec((1, 1, tn), lambda l, j: (l, 0, j)),
        ],
        out_specs=pl.BlockSpec((1, r, tn), lambda l, j: (l, 0, j)),
        out_shape=jax.ShapeDtypeStruct((depth, r, n), F32),
        compiler_params=_cparams("parallel", "parallel"),
        name="ada_mod",
    )(cin, w_ada, b_ada.reshape(depth, 1, n))


def _proj_in_kernel(x_ref, mod_ref, gain_ref, wfn_ref, wqkv_ref, whg_ref, fn_ref, qkv_ref, hg_ref):
    h = _norm_mod(x_ref[...], gain_ref[...], mod_ref[0, 0:1, :], mod_ref[0, 1:2, :]).astype(BF16)
    fn_ref[...] = _dot(h, wfn_ref[...])
    qkv_ref[...] = _dot(h, wqkv_ref[...]).astype(BF16)
    hg_ref[...] = _dot(h, whg_ref[...])


def _proj_in(x_all, mod, gain, w_fn, w_qkv, w_hg, n_lat, seq, tm=512):
    t, d = x_all.shape
    n_lat_tiles = n_lat // tm
    per_batch = seq // tm
    ctx_row = mod.shape[0] - 1

    def mod_map(i):
        return (jnp.where(i < n_lat_tiles, i // per_batch, ctx_row), 0, 0)

    const = lambda i: (0, 0)
    row = lambda i: (i, 0)
    return pl.pallas_call(
        _proj_in_kernel,
        grid=(t // tm,),
        in_specs=[
            pl.BlockSpec((tm, d), row),
            pl.BlockSpec((1, 6, d), mod_map),
            pl.BlockSpec((1, d), const),
            pl.BlockSpec(w_fn.shape, const),
            pl.BlockSpec(w_qkv.shape, const),
            pl.BlockSpec(w_hg.shape, const),
        ],
        out_specs=[
            pl.BlockSpec((tm, w_fn.shape[1]), row),
            pl.BlockSpec((tm, w_qkv.shape[1]), row),
            pl.BlockSpec((tm, w_hg.shape[1]), row),
        ],
        out_shape=[
            jax.ShapeDtypeStruct((t, w_fn.shape[1]), F32),
            jax.ShapeDtypeStruct((t, w_qkv.shape[1]), BF16),
            jax.ShapeDtypeStruct((t, w_hg.shape[1]), F32),
        ],
        compiler_params=_cparams("parallel"),
        name="proj_in",
    )(x_all, mod, gain.reshape(1, d), w_fn, w_qkv, w_hg)


def _dft_cos_sin(n):
    k = np.arange(n)
    ang = 2.0 * np.pi * ((k[:, None] * k[None, :]) % n) / n
    return np.cos(ang), np.sin(ang)


def _channel_dft_mats(scale):
    c, s = _dft_cos_sin(FN_GROUP_DIM)
    eye = np.eye(FN_GROUPS)
    return (jnp.asarray(np.kron(eye, c) * scale, BF16), jnp.asarray(np.kron(eye, -s) * scale, BF16))


FFT_R = 64
FFT_GROUP = 4


def _store_halves(ref, lead, rows, val):
    ref[lead + (0, rows, slice(None))] = val[:, :LANES]
    ref[lead + (1, rows, slice(None))] = val[:, LANES:]


def _load_halves(ref, lead, rows):
    return jnp.concatenate([ref[lead + (0, rows, slice(None))], ref[lead + (1, rows, slice(None))]], axis=1)


def _fourier_lat_kernel(u_ref, cc_ref, sc_ref, ga_ref, gb_ref, tc_ref, ts_ref, o_ref, u_s, z_s):
    ub = u_ref[...].astype(BF16)
    every = slice(None)
    _store_halves(u_s, (0,), every, _dot(ub, cc_ref[...]))
    _store_halves(u_s, (1,), every, _dot(ub, sc_ref[...]))

    def stage_a(g, carry):
        n2s = [g * FFT_GROUP + i for i in range(FFT_GROUP)]
        xs = []
        for n2 in n2s:
            rows = pl.ds(n2, FFT_R, stride=FFT_R)
            xs.append(jnp.concatenate([_load_halves(u_s, (0,), rows), _load_halves(u_s, (1,), rows)],
                                      axis=0).astype(BF16))
        prods = [_dot(ga_ref[...], x) for x in xs]
        for n2, a in zip(n2s, prods):
            ar, ai = a[:FFT_R], a[FFT_R:]
            tc = jnp.concatenate([tc_ref[n2], tc_ref[n2]], axis=1)
            ts = jnp.concatenate([ts_ref[n2], ts_ref[n2]], axis=1)
            dst = pl.ds(pl.multiple_of(n2 * FFT_R, FFT_R), FFT_R)
            _store_halves(z_s, (0,), dst, ar * tc + ai * ts)
            _store_halves(z_s, (1,), dst, ai * tc - ar * ts)
        return carry

    lax.fori_loop(0, FFT_R // FFT_GROUP, stage_a, 0)

    def stage_b(g, carry):
        k1s = [g * FFT_GROUP + i for i in range(FFT_GROUP)]
        zs = []
        for k1 in k1s:
            rows = pl.ds(k1, FFT_R, stride=FFT_R)
            zs.append(jnp.concatenate([_load_halves(z_s, (0,), rows), _load_halves(z_s, (1,), rows)],
                                      axis=0).astype(BF16))
        prods = [_dot(gb_ref[...], z) for z in zs]
        for k1, p in zip(k1s, prods):
            _store_halves(o_ref, (), pl.ds(k1, FFT_R, stride=FFT_R), p)
        return carry

    lax.fori_loop(0, FFT_R // FFT_GROUP, stage_b, 0)


def _fourier_lat(z_fn, n_batch, seq):
    assert seq == FFT_R * FFT_R
    c, s = _dft_cos_sin(FFT_R)
    cc, sc = _channel_dft_mats((seq * FN_GROUP_DIM) ** -0.5)
    ga = jnp.asarray(np.block([[c, s], [-s, c]]), BF16)
    gb = jnp.asarray(np.concatenate([c, s], axis=1), BF16)
    k = np.arange(FFT_R)
    ang = 2.0 * np.pi * (k[:, None] * k[None, :]) / seq
    tc = jnp.asarray(np.broadcast_to(np.cos(ang)[:, :, None], (FFT_R, FFT_R, 128)), F32)
    ts = jnp.asarray(np.broadcast_to(np.sin(ang)[:, :, None], (FFT_R, FFT_R, 128)), F32)
    const2 = lambda b: (0, 0)
    const3 = lambda b: (0, 0, 0)
    return pl.pallas_call(
        _fourier_lat_kernel,
        grid=(n_batch,),
        in_specs=[
            pl.BlockSpec((seq, FN_WIDTH), lambda b: (b, 0)),
            pl.BlockSpec(cc.shape, const2),
            pl.BlockSpec(sc.shape, const2),
            pl.BlockSpec(ga.shape, const2),
            pl.BlockSpec(gb.shape, const2),
            pl.BlockSpec(tc.shape, const3),
            pl.BlockSpec(ts.shape, const3),
        ],
        out_specs=pl.BlockSpec((2, seq, LANES), lambda b: (0, b, 0)),
        out_shape=jax.ShapeDtypeStruct((2, n_batch * seq, LANES), F32),
        scratch_shapes=[pltpu.VMEM((2, 2, seq, LANES), F32), pltpu.VMEM((2, 2, seq, LANES), F32)],
        compiler_params=_cparams("parallel"),
        name="fourier_lat",
    )(z_fn, cc, sc, ga, gb, tc, ts)


def _fourier_ctx_kernel(u_ref, cc_ref, sc_ref, g_ref, o_ref):
    ub = u_ref[...].astype(BF16)
    x = jnp.concatenate([_dot(ub, cc_ref[...]), _dot(ub, sc_ref[...])], axis=0).astype(BF16)
    _store_halves(o_ref, (), slice(None), _dot(g_ref[...], x))


def _fourier_ctx(z_fn, n_batch, n_ctx, row0):
    blk0 = row0 // n_ctx
    c, s = _dft_cos_sin(n_ctx)
    cc, sc = _channel_dft_mats((n_ctx * FN_GROUP_DIM) ** -0.5)
    g = jnp.asarray(np.concatenate([c, s], axis=1), BF16)
    const2 = lambda b: (0, 0)
    return pl.pallas_call(
        _fourier_ctx_kernel,
        grid=(n_batch,),
        in_specs=[
            pl.BlockSpec((n_ctx, FN_WIDTH), lambda b: (blk0 + b, 0)),
            pl.BlockSpec(cc.shape, const2),
            pl.BlockSpec(sc.shape, const2),
            pl.BlockSpec(g.shape, const2),
        ],
        out_specs=pl.BlockSpec((2, n_ctx, LANES), lambda b: (0, b, 0)),
        out_shape=jax.ShapeDtypeStruct((2, n_batch * n_ctx, LANES), F32),
        compiler_params=_cparams("parallel"),
        name="fourier_ctx",
    )(z_fn, cc, sc, g)


NEG_BIG = -1e30
HEAD_PAIR = 2 * NA_HEAD_DIM


def _na_bias_table(rpb):
    wr, wc, w = NA_WIN_ROWS, NA_WIN_COLS, GRID_W
    col = np.arange(w)
    col_start = np.clip(col - wc // 2, 0, w - wc)
    in_win = (col[None, :] >= col_start[:, None]) & (col[None, :] < col_start[:, None] + wc)
    dc_idx = np.clip(col[None, :] - col[:, None], 1 - wc, wc - 1) + (wc - 1)
    h, n_dr, n_dc = rpb.shape
    onehot = jnp.asarray(dc_idx.reshape(-1)[None, :] == np.arange(n_dc)[:, None], F32)
    t = jnp.dot(rpb.astype(F32).reshape(h * n_dr, n_dc), onehot, precision=lax.Precision.HIGHEST)
    t = jnp.where(in_win.reshape(-1)[None, :], t, NEG_BIG).reshape(h, n_dr, w, w)
    per_s = [jnp.concatenate([t[:, s + j] for j in range(wr)], axis=-1) for s in range(wr)]
    return jnp.stack(per_s, axis=1)


def _softmax_pv_many(problems):
    scores = []
    for q, keys, vals, biases in problems:
        s = []
        for k, b in zip(keys, biases):
            si = _dot_nt(q, k)
            s.append(si if b is None else si + b)
        scores.append(s)
    maxes = []
    for s in scores:
        m = s[0].max(axis=-1, keepdims=True)
        for si in s[1:]:
            m = jnp.maximum(m, si.max(axis=-1, keepdims=True))
        maxes.append(m)
    probs = [[jnp.exp(si - m) for si in s] for s, m in zip(scores, maxes)]
    outs = []
    for (q, keys, vals, biases), ps in zip(problems, probs):
        acc, l = None, None
        for p, v in zip(ps, vals):
            li = p.sum(axis=-1, keepdims=True)
            oi = _dot(p.astype(BF16), v)
            acc = oi if acc is None else acc + oi
            l = li if l is None else l + li
        outs.append(acc / l)
    return outs


def _softmax_pv(q, keys, vals, biases):
    return _softmax_pv_many([(q, keys, vals, biases)])[0]


NA_ROW_GROUP = 4


def _na_lat_kernel(q_ref, k_ref, v_ref, kc_ref, vc_ref, bias_ref, o_ref):
    w = GRID_W
    n_loc = NA_WIN_ROWS * w
    rows = q_ref.shape[0] // w
    lane = lax.broadcasted_iota(jnp.int32, (w, HEAD_PAIR), 1)
    first = lane < NA_HEAD_DIM
    kc = kc_ref[...]
    vc = vc_ref[...]

    def body(g, carry):
        problems = []
        for i in range(NA_ROW_GROUP):
            r = g * NA_ROW_GROUP + i
            kr0 = jnp.clip(r - NA_WIN_ROWS // 2, 0, rows - NA_WIN_ROWS)
            s = kr0 - r + (NA_WIN_ROWS - 1)
            q = q_ref[pl.ds(pl.multiple_of(r * w, w), w), :] * jnp.asarray(NA_HEAD_DIM ** -0.5, BF16)
            ks = k_ref[pl.ds(pl.multiple_of(kr0 * w, w), n_loc), :]
            vs = v_ref[pl.ds(pl.multiple_of(kr0 * w, w), n_loc), :]
            zero = jnp.zeros_like(q)
            q2 = jnp.concatenate([jnp.where(first, q, zero), jnp.where(first, zero, q)], axis=0)
            bias = jnp.concatenate([bias_ref[0, s], bias_ref[1, s]], axis=0)
            problems.append((q2, [ks, kc], [vs, vc], [bias, None]))
        for i, o2 in enumerate(_softmax_pv_many(problems)):
            r = g * NA_ROW_GROUP + i
            o_ref[pl.ds(pl.multiple_of(r * w, w), w), :] = jnp.where(first, o2[:w], o2[w:]).astype(o_ref.dtype)
        return carry

    lax.fori_loop(0, rows // NA_ROW_GROUP, body, 0)


def _na_lat(qkv, bias, n_batch, seq, n_ctx):
    n_pairs = NA_WIDTH // HEAD_PAIR
    ctx0 = n_batch * seq // n_ctx
    wr, w = NA_WIN_ROWS, GRID_W
    return pl.pallas_call(
        _na_lat_kernel,
        grid=(n_batch, n_pairs),
        in_specs=[
            pl.BlockSpec((seq, HEAD_PAIR), lambda b, p: (b, p)),
            pl.BlockSpec((seq, HEAD_PAIR), lambda b, p: (b, n_pairs + p)),
            pl.BlockSpec((seq, HEAD_PAIR), lambda b, p: (b, 2 * n_pairs + p)),
            pl.BlockSpec((n_ctx, HEAD_PAIR), lambda b, p: (ctx0 + b, n_pairs + p)),
            pl.BlockSpec((n_ctx, HEAD_PAIR), lambda b, p: (ctx0 + b, 2 * n_pairs + p)),
            pl.BlockSpec((2, wr, w, wr * w), lambda b, p: (p, 0, 0, 0)),
        ],
        out_specs=pl.BlockSpec((seq, HEAD_PAIR), lambda b, p: (b, p)),
        out_shape=jax.ShapeDtypeStruct((n_batch * seq, NA_WIDTH), BF16),
        compiler_params=_cparams("parallel", "parallel"),
        name="na_lat",
    )(qkv, qkv, qkv, qkv, qkv, bias)


def _na_ctx_kernel(q_ref, k_ref, v_ref, o_ref):
    lane = lax.broadcasted_iota(jnp.int32, q_ref.shape, 1)
    first = lane < NA_HEAD_DIM
    q = q_ref[...] * jnp.asarray(NA_HEAD_DIM ** -0.5, BF16)
    k = k_ref[...]
    v = v_ref[...]
    zero = jnp.zeros_like(q)
    o0 = _softmax_pv(jnp.where(first, q, zero), [k], [v], [None])
    o1 = _softmax_pv(jnp.where(first, zero, q), [k], [v], [None])
    o_ref[...] = jnp.where(first, o0, o1).astype(o_ref.dtype)


def _na_ctx(qkv, n_batch, seq, n_ctx):
    n_pairs = NA_WIDTH // HEAD_PAIR
    ctx0 = n_batch * seq // n_ctx
    return pl.pallas_call(
        _na_ctx_kernel,
        grid=(n_batch, n_pairs),
        in_specs=[
            pl.BlockSpec((n_ctx, HEAD_PAIR), lambda b, p: (ctx0 + b, p)),
            pl.BlockSpec((n_ctx, HEAD_PAIR), lambda b, p: (ctx0 + b, n_pairs + p)),
            pl.BlockSpec((n_ctx, HEAD_PAIR), lambda b, p: (ctx0 + b, 2 * n_pairs + p)),
        ],
        out_specs=pl.BlockSpec((n_ctx, HEAD_PAIR), lambda b, p: (b, p)),
        out_shape=jax.ShapeDtypeStruct((n_batch * n_ctx, NA_WIDTH), BF16),
        compiler_params=_cparams("parallel", "parallel"),
        name="na_ctx",
    )(qkv, qkv, qkv)


HG_CHUNK = 64
HG_LEVELS = (32, 16)
HG_DIAG = 16
HG_BATCH = 4


def _hgrn_tables():
    c = HG_CHUNK
    tri, masks, refs, last = [], [], [], []
    for reverse in (False, True):
        u = np.arange(c)[::-1] if reverse else np.arange(c)
        row_of = {int(uu): t for t, uu in enumerate(u)}
        ut, uj = u[:, None], u[None, :]
        tri.append(uj <= ut)
        lv, rf = [], []
        for h in HG_LEVELS:
            same = (ut // (2 * h)) == (uj // (2 * h))
            up_t, up_j = (ut % (2 * h)) >= h, (uj % (2 * h)) >= h
            lv.append(same & up_t & ~up_j)
            rf.append([row_of[int(uu) // (2 * h) * (2 * h) + h - 1] for uu in u])
        same_d = (ut // HG_DIAG) == (uj // HG_DIAG)
        lv.append(same_d & (uj <= ut))
        rf.append([row_of.get(int(uu) // HG_DIAG * HG_DIAG - 1, -1) for uu in u])
        masks.append(np.stack([np.tile(x, (1, HG_HEADS)) for x in lv]).astype(np.float32))
        refs.append(rf)
        last.append(row_of[c - 1])
    hm = (np.arange(HG_WIDTH)[:, None] // HG_HEAD_DIM) == (np.arange(HG_WIDTH)[None, :] // HG_HEAD_DIM)
    arrays = (jnp.asarray(np.stack(tri), BF16), jnp.asarray(np.stack(masks), F32), jnp.asarray(hm, F32))
    return arrays, refs, last


def _ref_rows(b, ref):
    pieces, t = [], 0
    while t < len(ref):
        t1 = t
        while t1 < len(ref) and ref[t1] == ref[t]:
            t1 += 1
        shape = (t1 - t, b.shape[1])
        pieces.append(jnp.zeros(shape, b.dtype) if ref[t] < 0 else jnp.broadcast_to(b[ref[t]:ref[t] + 1, :], shape))
        t = t1
    return jnp.concatenate(pieces, axis=0)


def _split3(x):
    hi = x.astype(BF16)
    r = x - hi.astype(F32)
    mid = r.astype(BF16)
    lo = (r - mid.astype(F32)).astype(BF16)
    return hi, mid, lo


def _hgrn_chunks(chains, hm, refs, last):
    hm_b = hm.astype(BF16)

    def heads_bd(x):
        return jnp.concatenate([x.astype(BF16)] * HG_HEADS, axis=0) * hm_b

    qs, ks, splits = [], [], []
    for zq, zv, zf, lbp, tri, masks, st, d in chains:
        e = jnp.exp(-jnp.abs(zf))
        inv = 1.0 / (1.0 + e)
        log_sig = jnp.minimum(zf, 0.0) - jnp.log(1.0 + e)
        sig_neg = jnp.where(zf >= 0, e * inv, inv)
        a = lbp[0:1, :]
        cc = lbp[1:2, :] + log_sig
        log_f = jnp.maximum(a, cc) + jnp.log(1.0 + jnp.exp(-jnp.abs(a - cc)))
        qs.append(_silu(zq))
        ks.append(lbp[2:3, :] * sig_neg)
        splits.append(_split3(log_f))
    bs = [_dot(c[4], hi) + _dot(c[4], mid) + _dot(c[4], lo) for c, (hi, mid, lo) in zip(chains, splits)]
    b_lasts = [b[last[c[7]]:last[c[7]] + 1] for c, b in zip(chains, bs)]
    a_alls = [None] * len(chains)
    for i in range(len(HG_LEVELS) + 1):
        pairs = []
        for c, q, k, b in zip(chains, qs, ks, bs):
            rel = b - _ref_rows(b, refs[c[7]][i])
            if i < len(HG_LEVELS):
                pairs.append((q * jnp.exp(jnp.minimum(rel, 0.0)), k * jnp.exp(jnp.minimum(-rel, 0.0))))
            else:
                pairs.append((q * jnp.exp(rel), k * jnp.exp(-rel)))
        for n, (c, (qx, kx)) in enumerate(zip(chains, pairs)):
            ai = _dot_nt(qx.astype(BF16), heads_bd(kx)) * c[5][i]
            a_alls[n] = ai if a_alls[n] is None else a_alls[n] + ai
    outs = [_dot(a_all.astype(BF16), heads_bd(c[1])) + _dot_nt((q * jnp.exp(b)).astype(BF16), c[6].astype(BF16))
            for c, q, b, a_all in zip(chains, qs, bs, a_alls)]
    states = []
    for c, k, b, b_last in zip(chains, ks, bs, b_lasts):
        kl = (k * jnp.exp(b_last - b)).astype(BF16)
        upd = lax.dot_general(c[1].astype(BF16), kl, (((0,), (0,)), ((), ())), preferred_element_type=F32)
        states.append(c[6] * jnp.exp(b_last) + upd * hm)
    return outs, states


def _hgrn_kernel(refs, last, *args):
    nb = HG_BATCH
    z_refs = args[:3 * nb]
    s0_ref, lbp_ref, tri_ref, mask_ref, hm_ref = args[3 * nb:3 * nb + 5]
    of_ref, ob_ref, sfin_ref, st = args[3 * nb + 5:]
    i = pl.program_id(1)

    @pl.when(i == 0)
    def _():
        st[...] = s0_ref[...]

    w = HG_WIDTH
    hm = hm_ref[...]
    chains = []
    for bb in range(nb):
        zf_ref, zb_ref, zbg_ref = z_refs[3 * bb:3 * bb + 3]
        chains.append((zf_ref[:, 0:w], zf_ref[:, w:2 * w], zf_ref[:, 2 * w:3 * w], lbp_ref[0], tri_ref[0],
                       mask_ref[0], st[bb, 0], 0))
        chains.append((zb_ref[:, 0:w], zb_ref[:, w:2 * w], zbg_ref[...], lbp_ref[1], tri_ref[1],
                       mask_ref[1], st[bb, 1], 1))
    outs, states = _hgrn_chunks(chains, hm, refs, last)
    for bb in range(nb):
        of_ref[bb] = outs[2 * bb]
        ob_ref[bb] = outs[2 * bb + 1]
        st[bb, 0] = states[2 * bb]
        st[bb, 1] = states[2 * bb + 1]

    @pl.when(i == pl.num_programs(1) - 1)
    def _():
        sfin_ref[...] = st[...]


def _hgrn(z_hg, s0, lbp, tables, n_batch, n_tok, row0):
    (tri, masks, hm), refs, last = tables
    c, w, nb = HG_CHUNK, HG_WIDTH, HG_BATCH
    assert n_batch % nb == 0 and n_tok % c == 0
    nch = n_tok // c
    base = row0 // c
    const3 = lambda b, i: (0, 0, 0)
    in_specs = []
    for bb in range(nb):
        fwd = lambda b, i, bb=bb: (base + (b * nb + bb) * nch + i, 0)
        bwd = lambda b, i, bb=bb: (base + (b * nb + bb) * nch + (nch - 1 - i), 0)
        bwd_gate = lambda b, i, bb=bb: (base + (b * nb + bb) * nch + (nch - 1 - i), 3)
        in_specs += [pl.BlockSpec((c, 3 * w), fwd), pl.BlockSpec((c, 2 * w), bwd), pl.BlockSpec((c, w), bwd_gate)]
    in_specs += [
        pl.BlockSpec((nb, 2, w, w), lambda b, i: (b, 0, 0, 0)),
        pl.BlockSpec(lbp.shape, const3),
        pl.BlockSpec(tri.shape, const3),
        pl.BlockSpec(masks.shape, lambda b, i: (0, 0, 0, 0)),
        pl.BlockSpec(hm.shape, lambda b, i: (0, 0)),
    ]
    o_f, o_b, s_fin = pl.pallas_call(
        functools.partial(_hgrn_kernel, refs, last),
        grid=(n_batch // nb, nch),
        in_specs=in_specs,
        out_specs=[
            pl.BlockSpec((nb, c, w), lambda b, i: (b, i, 0)),
            pl.BlockSpec((nb, c, w), lambda b, i: (b, nch - 1 - i, 0)),
            pl.BlockSpec((nb, 2, w, w), lambda b, i: (b, 0, 0, 0)),
        ],
        out_shape=[
            jax.ShapeDtypeStruct((n_batch, n_tok, w), F32),
            jax.ShapeDtypeStruct((n_batch, n_tok, w), F32),
            jax.ShapeDtypeStruct((n_batch, 2, w, w), F32),
        ],
        scratch_shapes=[pltpu.VMEM((nb, 2, w, w), F32)],
        compiler_params=_cparams("parallel", "arbitrary"),
        name="hgrn",
    )(*([z_hg] * (3 * nb)), s0, lbp, tri, masks, hm)
    return o_f.reshape(n_batch * n_tok, w), o_b.reshape(n_batch * n_tok, w), s_fin


def _hgrn_lb_params(lower_l):
    rows = jnp.stack([jnp.log(lower_l), jnp.log1p(-lower_l), 1.0 - lower_l], axis=1)
    return jnp.pad(rows, ((0, 0), (0, 5), (0, 0)))


def _merge_kernel(x_ref, mod_ref, g1_ref, g2_ref, fn_ref, na_ref, of_ref, ob_ref, hgz_ref, hgain_ref,
                  hmean_ref, wfn_ref, wna_ref, whg_ref, wbg_ref, wout_ref, x1_ref, h2_ref, h2t_ref):
    d = x_ref.shape[1]
    x = x_ref[...]
    h = _norm_mod(x, g1_ref[...], mod_ref[0, 0:1, :], mod_ref[0, 1:2, :]).astype(BF16)
    o = of_ref[...] + ob_ref[...]
    hi, mid, lo = _split3(o * o)
    ms = _dot(hi, hmean_ref[...]) + _dot(mid, hmean_ref[...]) + _dot(lo, hmean_ref[...])
    y_hg_in = o * lax.rsqrt(ms + RMS_EPS) * hgain_ref[...] * _silu(hgz_ref[...])
    y_fn = _dot(jnp.concatenate([fn_ref[0], fn_ref[1]], axis=1).astype(BF16), wfn_ref[...])
    y_na = _dot(na_ref[...], wna_ref[...])
    y_hg = _dot(y_hg_in.astype(BF16), whg_ref[...])
    m = (_sigmoid(_dot(h, wbg_ref[:, 0:d])) * y_fn
         + _sigmoid(_dot(h, wbg_ref[:, d:2 * d])) * y_na
         + _sigmoid(_dot(h, wbg_ref[:, 2 * d:3 * d])) * y_hg)
    x1 = x + mod_ref[0, 2:3, :] * _dot(m.astype(BF16), wout_ref[...])
    x1_ref[...] = x1
    h2 = _norm_mod(x1, g2_ref[...], mod_ref[0, 3:4, :], mod_ref[0, 4:5, :])
    h2_ref[...] = h2
    tm = x.shape[0]
    for s in range(SUBLANES):
        h2t_ref[pl.ds(s, tm, stride=SUBLANES), :] = h2[:, s * LANES:(s + 1) * LANES]


def _merge(x_all, mod, gain1, gain2, fn, na, o_f, o_b, z_hg, hgain, w_fnb, w_nab, w_hgb, w_bg, w_out,
           n_rows, n_lat, seq, tm=512):
    t, d = n_rows, x_all.shape[1]
    n_lat_tiles = n_lat // tm
    per_batch = seq // tm
    ctx_row = mod.shape[0] - 1
    w = HG_WIDTH
    hmean = jnp.asarray(np.kron(np.eye(HG_HEADS), np.full((HG_HEAD_DIM, HG_HEAD_DIM), 1.0 / HG_HEAD_DIM)), BF16)

    def mod_map(i):
        return (jnp.where(i < n_lat_tiles, i // per_batch, ctx_row), 0, 0)

    const = lambda i: (0, 0)
    row = lambda i: (i, 0)
    full = lambda a: pl.BlockSpec(a.shape, const)
    return pl.pallas_call(
        _merge_kernel,
        grid=(t // tm,),
        in_specs=[
            pl.BlockSpec((tm, d), row),
            pl.BlockSpec((1, 6, d), mod_map),
            pl.BlockSpec((1, d), const),
            pl.BlockSpec((1, d), const),
            pl.BlockSpec((2, tm, LANES), lambda i: (0, i, 0)),
            pl.BlockSpec((tm, NA_WIDTH), row),
            pl.BlockSpec((tm, w), row),
            pl.BlockSpec((tm, w), row),
            pl.BlockSpec((tm, w), lambda i: (i, 4)),
            pl.BlockSpec((1, w), const),
            full(hmean), full(w_fnb), full(w_nab), full(w_hgb), full(w_bg), full(w_out),
        ],
        out_specs=[pl.BlockSpec((tm, d), row), pl.BlockSpec((tm, d), row), pl.BlockSpec((tm * SUBLANES, LANES), row)],
        out_shape=[jax.ShapeDtypeStruct((t, d), F32), jax.ShapeDtypeStruct((t, d), F32),
                   jax.ShapeDtypeStruct((t * SUBLANES, LANES), F32)],
        compiler_params=_cparams("parallel"),
        name="merge",
    )(x_all, mod, gain1.reshape(1, d), gain2.reshape(1, d), fn, na, o_f, o_b, z_hg,
      jnp.tile(hgain, HG_HEADS).reshape(1, w), hmean, w_fnb, w_nab, w_hgb, w_bg, w_out)


def _router_kernel(h_ref, whi_ref, wlo_ref, b_ref, idx_ref, gate_ref, cnt_ref):
    h = h_ref[...]
    hi = h.astype(BF16)
    lo = (h - hi.astype(F32)).astype(BF16)
    logits = _dot(hi, whi_ref[...]) + (_dot(hi, wlo_ref[...]) + _dot(lo, whi_ref[...]))
    scores = _sigmoid(logits)
    sel = scores + b_ref[...]
    tm, ne = sel.shape
    col = lax.broadcasted_iota(jnp.int32, (tm, ne), 1)
    out_lane = lax.broadcasted_iota(jnp.int32, (tm, LANES), 1)
    idx_out = jnp.zeros((tm, LANES), jnp.int32)
    gate_out = jnp.zeros((tm, LANES), F32)
    total = jnp.zeros((tm, 1), F32)
    chosen = jnp.zeros((tm, ne), F32)
    for k in range(TOP_K):
        m = sel.max(axis=-1, keepdims=True)
        idx = jnp.where(sel == m, col, ne).min(axis=-1, keepdims=True)
        hit = col == idx
        g = jnp.where(hit, scores, 0.0).sum(axis=-1, keepdims=True)
        sel = jnp.where(hit, -jnp.inf, sel)
        chosen = jnp.where(hit, 1.0, chosen)
        idx_out = jnp.where(out_lane == k, idx, idx_out)
        gate_out = jnp.where(out_lane == k, g, gate_out)
        total = total + g
    idx_ref[...] = idx_out
    gate_ref[...] = gate_out * (ROUTE_SCALE / total)
    cnt_ref[...] = jnp.broadcast_to(chosen.sum(axis=0, keepdims=True), cnt_ref.shape).astype(jnp.int32)


def _router(h2, w_router, b_router, tm=512):
    t, d = h2.shape
    ne = w_router.shape[1]
    w_hi = w_router.astype(BF16)
    w_lo = (w_router - w_hi.astype(F32)).astype(BF16)
    const = lambda i: (0, 0)
    row = lambda i: (i, 0)
    return pl.pallas_call(
        _router_kernel,
        grid=(t // tm,),
        in_specs=[pl.BlockSpec((tm, d), row), pl.BlockSpec((d, ne), const), pl.BlockSpec((d, ne), const),
                  pl.BlockSpec((1, ne), const)],
        out_specs=[pl.BlockSpec((tm, LANES), row), pl.BlockSpec((tm, LANES), row), pl.BlockSpec((SUBLANES, ne), row)],
        out_shape=[jax.ShapeDtypeStruct((t, LANES), jnp.int32), jax.ShapeDtypeStruct((t, LANES), F32),
                   jax.ShapeDtypeStruct((t // tm * SUBLANES, ne), jnp.int32)],
        compiler_params=_cparams("parallel"),
        name="router",
    )(h2, w_hi, w_lo, b_router.reshape(1, ne).astype(F32))


MOE_TILE = 4352
ROUTER_TILE = 256
MOE_ROWS = 160
MOE_GROUP = 4
MOE_SCATTER_GROUP = 4


def _moe_dispatch(idx, gate, counts, n_tiles, tile):
    t, k = idx.shape
    pad = n_tiles * tile - t
    slots = tile * k
    e = jnp.pad(idx, ((0, pad), (0, 0)), constant_values=N_EXPERTS).reshape(n_tiles, slots)
    g = jnp.pad(gate, ((0, pad), (0, 0))).reshape(n_tiles, slots)
    key = e * slots + jnp.arange(slots, dtype=jnp.int32)[None, :]
    key_sorted, g_sorted = lax.sort((key, g), dimension=1, num_keys=1)
    row8_sorted = (key_sorted % slots) // k * SUBLANES
    group = tile // (t // counts.shape[0])
    c = jnp.pad(counts, ((0, n_tiles * group - counts.shape[0]), (0, 0))).reshape(n_tiles, group, -1).sum(axis=1)
    offs = jnp.concatenate([jnp.zeros((n_tiles, 1), jnp.int32), jnp.cumsum(c, axis=1)], axis=1)
    offs = jnp.pad(offs.astype(jnp.int32).reshape(-1), (0, SUBLANES))
    return row8_sorted.reshape(-1), g_sorted.reshape(-1), offs


def _moe_kernel(offs_ref, x_ref, row_ref, gate_ref, wg_ref, wu_ref, wd_ref, o_ref, g_a, g_b, y_a, y_b):
    i = pl.program_id(0)
    j = pl.program_id(1)
    nj = pl.num_programs(1)
    r, sl = MOE_ROWS, SUBLANES
    n_slots = row_ref.shape[0]
    all_rows = list(range(r))

    def segment(ee):
        start = offs_ref[i * (N_EXPERTS + 1) + ee]
        n = offs_ref[i * (N_EXPERTS + 1) + ee + 1] - start
        n_fast = jnp.where(start <= n_slots - r, jnp.minimum(n, r), 0)
        return start, jnp.minimum(start, n_slots - r), n_fast, n

    def gather(buf, base):
        for row in all_rows:
            buf[pl.ds(row * sl, sl), :] = x_ref[pl.ds(pl.multiple_of(row_ref[base + row], sl), sl), :]

    def experts(k, buf_in, buf_out, n_valid):
        x = jnp.concatenate([buf_in[pl.ds(s, r, stride=sl), :] for s in range(sl)], axis=1).astype(BF16)
        hid = (_silu(_dot(x, wg_ref[k])) * _dot(x, wu_ref[k])).astype(BF16)
        y = _dot(hid, wd_ref[k])
        y = jnp.where(lax.broadcasted_iota(jnp.int32, (r, 1), 0) < n_valid, y, 0.0)
        for s in range(sl):
            buf_out[pl.ds(s, r, stride=sl), :] = y[:, s * LANES:(s + 1) * LANES]

    def scatter(buf, base):
        for g0 in range(0, r, MOE_SCATTER_GROUP):
            dsts, news = [], []
            for row in all_rows[g0:g0 + MOE_SCATTER_GROUP]:
                dst = pl.ds(pl.multiple_of(row_ref[base + row], sl), sl)
                dsts.append(dst)
                news.append(o_ref[dst, :] + gate_ref[base + row] * buf[pl.ds(row * sl, sl), :])
            for dst, new in reversed(list(zip(dsts, news))):
                o_ref[dst, :] = new

    def slow_path(k, start, n_fast, n, g_free, y_free):
        def chunk(c, carry):
            base = start + n_fast + c * r
            n_valid = jnp.minimum(n - n_fast - c * r, r)

            def slot(row):
                return jnp.minimum(base + row, n_slots - 1)

            def gather_group(gi, carry):
                for row in [gi * sl + q for q in range(sl)]:
                    src = pl.ds(pl.multiple_of(row_ref[slot(row)], sl), sl)
                    g_free[pl.ds(pl.multiple_of(row * sl, sl), sl), :] = x_ref[src, :]
                return carry

            def scatter_group(gi, carry):
                dsts, news = [], []
                for row in [gi * sl + q for q in range(sl)]:
                    dst = pl.ds(pl.multiple_of(row_ref[slot(row)], sl), sl)
                    dsts.append(dst)
                    news.append(o_ref[dst, :] + gate_ref[slot(row)] * y_free[pl.ds(pl.multiple_of(row * sl, sl), sl), :])
                for dst, new in reversed(list(zip(dsts, news))):
                    o_ref[dst, :] = new
                return carry

            lax.fori_loop(0, r // sl, gather_group, 0)
            experts(k, g_free, y_free, n_valid)
            lax.fori_loop(0, r // sl, scatter_group, 0)
            return carry

        lax.fori_loop(0, (n - n_fast + r - 1) // r, chunk, 0)

    grp = MOE_GROUP
    e0 = grp * j
    segs = [segment(e0 + k) for k in range(grp)]
    _, base_next, _, _ = segment(jnp.minimum(e0 + grp, N_EXPERTS - 1))
    _, base_prev, _, _ = segment(jnp.maximum(e0 - 1, 0))
    bases = [base_prev] + [s[1] for s in segs] + [base_next]
    bufs = [(g_a, y_a), (g_b, y_b)]

    @pl.when(j == 0)
    def _():
        o_ref[...] = jnp.zeros_like(o_ref)
        y_b[...] = jnp.zeros_like(y_b)
        gather(g_a, bases[1])

    for k in range(grp):
        g_cur, y_cur = bufs[k % 2]
        g_other, y_other = bufs[(k + 1) % 2]
        experts(k, g_cur, y_cur, segs[k][2])
        gather(g_other, bases[k + 2])
        scatter(y_other, bases[k])

    @pl.when(j == nj - 1)
    def _():
        scatter(y_b, bases[grp])

    for k in range(grp):
        start, _, n_fast, n = segs[k]

        @pl.when(n > n_fast)
        def _(k=k, start=start, n_fast=n_fast, n=n):
            slow_path(k, start, n_fast, n, g_b, y_a)


def _moe_routed(x_tiles, tok, gate, offs, w_gate, w_up, w_down, layer, n_tiles, tile):
    _, ne, d, f = w_gate.shape
    grp = MOE_GROUP
    assert grp % 2 == 0 and ne % grp == 0
    rows = tile * SUBLANES
    slots = tile * TOP_K
    grid_spec = pltpu.PrefetchScalarGridSpec(
        num_scalar_prefetch=1,
        grid=(n_tiles, ne // grp),
        in_specs=[
            pl.BlockSpec((rows, LANES), lambda i, j, offs: (i, 0), pipeline_mode=pl.Buffered(1)),
            pl.BlockSpec((slots,), lambda i, j, offs: (i,), memory_space=pltpu.SMEM, pipeline_mode=pl.Buffered(1)),
            pl.BlockSpec((slots,), lambda i, j, offs: (i,), memory_space=pltpu.SMEM, pipeline_mode=pl.Buffered(1)),
            pl.BlockSpec((None, grp, d, f), lambda i, j, offs: (layer, j, 0, 0)),
            pl.BlockSpec((None, grp, d, f), lambda i, j, offs: (layer, j, 0, 0)),
            pl.BlockSpec((None, grp, f, d), lambda i, j, offs: (layer, j, 0, 0)),
        ],
        out_specs=pl.BlockSpec((rows, LANES), lambda i, j, offs: (i, 0), pipeline_mode=pl.Buffered(1)),
        scratch_shapes=[pltpu.VMEM((MOE_ROWS * SUBLANES, LANES), F32)] * 4,
    )
    return pl.pallas_call(
        _moe_kernel,
        grid_spec=grid_spec,
        out_shape=jax.ShapeDtypeStruct(x_tiles.shape, F32),
        compiler_params=_cparams("arbitrary", "arbitrary"),
        name="moe_routed",
    )(offs, x_tiles, tok, gate, w_gate, w_up, w_down)


def _shared_kernel(final, h_ref, routed_ref, x1_ref, mod_ref, wg_ref, wu_ref, wd_ref, fg_ref, o_ref):
    h = h_ref[...].astype(BF16)
    hid = (_silu(_dot(h, wg_ref[...])) * _dot(h, wu_ref[...])).astype(BF16)
    tm = h.shape[0]
    routed = jnp.concatenate([routed_ref[pl.ds(s, tm, stride=SUBLANES), :] for s in range(SUBLANES)], axis=1)
    x2 = x1_ref[...] + mod_ref[0, 5:6, :] * (routed + _dot(hid, wd_ref[...]))
    if final:
        x2 = x2 * lax.rsqrt(jnp.mean(x2 * x2, axis=-1, keepdims=True) + RMS_EPS) * fg_ref[...]
    o_ref[...] = x2


def _shared_residual(h2, routed, x1, mod, ws_gate, ws_up, ws_down, final_gain, final, n_lat, seq, tm=512):
    t, d = x1.shape
    n_lat_tiles = n_lat // tm
    per_batch = seq // tm
    ctx_row = mod.shape[0] - 1

    def mod_map(i):
        return (jnp.where(i < n_lat_tiles, i // per_batch, ctx_row), 0, 0)

    const = lambda i: (0, 0)
    row = lambda i: (i, 0)
    full = lambda a: pl.BlockSpec(a.shape, const)
    return pl.pallas_call(
        functools.partial(_shared_kernel, final),
        grid=(t // tm,),
        in_specs=[pl.BlockSpec((tm, d), row), pl.BlockSpec((tm * SUBLANES, LANES), row), pl.BlockSpec((tm, d), row),
                  pl.BlockSpec((1, 6, d), mod_map), full(ws_gate), full(ws_up), full(ws_down),
                  pl.BlockSpec((1, d), const)],
        out_specs=pl.BlockSpec((tm, d), row),
        out_shape=jax.ShapeDtypeStruct((t, d), F32),
        compiler_params=_cparams("parallel"),
        name="shared_residual",
    )(h2, routed, x1, mod, ws_gate, ws_up, ws_down, final_gain.reshape(1, d))


MOD_ROWS = 16


def kernel(x, c, ctx, c_ctx, w_ada, b_ada, norm1_gain, norm2_gain, w_in, na_rpb, hg_lower_bounds, hg_norm_gain,
           w_fn_branch, w_na_branch, w_hg_branch, w_out, w_router, b_router, w_exp_gate, w_exp_up, w_exp_down,
           w_sh_gate, w_sh_up, w_sh_down, final_gain):
    b, n, d = x.shape
    nc = ctx.shape[1]
    depth = w_ada.shape[0]
    n_lat, n_ctx = b * n, b * nc

    sm = jax.nn.softmax(hg_lower_bounds.astype(F32), axis=0)
    cs = jnp.cumsum(sm, axis=0)
    lower = cs - cs[0]

    cin = jnp.concatenate([c, c_ctx[None, :], jnp.zeros((MOD_ROWS - b - 1, d), F32)], axis=0)
    mod = _ada_mod(cin, w_ada, b_ada)[:, :b + 1].reshape(depth, b + 1, 6, d)
    hg_consts = _hgrn_tables()
    zero_state = jnp.zeros((b, 2, HG_WIDTH, HG_WIDTH), F32)
    wx_gate, wx_up, wx_down = w_exp_gate.astype(BF16), w_exp_up.astype(BF16), w_exp_down.astype(BF16)

    x_all = jnp.concatenate([x.reshape(n_lat, d), ctx.reshape(n_ctx, d)], axis=0)
    for l in range(depth):
        last = l == depth - 1
        w = w_in[l].astype(BF16)
        c0, c1, c2 = FN_WIDTH, FN_WIDTH + 3 * NA_WIDTH, FN_WIDTH + 3 * NA_WIDTH + 5 * HG_WIDTH
        z_fn, z_qkv, z_hg = _proj_in(x_all, mod[l], norm1_gain[l], w[:, :c0], w[:, c0:c1], w[:, c1:c2], n_lat, n)

        lbp = _hgrn_lb_params(lower[l])
        of_c, ob_c, s_ctx = _hgrn(z_hg, zero_state, lbp, hg_consts, b, nc, n_lat)
        of_l, ob_l, _ = _hgrn(z_hg, s_ctx, lbp, hg_consts, b, n, 0)
        fn_l = _fourier_lat(z_fn, b, n)
        na_l = _na_lat(z_qkv, _na_bias_table(na_rpb[l]), b, n, nc)
        if last:
            n_rows = n_lat
            fn_a, na_a, of_a, ob_a = fn_l, na_l, of_l, ob_l
        else:
            n_rows = n_lat + n_ctx
            cat = lambda p, q: jnp.concatenate([p, q], axis=0)
            fn_a = jnp.concatenate([fn_l, _fourier_ctx(z_fn, b, nc, n_lat)], axis=1)
            na_a = cat(na_l, _na_ctx(z_qkv, b, n, nc))
            of_a, ob_a = cat(of_l, of_c), cat(ob_l, ob_c)
        x1, h2, h2_tiles = _merge(x_all, mod[l], norm1_gain[l], norm2_gain[l], fn_a, na_a, of_a, ob_a, z_hg, hg_norm_gain[l],
                        w_fn_branch[l].astype(BF16), w_na_branch[l].astype(BF16), w_hg_branch[l].astype(BF16),
                        w[:, c2:], w_out[l].astype(BF16), n_rows, n_lat, n)

        idx, gate, counts = _router(h2, w_router[l], b_router[l], ROUTER_TILE)
        n_tiles = -(-n_rows // MOE_TILE)
        tile = -(-n_rows // (n_tiles * ROUTER_TILE)) * ROUTER_TILE
        tok, gate_sorted, offs = _moe_dispatch(idx[:, :TOP_K], gate[:, :TOP_K], counts[::SUBLANES], n_tiles, tile)
        routed = _moe_routed(h2_tiles, tok, gate_sorted, offs, wx_gate, wx_up, wx_down, l, n_tiles, tile)
        x_all = _shared_residual(h2, routed, x1, mod[l], w_sh_gate[l].astype(BF16),
                                 w_sh_up[l].astype(BF16), w_sh_down[l].astype(BF16), final_gain, last, n_lat, n)
    return x_all.reshape(b, n, d)
```

```python
import functools

import numpy as np
import jax
import jax.numpy as jnp
from jax import lax
from jax.experimental import pallas as pl
from jax.experimental.pallas import tpu as pltpu

D_MODEL = 1024
GRID_W = 64
RMS_EPS = 1e-6
FN_GROUPS = 4
FN_GROUP_DIM = 64
FN_WIDTH = 256
NA_HEADS = 8
NA_HEAD_DIM = 64
NA_WIDTH = 512
NA_WIN_ROWS = 8
NA_WIN_COLS = 16
HG_HEADS = 4
HG_HEAD_DIM = 64
HG_WIDTH = 256
N_EXPERTS = 256
TOP_K = 8
EXPERT_DIM = 256
ROUTE_SCALE = 2.5

LANES = 128
SUBLANES = 8

F32 = jnp.float32
BF16 = jnp.bfloat16
VMEM_LIMIT = 56 * 1024 * 1024


def _cparams(*sem):
    return pltpu.CompilerParams(dimension_semantics=sem, vmem_limit_bytes=VMEM_LIMIT)


def _dot(a, b):
    return jnp.dot(a, b, preferred_element_type=F32)


def _dot_nt(a, b):
    return lax.dot_general(a, b, (((1,), (1,)), ((), ())), preferred_element_type=F32)


def _sigmoid(z):
    e = jnp.exp(-jnp.abs(z))
    r = 1.0 / (1.0 + e)
    return jnp.where(z >= 0, r, e * r)


def _silu(z):
    return z * _sigmoid(z)


def _norm_mod(x, gain, shift, scale):
    y = x * lax.rsqrt(jnp.mean(x * x, axis=-1, keepdims=True) + RMS_EPS)
    return (y * gain) * (1.0 + scale) + shift


def _ada_kernel(c_ref, w_ref, b_ref, o_ref):
    s = _silu(c_ref[...]).astype(BF16)
    o_ref[0] = _dot(s, w_ref[0].astype(BF16)) + b_ref[0]


def _ada_mod(cin, w_ada, b_ada):
    depth, d, n = w_ada.shape
    r = cin.shape[0]
    tn = 1536
    return pl.pallas_call(
        _ada_kernel,
        grid=(depth, n // tn),
        in_specs=[
            pl.BlockSpec((r, d), lambda l, j: (0, 0)),
            pl.BlockSpec((1, d, tn), lambda l, j: (l, 0, j)),
            pl.BlockSpec((1, 1, tn), lambda l, j: (l, 0, j)),
        ],
        out_specs=pl.BlockSpec((1, r, tn), lambda l, j: (l, 0, j)),
        out_shape=jax.ShapeDtypeStruct((depth, r, n), F32),
        compiler_params=_cparams("parallel", "parallel"),
        name="ada_mod",
    )(cin, w_ada, b_ada.reshape(depth, 1, n))


def _proj_in_kernel(x_ref, mod_ref, gain_ref, wfn_ref, wqkv_ref, whg_ref, fn_ref, qkv_ref, hg_ref):
    h = _norm_mod(x_ref[...], gain_ref[...], mod_ref[0, 0:1, :], mod_ref[0, 1:2, :]).astype(BF16)
    fn_ref[...] = _dot(h, wfn_ref[...])
    qkv_ref[...] = _dot(h, wqkv_ref[...]).astype(BF16)
    hg_ref[...] = _dot(h, whg_ref[...])


def _proj_in(x_all, mod, gain, w_fn, w_qkv, w_hg, n_lat, seq, tm=512):
    t, d = x_all.shape
    n_lat_tiles = n_lat // tm
    per_batch = seq // tm
    ctx_row = mod.shape[0] - 1

    def mod_map(i):
        return (jnp.where(i < n_lat_tiles, i // per_batch, ctx_row), 0, 0)

    const = lambda i: (0, 0)
    row = lambda i: (i, 0)
    return pl.pallas_call(
        _proj_in_kernel,
        grid=(t // tm,),
        in_specs=[
            pl.BlockSpec((tm, d), row),
            pl.BlockSpec((1, 6, d), mod_map),
            pl.BlockSpec((1, d), const),
            pl.BlockSpec(w_fn.shape, const),
            pl.BlockSpec(w_qkv.shape, const),
            pl.BlockSpec(w_hg.shape, const),
        ],
        out_specs=[
            pl.BlockSpec((tm, w_fn.shape[1]), row),
            pl.BlockSpec((tm, w_qkv.shape[1]), row),
            pl.BlockSpec((tm, w_hg.shape[1]), row),
        ],
        out_shape=[
            jax.ShapeDtypeStruct((t, w_fn.shape[1]), F32),
            jax.ShapeDtypeStruct((t, w_qkv.shape[1]), BF16),
            jax.ShapeDtypeStruct((t, w_hg.shape[1]), F32),
        ],
        compiler_params=_cparams("parallel"),
        name="proj_in",
    )(x_all, mod, gain.reshape(1, d), w_fn, w_qkv, w_hg)


def _dft_cos_sin(n):
    k = np.arange(n)
    ang = 2.0 * np.pi * ((k[:, None] * k[None, :]) % n) / n
    return np.cos(ang), np.sin(ang)


def _channel_dft_mats(scale):
    c, s = _dft_cos_sin(FN_GROUP_DIM)
    eye = np.eye(FN_GROUPS)
    return (jnp.asarray(np.kron(eye, c) * scale, BF16), jnp.asarray(np.kron(eye, -s) * scale, BF16))


FFT_R = 64
FFT_GROUP = 4


def _store_halves(ref, lead, rows, val):
    ref[lead + (0, rows, slice(None))] = val[:, :LANES]
    ref[lead + (1, rows, slice(None))] = val[:, LANES:]


def _load_halves(ref, lead, rows):
    return jnp.concatenate([ref[lead + (0, rows, slice(None))], ref[lead + (1, rows, slice(None))]], axis=1)


def _fourier_lat_kernel(u_ref, cc_ref, sc_ref, ga_ref, gb_ref, tc_ref, ts_ref, o_ref, u_s, z_s):
    ub = u_ref[...].astype(BF16)
    every = slice(None)
    _store_halves(u_s, (0,), every, _dot(ub, cc_ref[...]))
    _store_halves(u_s, (1,), every, _dot(ub, sc_ref[...]))

    def stage_a(g, carry):
        n2s = [g * FFT_GROUP + i for i in range(FFT_GROUP)]
        xs = []
        for n2 in n2s:
            rows = pl.ds(n2, FFT_R, stride=FFT_R)
            xs.append(jnp.concatenate([_load_halves(u_s, (0,), rows), _load_halves(u_s, (1,), rows)],
                                      axis=0).astype(BF16))
        prods = [_dot(ga_ref[...], x) for x in xs]
        for n2, a in zip(n2s, prods):
            ar, ai = a[:FFT_R], a[FFT_R:]
            tc = jnp.concatenate([tc_ref[n2], tc_ref[n2]], axis=1)
            ts = jnp.concatenate([ts_ref[n2], ts_ref[n2]], axis=1)
            dst = pl.ds(pl.multiple_of(n2 * FFT_R, FFT_R), FFT_R)
            _store_halves(z_s, (0,), dst, ar * tc + ai * ts)
            _store_halves(z_s, (1,), dst, ai * tc - ar * ts)
        return carry

    lax.fori_loop(0, FFT_R // FFT_GROUP, stage_a, 0)

    def stage_b(g, carry):
        k1s = [g * FFT_GROUP + i for i in range(FFT_GROUP)]
        zs = []
        for k1 in k1s:
            rows = pl.ds(k1, FFT_R, stride=FFT_R)
            zs.append(jnp.concatenate([_load_halves(z_s, (0,), rows), _load_halves(z_s, (1,), rows)],
                                      axis=0).astype(BF16))
        prods = [_dot(gb_ref[...], z) for z in zs]
        for k1, p in zip(k1s, prods):
            _store_halves(o_ref, (), pl.ds(k1, FFT_R, stride=FFT_R), p)
        return carry

    lax.fori_loop(0, FFT_R // FFT_GROUP, stage_b, 0)


def _fourier_lat(z_fn, n_batch, seq):
    assert seq == FFT_R * FFT_R
    c, s = _dft_cos_sin(FFT_R)
    cc, sc = _channel_dft_mats((seq * FN_GROUP_DIM) ** -0.5)
    ga = jnp.asarray(np.block([[c, s], [-s, c]]), BF16)
    gb = jnp.asarray(np.concatenate([c, s], axis=1), BF16)
    k = np.arange(FFT_R)
    ang = 2.0 * np.pi * (k[:, None] * k[None, :]) / seq
    tc = jnp.asarray(np.broadcast_to(np.cos(ang)[:, :, None], (FFT_R, FFT_R, 128)), F32)
    ts = jnp.asarray(np.broadcast_to(np.sin(ang)[:, :, None], (FFT_R, FFT_R, 128)), F32)
    const2 = lambda b: (0, 0)
    const3 = lambda b: (0, 0, 0)
    return pl.pallas_call(
        _fourier_lat_kernel,
        grid=(n_batch,),
        in_specs=[
            pl.BlockSpec((seq, FN_WIDTH), lambda b: (b, 0)),
            pl.BlockSpec(cc.shape, const2),
            pl.BlockSpec(sc.shape, const2),
            pl.BlockSpec(ga.shape, const2),
            pl.BlockSpec(gb.shape, const2),
            pl.BlockSpec(tc.shape, const3),
            pl.BlockSpec(ts.shape, const3),
        ],
        out_specs=pl.BlockSpec((2, seq, LANES), lambda b: (0, b, 0)),
        out_shape=jax.ShapeDtypeStruct((2, n_batch * seq, LANES), F32),
        scratch_shapes=[pltpu.VMEM((2, 2, seq, LANES), F32), pltpu.VMEM((2, 2, seq, LANES), F32)],
        compiler_params=_cparams("parallel"),
        name="fourier_lat",
    )(z_fn, cc, sc, ga, gb, tc, ts)


def _fourier_ctx_kernel(u_ref, cc_ref, sc_ref, g_ref, o_ref):
    ub = u_ref[...].astype(BF16)
    x = jnp.concatenate([_dot(ub, cc_ref[...]), _dot(ub, sc_ref[...])], axis=0).astype(BF16)
    _store_halves(o_ref, (), slice(None), _dot(g_ref[...], x))


def _fourier_ctx(z_fn, n_batch, n_ctx, row0):
    blk0 = row0 // n_ctx
    c, s = _dft_cos_sin(n_ctx)
    cc, sc = _channel_dft_mats((n_ctx * FN_GROUP_DIM) ** -0.5)
    g = jnp.asarray(np.concatenate([c, s], axis=1), BF16)
    const2 = lambda b: (0, 0)
    return pl.pallas_call(
        _fourier_ctx_kernel,
        grid=(n_batch,),
        in_specs=[
            pl.BlockSpec((n_ctx, FN_WIDTH), lambda b: (blk0 + b, 0)),
            pl.BlockSpec(cc.shape, const2),
            pl.BlockSpec(sc.shape, const2),
            pl.BlockSpec(g.shape, const2),
        ],
        out_specs=pl.BlockSpec((2, n_ctx, LANES), lambda b: (0, b, 0)),
        out_shape=jax.ShapeDtypeStruct((2, n_batch * n_ctx, LANES), F32),
        compiler_params=_cparams("parallel"),
        name="fourier_ctx",
    )(z_fn, cc, sc, g)


NEG_BIG = -1e30
HEAD_PAIR = 2 * NA_HEAD_DIM


def _na_bias_table(rpb):
    wr, wc, w = NA_WIN_ROWS, NA_WIN_COLS, GRID_W
    col = np.arange(w)
    col_start = np.clip(col - wc // 2, 0, w - wc)
    in_win = (col[None, :] >= col_start[:, None]) & (col[None, :] < col_start[:, None] + wc)
    dc_idx = np.clip(col[None, :] - col[:, None], 1 - wc, wc - 1) + (wc - 1)
    h, n_dr, n_dc = rpb.shape
    onehot = jnp.asarray(dc_idx.reshape(-1)[None, :] == np.arange(n_dc)[:, None], F32)
    t = jnp.dot(rpb.astype(F32).reshape(h * n_dr, n_dc), onehot, precision=lax.Precision.HIGHEST)
    t = jnp.where(in_win.reshape(-1)[None, :], t, NEG_BIG).reshape(h, n_dr, w, w)
    per_s = [jnp.concatenate([t[:, s + j] for j in range(wr)], axis=-1) for s in range(wr)]
    return jnp.stack(per_s, axis=1)


def _softmax_pv_many(problems):
    scores = []
    for q, keys, vals, biases in problems:
        s = []
        for k, b in zip(keys, biases):
            si = _dot_nt(q, k)
            s.append(si if b is None else si + b)
        scores.append(s)
    maxes = []
    for s in scores:
        m = s[0].max(axis=-1, keepdims=True)
        for si in s[1:]:
            m = jnp.maximum(m, si.max(axis=-1, keepdims=True))
        maxes.append(m)
    probs = [[jnp.exp(si - m) for si in s] for s, m in zip(scores, maxes)]
    outs = []
    for (q, keys, vals, biases), ps in zip(problems, probs):
        acc, l = None, None
        for p, v in zip(ps, vals):
            li = p.sum(axis=-1, keepdims=True)
            oi = _dot(p.astype(BF16), v)
            acc = oi if acc is None else acc + oi
            l = li if l is None else l + li
        outs.append(acc / l)
    return outs


def _softmax_pv(q, keys, vals, biases):
    return _softmax_pv_many([(q, keys, vals, biases)])[0]


NA_ROW_GROUP = 4


def _na_lat_kernel(q_ref, k_ref, v_ref, kc_ref, vc_ref, bias_ref, o_ref):
    w = GRID_W
    n_loc = NA_WIN_ROWS * w
    rows = q_ref.shape[0] // w
    lane = lax.broadcasted_iota(jnp.int32, (w, HEAD_PAIR), 1)
    first = lane < NA_HEAD_DIM
    kc = kc_ref[...]
    vc = vc_ref[...]

    def body(g, carry):
        problems = []
        for i in range(NA_ROW_GROUP):
            r = g * NA_ROW_GROUP + i
            kr0 = jnp.clip(r - NA_WIN_ROWS // 2, 0, rows - NA_WIN_ROWS)
            s = kr0 - r + (NA_WIN_ROWS - 1)
            q = q_ref[pl.ds(pl.multiple_of(r * w, w), w), :] * jnp.asarray(NA_HEAD_DIM ** -0.5, BF16)
            ks = k_ref[pl.ds(pl.multiple_of(kr0 * w, w), n_loc), :]
            vs = v_ref[pl.ds(pl.multiple_of(kr0 * w, w), n_loc), :]
            zero = jnp.zeros_like(q)
            q2 = jnp.concatenate([jnp.where(first, q, zero), jnp.where(first, zero, q)], axis=0)
            bias = jnp.concatenate([bias_ref[0, s], bias_ref[1, s]], axis=0)
            problems.append((q2, [ks, kc], [vs, vc], [bias, None]))
        for i, o2 in enumerate(_softmax_pv_many(problems)):
            r = g * NA_ROW_GROUP + i
            o_ref[pl.ds(pl.multiple_of(r * w, w), w), :] = jnp.where(first, o2[:w], o2[w:]).astype(o_ref.dtype)
        return carry

    lax.fori_loop(0, rows // NA_ROW_GROUP, body, 0)


def _na_lat(qkv, bias, n_batch, seq, n_ctx):
    n_pairs = NA_WIDTH // HEAD_PAIR
    ctx0 = n_batch * seq // n_ctx
    wr, w = NA_WIN_ROWS, GRID_W
    return pl.pallas_call(
        _na_lat_kernel,
        grid=(n_batch, n_pairs),
        in_specs=[
            pl.BlockSpec((seq, HEAD_PAIR), lambda b, p: (b, p)),
            pl.BlockSpec((seq, HEAD_PAIR), lambda b, p: (b, n_pairs + p)),
            pl.BlockSpec((seq, HEAD_PAIR), lambda b, p: (b, 2 * n_pairs + p)),
            pl.BlockSpec((n_ctx, HEAD_PAIR), lambda b, p: (ctx0 + b, n_pairs + p)),
            pl.BlockSpec((n_ctx, HEAD_PAIR), lambda b, p: (ctx0 + b, 2 * n_pairs + p)),
            pl.BlockSpec((2, wr, w, wr * w), lambda b, p: (p, 0, 0, 0)),
        ],
        out_specs=pl.BlockSpec((seq, HEAD_PAIR), lambda b, p: (b, p)),
        out_shape=jax.ShapeDtypeStruct((n_batch * seq, NA_WIDTH), BF16),
        compiler_params=_cparams("parallel", "parallel"),
        name="na_lat",
    )(qkv, qkv, qkv, qkv, qkv, bias)


def _na_ctx_kernel(q_ref, k_ref, v_ref, o_ref):
    lane = lax.broadcasted_iota(jnp.int32, q_ref.shape, 1)
    first = lane < NA_HEAD_DIM
    q = q_ref[...] * jnp.asarray(NA_HEAD_DIM ** -0.5, BF16)
    k = k_ref[...]
    v = v_ref[...]
    zero = jnp.zeros_like(q)
    o0 = _softmax_pv(jnp.where(first, q, zero), [k], [v], [None])
    o1 = _softmax_pv(jnp.where(first, zero, q), [k], [v], [None])
    o_ref[...] = jnp.where(first, o0, o1).astype(o_ref.dtype)


def _na_ctx(qkv, n_batch, seq, n_ctx):
    n_pairs = NA_WIDTH // HEAD_PAIR
    ctx0 = n_batch * seq // n_ctx
    return pl.pallas_call(
        _na_ctx_kernel,
        grid=(n_batch, n_pairs),
        in_specs=[
            pl.BlockSpec((n_ctx, HEAD_PAIR), lambda b, p: (ctx0 + b, p)),
            pl.BlockSpec((n_ctx, HEAD_PAIR), lambda b, p: (ctx0 + b, n_pairs + p)),
            pl.BlockSpec((n_ctx, HEAD_PAIR), lambda b, p: (ctx0 + b, 2 * n_pairs + p)),
        ],
        out_specs=pl.BlockSpec((n_ctx, HEAD_PAIR), lambda b, p: (b, p)),
        out_shape=jax.ShapeDtypeStruct((n_batch * n_ctx, NA_WIDTH), BF16),
        compiler_params=_cparams("parallel", "parallel"),
        name="na_ctx",
    )(qkv, qkv, qkv)


HG_CHUNK = 64
HG_LEVELS = (32, 16)
HG_DIAG = 16
HG_BATCH = 4


def _hgrn_tables():
    c = HG_CHUNK
    tri, masks, refs, last = [], [], [], []
    for reverse in (False, True):
        u = np.arange(c)[::-1] if reverse else np.arange(c)
        row_of = {int(uu): t for t, uu in enumerate(u)}
        ut, uj = u[:, None], u[None, :]
        tri.append(uj <= ut)
        lv, rf = [], []
        for h in HG_LEVELS:
            same = (ut // (2 * h)) == (uj // (2 * h))
            up_t, up_j = (ut % (2 * h)) >= h, (uj % (2 * h)) >= h
            lv.append(same & up_t & ~up_j)
            rf.append([row_of[int(uu) // (2 * h) * (2 * h) + h - 1] for uu in u])
        same_d = (ut // HG_DIAG) == (uj // HG_DIAG)
        lv.append(same_d & (uj <= ut))
        rf.append([row_of.get(int(uu) // HG_DIAG * HG_DIAG - 1, -1) for uu in u])
        masks.append(np.stack([np.tile(x, (1, HG_HEADS)) for x in lv]).astype(np.float32))
        refs.append(rf)
        last.append(row_of[c - 1])
    hm = (np.arange(HG_WIDTH)[:, None] // HG_HEAD_DIM) == (np.arange(HG_WIDTH)[None, :] // HG_HEAD_DIM)
    arrays = (jnp.asarray(np.stack(tri), BF16), jnp.asarray(np.stack(masks), F32), jnp.asarray(hm, F32))
    return arrays, refs, last


def _ref_rows(b, ref):
    pieces, t = [], 0
    while t < len(ref):
        t1 = t
        while t1 < len(ref) and ref[t1] == ref[t]:
            t1 += 1
        shape = (t1 - t, b.shape[1])
        pieces.append(jnp.zeros(shape, b.dtype) if ref[t] < 0 else jnp.broadcast_to(b[ref[t]:ref[t] + 1, :], shape))
        t = t1
    return jnp.concatenate(pieces, axis=0)


def _split3(x):
    hi = x.astype(BF16)
    r = x - hi.astype(F32)
    mid = r.astype(BF16)
    lo = (r - mid.astype(F32)).astype(BF16)
    return hi, mid, lo


def _hgrn_chunks(chains, hm, refs, last):
    hm_b = hm.astype(BF16)

    def heads_bd(x):
        return jnp.concatenate([x.astype(BF16)] * HG_HEADS, axis=0) * hm_b

    qs, ks, splits = [], [], []
    for zq, zv, zf, lbp, tri, masks, st, d in chains:
        e = jnp.exp(-jnp.abs(zf))
        inv = 1.0 / (1.0 + e)
        log_sig = jnp.minimum(zf, 0.0) - jnp.log(1.0 + e)
        sig_neg = jnp.where(zf >= 0, e * inv, inv)
        a = lbp[0:1, :]
        cc = lbp[1:2, :] + log_sig
        log_f = jnp.maximum(a, cc) + jnp.log(1.0 + jnp.exp(-jnp.abs(a - cc)))
        qs.append(_silu(zq))
        ks.append(lbp[2:3, :] * sig_neg)
        splits.append(_split3(log_f))
    bs = [_dot(c[4], hi) + _dot(c[4], mid) + _dot(c[4], lo) for c, (hi, mid, lo) in zip(chains, splits)]
    b_lasts = [b[last[c[7]]:last[c[7]] + 1] for c, b in zip(chains, bs)]
    a_alls = [None] * len(chains)
    for i in range(len(HG_LEVELS) + 1):
        pairs = []
        for c, q, k, b in zip(chains, qs, ks, bs):
            rel = b - _ref_rows(b, refs[c[7]][i])
            if i < len(HG_LEVELS):
                pairs.append((q * jnp.exp(jnp.minimum(rel, 0.0)), k * jnp.exp(jnp.minimum(-rel, 0.0))))
            else:
                pairs.append((q * jnp.exp(rel), k * jnp.exp(-rel)))
        for n, (c, (qx, kx)) in enumerate(zip(chains, pairs)):
            ai = _dot_nt(qx.astype(BF16), heads_bd(kx)) * c[5][i]
            a_alls[n] = ai if a_alls[n] is None else a_alls[n] + ai
    outs = [_dot(a_all.astype(BF16), heads_bd(c[1])) + _dot_nt((q * jnp.exp(b)).astype(BF16), c[6].astype(BF16))
            for c, q, b, a_all in zip(chains, qs, bs, a_alls)]
    states = []
    for c, k, b, b_last in zip(chains, ks, bs, b_lasts):
        kl = (k * jnp.exp(b_last - b)).astype(BF16)
        upd = lax.dot_general(c[1].astype(BF16), kl, (((0,), (0,)), ((), ())), preferred_element_type=F32)
        states.append(c[6] * jnp.exp(b_last) + upd * hm)
    return outs, states


def _hgrn_kernel(refs, last, *args):
    nb = HG_BATCH
    z_refs = args[:3 * nb]
    s0_ref, lbp_ref, tri_ref, mask_ref, hm_ref = args[3 * nb:3 * nb + 5]
    of_ref, ob_ref, sfin_ref, st = args[3 * nb + 5:]
    i = pl.program_id(1)

    @pl.when(i == 0)
    def _():
        st[...] = s0_ref[...]

    w = HG_WIDTH
    hm = hm_ref[...]
    chains = []
    for bb in range(nb):
        zf_ref, zb_ref, zbg_ref = z_refs[3 * bb:3 * bb + 3]
        chains.append((zf_ref[:, 0:w], zf_ref[:, w:2 * w], zf_ref[:, 2 * w:3 * w], lbp_ref[0], tri_ref[0],
                       mask_ref[0], st[bb, 0], 0))
        chains.append((zb_ref[:, 0:w], zb_ref[:, w:2 * w], zbg_ref[...], lbp_ref[1], tri_ref[1],
                       mask_ref[1], st[bb, 1], 1))
    outs, states = _hgrn_chunks(chains, hm, refs, last)
    for bb in range(nb):
        of_ref[bb] = outs[2 * bb]
        ob_ref[bb] = outs[2 * bb + 1]
        st[bb, 0] = states[2 * bb]
        st[bb, 1] = states[2 * bb + 1]

    @pl.when(i == pl.num_programs(1) - 1)
    def _():
        sfin_ref[...] = st[...]


def _hgrn(z_hg, s0, lbp, tables, n_batch, n_tok, row0):
    (tri, masks, hm), refs, last = tables
    c, w, nb = HG_CHUNK, HG_WIDTH, HG_BATCH
    assert n_batch % nb == 0 and n_tok % c == 0
    nch = n_tok // c
    base = row0 // c
    const3 = lambda b, i: (0, 0, 0)
    in_specs = []
    for bb in range(nb):
        fwd = lambda b, i, bb=bb: (base + (b * nb + bb) * nch + i, 0)
        bwd = lambda b, i, bb=bb: (base + (b * nb + bb) * nch + (nch - 1 - i), 0)
        bwd_gate = lambda b, i, bb=bb: (base + (b * nb + bb) * nch + (nch - 1 - i), 3)
        in_specs += [pl.BlockSpec((c, 3 * w), fwd), pl.BlockSpec((c, 2 * w), bwd), pl.BlockSpec((c, w), bwd_gate)]
    in_specs += [
        pl.BlockSpec((nb, 2, w, w), lambda b, i: (b, 0, 0, 0)),
        pl.BlockSpec(lbp.shape, const3),
        pl.BlockSpec(tri.shape, const3),
        pl.BlockSpec(masks.shape, lambda b, i: (0, 0, 0, 0)),
        pl.BlockSpec(hm.shape, lambda b, i: (0, 0)),
    ]
    o_f, o_b, s_fin = pl.pallas_call(
        functools.partial(_hgrn_kernel, refs, last),
        grid=(n_batch // nb, nch),
        in_specs=in_specs,
        out_specs=[
            pl.BlockSpec((nb, c, w), lambda b, i: (b, i, 0)),
            pl.BlockSpec((nb, c, w), lambda b, i: (b, nch - 1 - i, 0)),
            pl.BlockSpec((nb, 2, w, w), lambda b, i: (b, 0, 0, 0)),
        ],
        out_shape=[
            jax.ShapeDtypeStruct((n_batch, n_tok, w), F32),
            jax.ShapeDtypeStruct((n_batch, n_tok, w), F32),
            jax.ShapeDtypeStruct((n_batch, 2, w, w), F32),
        ],
        scratch_shapes=[pltpu.VMEM((nb, 2, w, w), F32)],
        compiler_params=_cparams("parallel", "arbitrary"),
        name="hgrn",
    )(*([z_hg] * (3 * nb)), s0, lbp, tri, masks, hm)
    return o_f.reshape(n_batch * n_tok, w), o_b.reshape(n_batch * n_tok, w), s_fin


def _hgrn_lb_params(lower_l):
    rows = jnp.stack([jnp.log(lower_l), jnp.log1p(-lower_l), 1.0 - lower_l], axis=1)
    return jnp.pad(rows, ((0, 0), (0, 5), (0, 0)))


def _merge_kernel(x_ref, mod_ref, g1_ref, g2_ref, fn_ref, na_ref, of_ref, ob_ref, hgz_ref, hgain_ref,
                  hmean_ref, wfn_ref, wna_ref, whg_ref, wbg_ref, wout_ref, x1_ref, h2_ref, h2t_ref):
    d = x_ref.shape[1]
    x = x_ref[...]
    h = _norm_mod(x, g1_ref[...], mod_ref[0, 0:1, :], mod_ref[0, 1:2, :]).astype(BF16)
    o = of_ref[...] + ob_ref[...]
    hi, mid, lo = _split3(o * o)
    ms = _dot(hi, hmean_ref[...]) + _dot(mid, hmean_ref[...]) + _dot(lo, hmean_ref[...])
    y_hg_in = o * lax.rsqrt(ms + RMS_EPS) * hgain_ref[...] * _silu(hgz_ref[...])
    y_fn = _dot(jnp.concatenate([fn_ref[0], fn_ref[1]], axis=1).astype(BF16), wfn_ref[...])
    y_na = _dot(na_ref[...], wna_ref[...])
    y_hg = _dot(y_hg_in.astype(BF16), whg_ref[...])
    m = (_sigmoid(_dot(h, wbg_ref[:, 0:d])) * y_fn
         + _sigmoid(_dot(h, wbg_ref[:, d:2 * d])) * y_na
         + _sigmoid(_dot(h, wbg_ref[:, 2 * d:3 * d])) * y_hg)
    x1 = x + mod_ref[0, 2:3, :] * _dot(m.astype(BF16), wout_ref[...])
    x1_ref[...] = x1
    h2 = _norm_mod(x1, g2_ref[...], mod_ref[0, 3:4, :], mod_ref[0, 4:5, :])
    h2_ref[...] = h2
    tm = x.shape[0]
    for s in range(SUBLANES):
        h2t_ref[pl.ds(s, tm, stride=SUBLANES), :] = h2[:, s * LANES:(s + 1) * LANES]


def _merge(x_all, mod, gain1, gain2, fn, na, o_f, o_b, z_hg, hgain, w_fnb, w_nab, w_hgb, w_bg, w_out,
           n_rows, n_lat, seq, tm=512):
    t, d = n_rows, x_all.shape[1]
    n_lat_tiles = n_lat // tm
    per_batch = seq // tm
    ctx_row = mod.shape[0] - 1
    w = HG_WIDTH
    hmean = jnp.asarray(np.kron(np.eye(HG_HEADS), np.full((HG_HEAD_DIM, HG_HEAD_DIM), 1.0 / HG_HEAD_DIM)), BF16)

    def mod_map(i):
        return (jnp.where(i < n_lat_tiles, i // per_batch, ctx_row), 0, 0)

    const = lambda i: (0, 0)
    row = lambda i: (i, 0)
    full = lambda a: pl.BlockSpec(a.shape, const)
    return pl.pallas_call(
        _merge_kernel,
        grid=(t // tm,),
        in_specs=[
            pl.BlockSpec((tm, d), row),
            pl.BlockSpec((1, 6, d), mod_map),
            pl.BlockSpec((1, d), const),
            pl.BlockSpec((1, d), const),
            pl.BlockSpec((2, tm, LANES), lambda i: (0, i, 0)),
            pl.BlockSpec((tm, NA_WIDTH), row),
            pl.BlockSpec((tm, w), row),
            pl.BlockSpec((tm, w), row),
            pl.BlockSpec((tm, w), lambda i: (i, 4)),
            pl.BlockSpec((1, w), const),
            full(hmean), full(w_fnb), full(w_nab), full(w_hgb), full(w_bg), full(w_out),
        ],
        out_specs=[pl.BlockSpec((tm, d), row), pl.BlockSpec((tm, d), row), pl.BlockSpec((tm * SUBLANES, LANES), row)],
        out_shape=[jax.ShapeDtypeStruct((t, d), F32), jax.ShapeDtypeStruct((t, d), F32),
                   jax.ShapeDtypeStruct((t * SUBLANES, LANES), F32)],
        compiler_params=_cparams("parallel"),
        name="merge",
    )(x_all, mod, gain1.reshape(1, d), gain2.reshape(1, d), fn, na, o_f, o_b, z_hg,
      jnp.tile(hgain, HG_HEADS).reshape(1, w), hmean, w_fnb, w_nab, w_hgb, w_bg, w_out)


def _router_kernel(h_ref, whi_ref, wlo_ref, b_ref, idx_ref, gate_ref, cnt_ref):
    h = h_ref[...]
    hi = h.astype(BF16)
    lo = (h - hi.astype(F32)).astype(BF16)
    logits = _dot(hi, whi_ref[...]) + (_dot(hi, wlo_ref[...]) + _dot(lo, whi_ref[...]))
    scores = _sigmoid(logits)
    sel = scores + b_ref[...]
    tm, ne = sel.shape
    col = lax.broadcasted_iota(jnp.int32, (tm, ne), 1)
    out_lane = lax.broadcasted_iota(jnp.int32, (tm, LANES), 1)
    idx_out = jnp.zeros((tm, LANES), jnp.int32)
    gate_out = jnp.zeros((tm, LANES), F32)
    total = jnp.zeros((tm, 1), F32)
    chosen = jnp.zeros((tm, ne), F32)
    for k in range(TOP_K):
        m = sel.max(axis=-1, keepdims=True)
        idx = jnp.where(sel == m, col, ne).min(axis=-1, keepdims=True)
        hit = col == idx
        g = jnp.where(hit, scores, 0.0).sum(axis=-1, keepdims=True)
        sel = jnp.where(hit, -jnp.inf, sel)
        chosen = jnp.where(hit, 1.0, chosen)
        idx_out = jnp.where(out_lane == k, idx, idx_out)
        gate_out = jnp.where(out_lane == k, g, gate_out)
        total = total + g
    idx_ref[...] = idx_out
    gate_ref[...] = gate_out * (ROUTE_SCALE / total)
    for part in range(tm // COUNT_ROWS):
        c = chosen[part * COUNT_ROWS:(part + 1) * COUNT_ROWS].sum(axis=0, keepdims=True)
        cnt_ref[part * SUBLANES:(part + 1) * SUBLANES, :] = jnp.broadcast_to(c, (SUBLANES, ne)).astype(jnp.int32)


def _router(h2, w_router, b_router, tm=512):
    t, d = h2.shape
    ne = w_router.shape[1]
    w_hi = w_router.astype(BF16)
    w_lo = (w_router - w_hi.astype(F32)).astype(BF16)
    const = lambda i: (0, 0)
    row = lambda i: (i, 0)
    return pl.pallas_call(
        _router_kernel,
        grid=(t // tm,),
        in_specs=[pl.BlockSpec((tm, d), row), pl.BlockSpec((d, ne), const), pl.BlockSpec((d, ne), const),
                  pl.BlockSpec((1, ne), const)],
        out_specs=[pl.BlockSpec((tm, LANES), row), pl.BlockSpec((tm, LANES), row),
                   pl.BlockSpec((tm // COUNT_ROWS * SUBLANES, ne), row)],
        out_shape=[jax.ShapeDtypeStruct((t, LANES), jnp.int32), jax.ShapeDtypeStruct((t, LANES), F32),
                   jax.ShapeDtypeStruct((t // COUNT_ROWS * SUBLANES, ne), jnp.int32)],
        compiler_params=_cparams("parallel"),
        name="router",
    )(h2, w_hi, w_lo, b_router.reshape(1, ne).astype(F32))


MOE_TILE = 4352
COUNT_ROWS = 256
MOE_ROWS = 160
MOE_GROUP = 4
MOE_SCATTER_GROUP = 4


def _moe_dispatch(idx, gate, counts, n_tiles, tile):
    t, k = idx.shape
    pad = n_tiles * tile - t
    slots = tile * k
    e = jnp.pad(idx, ((0, pad), (0, 0)), constant_values=N_EXPERTS).reshape(n_tiles, slots)
    g = jnp.pad(gate, ((0, pad), (0, 0))).reshape(n_tiles, slots)
    key = e * slots + jnp.arange(slots, dtype=jnp.int32)[None, :]
    key_sorted, g_sorted = lax.sort((key, g), dimension=1, num_keys=1)
    row8_sorted = (key_sorted % slots) // k * SUBLANES
    group = tile // (t // counts.shape[0])
    c = jnp.pad(counts, ((0, n_tiles * group - counts.shape[0]), (0, 0))).reshape(n_tiles, group, -1).sum(axis=1)
    offs = jnp.concatenate([jnp.zeros((n_tiles, 1), jnp.int32), jnp.cumsum(c, axis=1)], axis=1)
    offs = jnp.pad(offs.astype(jnp.int32).reshape(-1), (0, SUBLANES))
    return row8_sorted.reshape(-1), g_sorted.reshape(-1), offs


def _moe_kernel(offs_ref, x_ref, row_ref, gate_ref, wg_ref, wu_ref, wd_ref, o_ref, g_a, g_b, y_a, y_b):
    i = pl.program_id(0)
    j = pl.program_id(1)
    nj = pl.num_programs(1)
    r, sl = MOE_ROWS, SUBLANES
    n_slots = row_ref.shape[0]
    all_rows = list(range(r))

    def segment(ee):
        start = offs_ref[i * (N_EXPERTS + 1) + ee]
        n = offs_ref[i * (N_EXPERTS + 1) + ee + 1] - start
        n_fast = jnp.where(start <= n_slots - r, jnp.minimum(n, r), 0)
        return start, jnp.minimum(start, n_slots - r), n_fast, n

    def gather(buf, base):
        for row in all_rows:
            buf[pl.ds(row * sl, sl), :] = x_ref[pl.ds(pl.multiple_of(row_ref[base + row], sl), sl), :]

    def experts(k, buf_in, buf_out, n_valid):
        x = jnp.concatenate([buf_in[pl.ds(s, r, stride=sl), :] for s in range(sl)], axis=1).astype(BF16)
        hid = (_silu(_dot(x, wg_ref[k])) * _dot(x, wu_ref[k])).astype(BF16)
        y = _dot(hid, wd_ref[k])
        y = jnp.where(lax.broadcasted_iota(jnp.int32, (r, 1), 0) < n_valid, y, 0.0)
        for s in range(sl):
            buf_out[pl.ds(s, r, stride=sl), :] = y[:, s * LANES:(s + 1) * LANES]

    def scatter(buf, base):
        for g0 in range(0, r, MOE_SCATTER_GROUP):
            dsts, news = [], []
            for row in all_rows[g0:g0 + MOE_SCATTER_GROUP]:
                dst = pl.ds(pl.multiple_of(row_ref[base + row], sl), sl)
                dsts.append(dst)
                news.append(o_ref[dst, :] + gate_ref[base + row] * buf[pl.ds(row * sl, sl), :])
            for dst, new in reversed(list(zip(dsts, news))):
                o_ref[dst, :] = new

    def slow_path(k, start, n_fast, n, g_free, y_free):
        def chunk(c, carry):
            base = start + n_fast + c * r
            n_valid = jnp.minimum(n - n_fast - c * r, r)

            def slot(row):
                return jnp.minimum(base + row, n_slots - 1)

            def gather_group(gi, carry):
                for row in [gi * sl + q for q in range(sl)]:
                    src = pl.ds(pl.multiple_of(row_ref[slot(row)], sl), sl)
                    g_free[pl.ds(pl.multiple_of(row * sl, sl), sl), :] = x_ref[src, :]
                return carry

            def scatter_group(gi, carry):
                dsts, news = [], []
                for row in [gi * sl + q for q in range(sl)]:
                    dst = pl.ds(pl.multiple_of(row_ref[slot(row)], sl), sl)
                    dsts.append(dst)
                    news.append(o_ref[dst, :] + gate_ref[slot(row)] * y_free[pl.ds(pl.multiple_of(row * sl, sl), sl), :])
                for dst, new in reversed(list(zip(dsts, news))):
                    o_ref[dst, :] = new
                return carry

            lax.fori_loop(0, r // sl, gather_group, 0)
            experts(k, g_free, y_free, n_valid)
            lax.fori_loop(0, r // sl, scatter_group, 0)
            return carry

        lax.fori_loop(0, (n - n_fast + r - 1) // r, chunk, 0)

    grp = MOE_GROUP
    e0 = grp * j
    segs = [segment(e0 + k) for k in range(grp)]
    _, base_next, _, _ = segment(jnp.minimum(e0 + grp, N_EXPERTS - 1))
    _, base_prev, _, _ = segment(jnp.maximum(e0 - 1, 0))
    bases = [base_prev] + [s[1] for s in segs] + [base_next]
    bufs = [(g_a, y_a), (g_b, y_b)]

    @pl.when(j == 0)
    def _():
        o_ref[...] = jnp.zeros_like(o_ref)
        y_b[...] = jnp.zeros_like(y_b)
        gather(g_a, bases[1])

    for k in range(grp):
        g_cur, y_cur = bufs[k % 2]
        g_other, y_other = bufs[(k + 1) % 2]
        experts(k, g_cur, y_cur, segs[k][2])
        gather(g_other, bases[k + 2])
        scatter(y_other, bases[k])

    @pl.when(j == nj - 1)
    def _():
        scatter(y_b, bases[grp])

    for k in range(grp):
        start, _, n_fast, n = segs[k]

        @pl.when(n > n_fast)
        def _(k=k, start=start, n_fast=n_fast, n=n):
            slow_path(k, start, n_fast, n, g_b, y_a)


def _moe_routed(x_tiles, tok, gate, offs, w_gate, w_up, w_down, layer, n_tiles, tile):
    _, ne, d, f = w_gate.shape
    grp = MOE_GROUP
    assert grp % 2 == 0 and ne % grp == 0
    rows = tile * SUBLANES
    slots = tile * TOP_K
    grid_spec = pltpu.PrefetchScalarGridSpec(
        num_scalar_prefetch=1,
        grid=(n_tiles, ne // grp),
        in_specs=[
            pl.BlockSpec((rows, LANES), lambda i, j, offs: (i, 0), pipeline_mode=pl.Buffered(1)),
            pl.BlockSpec((slots,), lambda i, j, offs: (i,), memory_space=pltpu.SMEM, pipeline_mode=pl.Buffered(1)),
            pl.BlockSpec((slots,), lambda i, j, offs: (i,), memory_space=pltpu.SMEM, pipeline_mode=pl.Buffered(1)),
            pl.BlockSpec((None, grp, d, f), lambda i, j, offs: (layer, j, 0, 0)),
            pl.BlockSpec((None, grp, d, f), lambda i, j, offs: (layer, j, 0, 0)),
            pl.BlockSpec((None, grp, f, d), lambda i, j, offs: (layer, j, 0, 0)),
        ],
        out_specs=pl.BlockSpec((rows, LANES), lambda i, j, offs: (i, 0), pipeline_mode=pl.Buffered(1)),
        scratch_shapes=[pltpu.VMEM((MOE_ROWS * SUBLANES, LANES), F32)] * 4,
    )
    return pl.pallas_call(
        _moe_kernel,
        grid_spec=grid_spec,
        out_shape=jax.ShapeDtypeStruct(x_tiles.shape, F32),
        compiler_params=_cparams("arbitrary", "arbitrary"),
        name="moe_routed",
    )(offs, x_tiles, tok, gate, w_gate, w_up, w_down)


def _shared_kernel(final, h_ref, routed_ref, x1_ref, mod_ref, wg_ref, wu_ref, wd_ref, fg_ref, o_ref):
    h = h_ref[...].astype(BF16)
    hid = (_silu(_dot(h, wg_ref[...])) * _dot(h, wu_ref[...])).astype(BF16)
    tm = h.shape[0]
    routed = jnp.concatenate([routed_ref[pl.ds(s, tm, stride=SUBLANES), :] for s in range(SUBLANES)], axis=1)
    x2 = x1_ref[...] + mod_ref[0, 5:6, :] * (routed + _dot(hid, wd_ref[...]))
    if final:
        x2 = x2 * lax.rsqrt(jnp.mean(x2 * x2, axis=-1, keepdims=True) + RMS_EPS) * fg_ref[...]
    o_ref[...] = x2


def _shared_residual(h2, routed, x1, mod, ws_gate, ws_up, ws_down, final_gain, final, n_lat, seq, tm=512):
    t, d = x1.shape
    n_lat_tiles = n_lat // tm
    per_batch = seq // tm
    ctx_row = mod.shape[0] - 1

    def mod_map(i):
        return (jnp.where(i < n_lat_tiles, i // per_batch, ctx_row), 0, 0)

    const = lambda i: (0, 0)
    row = lambda i: (i, 0)
    full = lambda a: pl.BlockSpec(a.shape, const)
    return pl.pallas_call(
        functools.partial(_shared_kernel, final),
        grid=(t // tm,),
        in_specs=[pl.BlockSpec((tm, d), row), pl.BlockSpec((tm * SUBLANES, LANES), row), pl.BlockSpec((tm, d), row),
                  pl.BlockSpec((1, 6, d), mod_map), full(ws_gate), full(ws_up), full(ws_down),
                  pl.BlockSpec((1, d), const)],
        out_specs=pl.BlockSpec((tm, d), row),
        out_shape=jax.ShapeDtypeStruct((t, d), F32),
        compiler_params=_cparams("parallel"),
        name="shared_residual",
    )(h2, routed, x1, mod, ws_gate, ws_up, ws_down, final_gain.reshape(1, d))


MOD_ROWS = 16


def kernel(x, c, ctx, c_ctx, w_ada, b_ada, norm1_gain, norm2_gain, w_in, na_rpb, hg_lower_bounds, hg_norm_gain,
           w_fn_branch, w_na_branch, w_hg_branch, w_out, w_router, b_router, w_exp_gate, w_exp_up, w_exp_down,
           w_sh_gate, w_sh_up, w_sh_down, final_gain):
    b, n, d = x.shape
    nc = ctx.shape[1]
    depth = w_ada.shape[0]
    n_lat, n_ctx = b * n, b * nc

    sm = jax.nn.softmax(hg_lower_bounds.astype(F32), axis=0)
    cs = jnp.cumsum(sm, axis=0)
    lower = cs - cs[0]

    cin = jnp.concatenate([c, c_ctx[None, :], jnp.zeros((MOD_ROWS - b - 1, d), F32)], axis=0)
    mod = _ada_mod(cin, w_ada, b_ada)[:, :b + 1].reshape(depth, b + 1, 6, d)
    hg_consts = _hgrn_tables()
    zero_state = jnp.zeros((b, 2, HG_WIDTH, HG_WIDTH), F32)
    wx_gate, wx_up, wx_down = w_exp_gate.astype(BF16), w_exp_up.astype(BF16), w_exp_down.astype(BF16)

    x_all = jnp.concatenate([x.reshape(n_lat, d), ctx.reshape(n_ctx, d)], axis=0)
    for l in range(depth):
        last = l == depth - 1
        w = w_in[l].astype(BF16)
        c0, c1, c2 = FN_WIDTH, FN_WIDTH + 3 * NA_WIDTH, FN_WIDTH + 3 * NA_WIDTH + 5 * HG_WIDTH
        z_fn, z_qkv, z_hg = _proj_in(x_all, mod[l], norm1_gain[l], w[:, :c0], w[:, c0:c1], w[:, c1:c2], n_lat, n)

        lbp = _hgrn_lb_params(lower[l])
        of_c, ob_c, s_ctx = _hgrn(z_hg, zero_state, lbp, hg_consts, b, nc, n_lat)
        of_l, ob_l, _ = _hgrn(z_hg, s_ctx, lbp, hg_consts, b, n, 0)
        fn_l = _fourier_lat(z_fn, b, n)
        na_l = _na_lat(z_qkv, _na_bias_table(na_rpb[l]), b, n, nc)
        if last:
            n_rows = n_lat
            fn_a, na_a, of_a, ob_a = fn_l, na_l, of_l, ob_l
        else:
            n_rows = n_lat + n_ctx
            cat = lambda p, q: jnp.concatenate([p, q], axis=0)
            fn_a = jnp.concatenate([fn_l, _fourier_ctx(z_fn, b, nc, n_lat)], axis=1)
            na_a = cat(na_l, _na_ctx(z_qkv, b, n, nc))
            of_a, ob_a = cat(of_l, of_c), cat(ob_l, ob_c)
        x1, h2, h2_tiles = _merge(x_all, mod[l], norm1_gain[l], norm2_gain[l], fn_a, na_a, of_a, ob_a, z_hg, hg_norm_gain[l],
                        w_fn_branch[l].astype(BF16), w_na_branch[l].astype(BF16), w_hg_branch[l].astype(BF16),
                        w[:, c2:], w_out[l].astype(BF16), n_rows, n_lat, n)

        idx, gate, counts = _router(h2, w_router[l], b_router[l])
        n_tiles = -(-n_rows // MOE_TILE)
        tile = -(-n_rows // (n_tiles * COUNT_ROWS)) * COUNT_ROWS
        tok, gate_sorted, offs = _moe_dispatch(idx[:, :TOP_K], gate[:, :TOP_K], counts[::SUBLANES], n_tiles, tile)
        routed = _moe_routed(h2_tiles, tok, gate_sorted, offs, wx_gate, wx_up, wx_down, l, n_tiles, tile)
        x_all = _shared_residual(h2, routed, x1, mod[l], w_sh_gate[l].astype(BF16),
                                 w_sh_up[l].astype(BF16), w_sh_down[l].astype(BF16), final_gain, last, n_lat, n)
    return x_all.reshape(b, n, d)
```

```python
import functools

import numpy as np
import jax
import jax.numpy as jnp
from jax import lax
from jax.experimental import pallas as pl
from jax.experimental.pallas import tpu as pltpu

GRID_W = 64
RMS_EPS = 1e-6
FN_GROUPS = 4
FN_GROUP_DIM = 64
FN_WIDTH = 256
NA_HEAD_DIM = 64
NA_WIDTH = 512
NA_WIN_ROWS = 8
NA_WIN_COLS = 16
HG_HEADS = 4
HG_HEAD_DIM = 64
HG_WIDTH = 256
N_EXPERTS = 256
TOP_K = 8
ROUTE_SCALE = 2.5

LANES = 128
SUBLANES = 8

F32 = jnp.float32
BF16 = jnp.bfloat16
VMEM_LIMIT = 56 * 1024 * 1024


def _cparams(*sem):
    return pltpu.CompilerParams(dimension_semantics=sem, vmem_limit_bytes=VMEM_LIMIT)


def _dot(a, b):
    return jnp.dot(a, b, preferred_element_type=F32)


def _dot_nt(a, b):
    return lax.dot_general(a, b, (((1,), (1,)), ((), ())), preferred_element_type=F32)


def _sigmoid(z):
    e = jnp.exp(-jnp.abs(z))
    r = 1.0 / (1.0 + e)
    return jnp.where(z >= 0, r, e * r)


def _silu(z):
    return z * _sigmoid(z)


def _norm_mod(x, gain, shift, scale):
    y = x * lax.rsqrt(jnp.mean(x * x, axis=-1, keepdims=True) + RMS_EPS)
    return (y * gain) * (1.0 + scale) + shift


def _ada_kernel(c_ref, w_ref, b_ref, o_ref):
    s = _silu(c_ref[...]).astype(BF16)
    o_ref[0] = _dot(s, w_ref[0].astype(BF16)) + b_ref[0]


def _ada_mod(cin, w_ada, b_ada):
    depth, d, n = w_ada.shape
    r = cin.shape[0]
    tn = 1536
    return pl.pallas_call(
        _ada_kernel,
        grid=(depth, n // tn),
        in_specs=[
            pl.BlockSpec((r, d), lambda l, j: (0, 0)),
            pl.BlockSpec((1, d, tn), lambda l, j: (l, 0, j)),
            pl.BlockSpec((1, 1, tn), lambda l, j: (l, 0, j)),
        ],
        out_specs=pl.BlockSpec((1, r, tn), lambda l, j: (l, 0, j)),
        out_shape=jax.ShapeDtypeStruct((depth, r, n), F32),
        compiler_params=_cparams("parallel", "parallel"),
        name="ada_mod",
    )(cin, w_ada, b_ada.reshape(depth, 1, n))


def _proj_in_kernel(x_ref, mod_ref, gain_ref, wfn_ref, wqkv_ref, whg_ref, fn_ref, qkv_ref, hg_ref):
    h = _norm_mod(x_ref[...], gain_ref[...], mod_ref[0, 0:1, :], mod_ref[0, 1:2, :]).astype(BF16)
    fn_ref[...] = _dot(h, wfn_ref[...])
    qkv_ref[...] = _dot(h, wqkv_ref[...]).astype(BF16)
    hg_ref[...] = _dot(h, whg_ref[...])


def _proj_in(x_all, mod, gain, w_fn, w_qkv, w_hg, n_lat, seq, tm=512):
    t, d = x_all.shape
    n_lat_tiles = n_lat // tm
    per_batch = seq // tm
    ctx_row = mod.shape[0] - 1

    def mod_map(i):
        return (jnp.where(i < n_lat_tiles, i // per_batch, ctx_row), 0, 0)

    const = lambda i: (0, 0)
    row = lambda i: (i, 0)
    return pl.pallas_call(
        _proj_in_kernel,
        grid=(t // tm,),
        in_specs=[
            pl.BlockSpec((tm, d), row),
            pl.BlockSpec((1, 6, d), mod_map),
            pl.BlockSpec((1, d), const),
            pl.BlockSpec(w_fn.shape, const),
            pl.BlockSpec(w_qkv.shape, const),
            pl.BlockSpec(w_hg.shape, const),
        ],
        out_specs=[
            pl.BlockSpec((tm, w_fn.shape[1]), row),
            pl.BlockSpec((tm, w_qkv.shape[1]), row),
            pl.BlockSpec((tm, w_hg.shape[1]), row),
        ],
        out_shape=[
            jax.ShapeDtypeStruct((t, w_fn.shape[1]), F32),
            jax.ShapeDtypeStruct((t, w_qkv.shape[1]), BF16),
            jax.ShapeDtypeStruct((t, w_hg.shape[1]), F32),
        ],
        compiler_params=_cparams("parallel"),
        name="proj_in",
    )(x_all, mod, gain.reshape(1, d), w_fn, w_qkv, w_hg)


def _dft_cos_sin(n):
    k = np.arange(n)
    ang = 2.0 * np.pi * ((k[:, None] * k[None, :]) % n) / n
    return np.cos(ang), np.sin(ang)


def _channel_dft_mats(scale):
    c, s = _dft_cos_sin(FN_GROUP_DIM)
    eye = np.eye(FN_GROUPS)
    return (jnp.asarray(np.kron(eye, c) * scale, BF16), jnp.asarray(np.kron(eye, -s) * scale, BF16))


FFT_R = 64
FFT_GROUP = 4


def _store_halves(ref, lead, rows, val):
    ref[lead + (0, rows, slice(None))] = val[:, :LANES]
    ref[lead + (1, rows, slice(None))] = val[:, LANES:]


def _load_halves(ref, lead, rows):
    return jnp.concatenate([ref[lead + (0, rows, slice(None))], ref[lead + (1, rows, slice(None))]], axis=1)


def _fourier_lat_kernel(u_ref, cc_ref, sc_ref, ga_ref, gb_ref, tc_ref, ts_ref, o_ref, u_s, z_s):
    ub = u_ref[...].astype(BF16)
    every = slice(None)
    _store_halves(u_s, (0,), every, _dot(ub, cc_ref[...]))
    _store_halves(u_s, (1,), every, _dot(ub, sc_ref[...]))

    def stage_a(g, carry):
        n2s = [g * FFT_GROUP + i for i in range(FFT_GROUP)]
        xs = []
        for n2 in n2s:
            rows = pl.ds(n2, FFT_R, stride=FFT_R)
            xs.append(jnp.concatenate([_load_halves(u_s, (0,), rows), _load_halves(u_s, (1,), rows)],
                                      axis=0).astype(BF16))
        prods = [_dot(ga_ref[...], x) for x in xs]
        for n2, a in zip(n2s, prods):
            ar, ai = a[:FFT_R], a[FFT_R:]
            tc = jnp.concatenate([tc_ref[n2], tc_ref[n2]], axis=1)
            ts = jnp.concatenate([ts_ref[n2], ts_ref[n2]], axis=1)
            dst = pl.ds(pl.multiple_of(n2 * FFT_R, FFT_R), FFT_R)
            _store_halves(z_s, (0,), dst, ar * tc + ai * ts)
            _store_halves(z_s, (1,), dst, ai * tc - ar * ts)
        return carry

    lax.fori_loop(0, FFT_R // FFT_GROUP, stage_a, 0)

    def stage_b(g, carry):
        k1s = [g * FFT_GROUP + i for i in range(FFT_GROUP)]
        zs = []
        for k1 in k1s:
            rows = pl.ds(k1, FFT_R, stride=FFT_R)
            zs.append(jnp.concatenate([_load_halves(z_s, (0,), rows), _load_halves(z_s, (1,), rows)],
                                      axis=0).astype(BF16))
        prods = [_dot(gb_ref[...], z) for z in zs]
        for k1, p in zip(k1s, prods):
            _store_halves(o_ref, (), pl.ds(k1, FFT_R, stride=FFT_R), p)
        return carry

    lax.fori_loop(0, FFT_R // FFT_GROUP, stage_b, 0)


def _fourier_lat(z_fn, n_batch, seq):
    assert seq == FFT_R * FFT_R
    c, s = _dft_cos_sin(FFT_R)
    cc, sc = _channel_dft_mats((seq * FN_GROUP_DIM) ** -0.5)
    ga = jnp.asarray(np.block([[c, s], [-s, c]]), BF16)
    gb = jnp.asarray(np.concatenate([c, s], axis=1), BF16)
    k = np.arange(FFT_R)
    ang = 2.0 * np.pi * (k[:, None] * k[None, :]) / seq
    tc = jnp.asarray(np.broadcast_to(np.cos(ang)[:, :, None], (FFT_R, FFT_R, 128)), F32)
    ts = jnp.asarray(np.broadcast_to(np.sin(ang)[:, :, None], (FFT_R, FFT_R, 128)), F32)
    const2 = lambda b: (0, 0)
    const3 = lambda b: (0, 0, 0)
    return pl.pallas_call(
        _fourier_lat_kernel,
        grid=(n_batch,),
        in_specs=[
            pl.BlockSpec((seq, FN_WIDTH), lambda b: (b, 0)),
            pl.BlockSpec(cc.shape, const2),
            pl.BlockSpec(sc.shape, const2),
            pl.BlockSpec(ga.shape, const2),
            pl.BlockSpec(gb.shape, const2),
            pl.BlockSpec(tc.shape, const3),
            pl.BlockSpec(ts.shape, const3),
        ],
        out_specs=pl.BlockSpec((2, seq, LANES), lambda b: (0, b, 0)),
        out_shape=jax.ShapeDtypeStruct((2, n_batch * seq, LANES), F32),
        scratch_shapes=[pltpu.VMEM((2, 2, seq, LANES), F32), pltpu.VMEM((2, 2, seq, LANES), F32)],
        compiler_params=_cparams("parallel"),
        name="fourier_lat",
    )(z_fn, cc, sc, ga, gb, tc, ts)


def _fourier_ctx_kernel(u_ref, cc_ref, sc_ref, g_ref, o_ref):
    ub = u_ref[...].astype(BF16)
    x = jnp.concatenate([_dot(ub, cc_ref[...]), _dot(ub, sc_ref[...])], axis=0).astype(BF16)
    _store_halves(o_ref, (), slice(None), _dot(g_ref[...], x))


def _fourier_ctx(z_fn, n_batch, n_ctx, row0):
    blk0 = row0 // n_ctx
    c, s = _dft_cos_sin(n_ctx)
    cc, sc = _channel_dft_mats((n_ctx * FN_GROUP_DIM) ** -0.5)
    g = jnp.asarray(np.concatenate([c, s], axis=1), BF16)
    const2 = lambda b: (0, 0)
    return pl.pallas_call(
        _fourier_ctx_kernel,
        grid=(n_batch,),
        in_specs=[
            pl.BlockSpec((n_ctx, FN_WIDTH), lambda b: (blk0 + b, 0)),
            pl.BlockSpec(cc.shape, const2),
            pl.BlockSpec(sc.shape, const2),
            pl.BlockSpec(g.shape, const2),
        ],
        out_specs=pl.BlockSpec((2, n_ctx, LANES), lambda b: (0, b, 0)),
        out_shape=jax.ShapeDtypeStruct((2, n_batch * n_ctx, LANES), F32),
        compiler_params=_cparams("parallel"),
        name="fourier_ctx",
    )(z_fn, cc, sc, g)


NEG_BIG = -1e30
HEAD_PAIR = 2 * NA_HEAD_DIM


def _na_bias_table(rpb):
    wr, wc, w = NA_WIN_ROWS, NA_WIN_COLS, GRID_W
    col = np.arange(w)
    col_start = np.clip(col - wc // 2, 0, w - wc)
    in_win = (col[None, :] >= col_start[:, None]) & (col[None, :] < col_start[:, None] + wc)
    dc_idx = np.clip(col[None, :] - col[:, None], 1 - wc, wc - 1) + (wc - 1)
    h, n_dr, n_dc = rpb.shape
    onehot = jnp.asarray(dc_idx.reshape(-1)[None, :] == np.arange(n_dc)[:, None], F32)
    t = jnp.dot(rpb.astype(F32).reshape(h * n_dr, n_dc), onehot, precision=lax.Precision.HIGHEST)
    t = jnp.where(in_win.reshape(-1)[None, :], t, NEG_BIG).reshape(h, n_dr, w, w)
    per_s = [jnp.concatenate([t[:, s + j] for j in range(wr)], axis=-1) for s in range(wr)]
    return jnp.stack(per_s, axis=1)


def _softmax_pv_many(problems):
    scores = []
    for q, keys, vals, biases in problems:
        s = []
        for k, b in zip(keys, biases):
            si = _dot_nt(q, k)
            s.append(si if b is None else si + b)
        scores.append(s)
    maxes = []
    for s in scores:
        m = s[0].max(axis=-1, keepdims=True)
        for si in s[1:]:
            m = jnp.maximum(m, si.max(axis=-1, keepdims=True))
        maxes.append(m)
    probs = [[jnp.exp(si - m) for si in s] for s, m in zip(scores, maxes)]
    outs = []
    for (q, keys, vals, biases), ps in zip(problems, probs):
        acc, l = None, None
        for p, v in zip(ps, vals):
            li = p.sum(axis=-1, keepdims=True)
            oi = _dot(p.astype(BF16), v)
            acc = oi if acc is None else acc + oi
            l = li if l is None else l + li
        outs.append(acc / l)
    return outs


def _softmax_pv(q, keys, vals, biases):
    return _softmax_pv_many([(q, keys, vals, biases)])[0]


NA_ROW_GROUP = 4


def _na_lat_kernel(q_ref, k_ref, v_ref, kc_ref, vc_ref, bias_ref, o_ref):
    w = GRID_W
    n_loc = NA_WIN_ROWS * w
    rows = q_ref.shape[0] // w
    lane = lax.broadcasted_iota(jnp.int32, (w, HEAD_PAIR), 1)
    first = lane < NA_HEAD_DIM
    kc = kc_ref[...]
    vc = vc_ref[...]

    def body(g, carry):
        problems = []
        for i in range(NA_ROW_GROUP):
            r = g * NA_ROW_GROUP + i
            kr0 = jnp.clip(r - NA_WIN_ROWS // 2, 0, rows - NA_WIN_ROWS)
            s = kr0 - r + (NA_WIN_ROWS - 1)
            q = q_ref[pl.ds(pl.multiple_of(r * w, w), w), :] * jnp.asarray(NA_HEAD_DIM ** -0.5, BF16)
            ks = k_ref[pl.ds(pl.multiple_of(kr0 * w, w), n_loc), :]
            vs = v_ref[pl.ds(pl.multiple_of(kr0 * w, w), n_loc), :]
            zero = jnp.zeros_like(q)
            q2 = jnp.concatenate([jnp.where(first, q, zero), jnp.where(first, zero, q)], axis=0)
            bias = jnp.concatenate([bias_ref[0, s], bias_ref[1, s]], axis=0)
            problems.append((q2, [ks, kc], [vs, vc], [bias, None]))
        for i, o2 in enumerate(_softmax_pv_many(problems)):
            r = g * NA_ROW_GROUP + i
            o_ref[pl.ds(pl.multiple_of(r * w, w), w), :] = jnp.where(first, o2[:w], o2[w:]).astype(o_ref.dtype)
        return carry

    lax.fori_loop(0, rows // NA_ROW_GROUP, body, 0)


def _na_lat(qkv, bias, n_batch, seq, n_ctx):
    n_pairs = NA_WIDTH // HEAD_PAIR
    ctx0 = n_batch * seq // n_ctx
    wr, w = NA_WIN_ROWS, GRID_W
    return pl.pallas_call(
        _na_lat_kernel,
        grid=(n_batch, n_pairs),
        in_specs=[
            pl.BlockSpec((seq, HEAD_PAIR), lambda b, p: (b, p)),
            pl.BlockSpec((seq, HEAD_PAIR), lambda b, p: (b, n_pairs + p)),
            pl.BlockSpec((seq, HEAD_PAIR), lambda b, p: (b, 2 * n_pairs + p)),
            pl.BlockSpec((n_ctx, HEAD_PAIR), lambda b, p: (ctx0 + b, n_pairs + p)),
            pl.BlockSpec((n_ctx, HEAD_PAIR), lambda b, p: (ctx0 + b, 2 * n_pairs + p)),
            pl.BlockSpec((2, wr, w, wr * w), lambda b, p: (p, 0, 0, 0)),
        ],
        out_specs=pl.BlockSpec((seq, HEAD_PAIR), lambda b, p: (b, p)),
        out_shape=jax.ShapeDtypeStruct((n_batch * seq, NA_WIDTH), BF16),
        compiler_params=_cparams("parallel", "parallel"),
        name="na_lat",
    )(qkv, qkv, qkv, qkv, qkv, bias)


def _na_ctx_kernel(q_ref, k_ref, v_ref, o_ref):
    lane = lax.broadcasted_iota(jnp.int32, q_ref.shape, 1)
    first = lane < NA_HEAD_DIM
    q = q_ref[...] * jnp.asarray(NA_HEAD_DIM ** -0.5, BF16)
    k = k_ref[...]
    v = v_ref[...]
    zero = jnp.zeros_like(q)
    o0 = _softmax_pv(jnp.where(first, q, zero), [k], [v], [None])
    o1 = _softmax_pv(jnp.where(first, zero, q), [k], [v], [None])
    o_ref[...] = jnp.where(first, o0, o1).astype(o_ref.dtype)


def _na_ctx(qkv, n_batch, seq, n_ctx):
    n_pairs = NA_WIDTH // HEAD_PAIR
    ctx0 = n_batch * seq // n_ctx
    return pl.pallas_call(
        _na_ctx_kernel,
        grid=(n_batch, n_pairs),
        in_specs=[
            pl.BlockSpec((n_ctx, HEAD_PAIR), lambda b, p: (ctx0 + b, p)),
            pl.BlockSpec((n_ctx, HEAD_PAIR), lambda b, p: (ctx0 + b, n_pairs + p)),
            pl.BlockSpec((n_ctx, HEAD_PAIR), lambda b, p: (ctx0 + b, 2 * n_pairs + p)),
        ],
        out_specs=pl.BlockSpec((n_ctx, HEAD_PAIR), lambda b, p: (b, p)),
        out_shape=jax.ShapeDtypeStruct((n_batch * n_ctx, NA_WIDTH), BF16),
        compiler_params=_cparams("parallel", "parallel"),
        name="na_ctx",
    )(qkv, qkv, qkv)


HG_CHUNK = 64
HG_LEVELS = (32, 16)
HG_DIAG = 16
HG_BATCH = 4


def _hgrn_tables():
    c = HG_CHUNK
    tri, masks, refs, last = [], [], [], []
    for reverse in (False, True):
        u = np.arange(c)[::-1] if reverse else np.arange(c)
        row_of = {int(uu): t for t, uu in enumerate(u)}
        ut, uj = u[:, None], u[None, :]
        tri.append(uj <= ut)
        lv, rf = [], []
        for h in HG_LEVELS:
            same = (ut // (2 * h)) == (uj // (2 * h))
            up_t, up_j = (ut % (2 * h)) >= h, (uj % (2 * h)) >= h
            lv.append(same & up_t & ~up_j)
            rf.append([row_of[int(uu) // (2 * h) * (2 * h) + h - 1] for uu in u])
        same_d = (ut // HG_DIAG) == (uj // HG_DIAG)
        lv.append(same_d & (uj <= ut))
        rf.append([row_of.get(int(uu) // HG_DIAG * HG_DIAG - 1, -1) for uu in u])
        masks.append(np.stack([np.tile(x, (1, HG_HEADS)) for x in lv]).astype(np.float32))
        refs.append(rf)
        last.append(row_of[c - 1])
    hm = (np.arange(HG_WIDTH)[:, None] // HG_HEAD_DIM) == (np.arange(HG_WIDTH)[None, :] // HG_HEAD_DIM)
    arrays = (jnp.asarray(np.stack(tri), BF16), jnp.asarray(np.stack(masks), F32), jnp.asarray(hm, F32))
    return arrays, refs, last


def _ref_rows(b, ref):
    pieces, t = [], 0
    while t < len(ref):
        t1 = t
        while t1 < len(ref) and ref[t1] == ref[t]:
            t1 += 1
        shape = (t1 - t, b.shape[1])
        pieces.append(jnp.zeros(shape, b.dtype) if ref[t] < 0 else jnp.broadcast_to(b[ref[t]:ref[t] + 1, :], shape))
        t = t1
    return jnp.concatenate(pieces, axis=0)


def _split3(x):
    hi = x.astype(BF16)
    r = x - hi.astype(F32)
    mid = r.astype(BF16)
    lo = (r - mid.astype(F32)).astype(BF16)
    return hi, mid, lo


def _hgrn_chunks(chains, hm, refs, last):
    hm_b = hm.astype(BF16)

    def heads_bd(x):
        return jnp.concatenate([x.astype(BF16)] * HG_HEADS, axis=0) * hm_b

    qs, ks, splits = [], [], []
    for zq, zv, zf, lbp, tri, masks, st, d in chains:
        e = jnp.exp(-jnp.abs(zf))
        inv = 1.0 / (1.0 + e)
        log_sig = jnp.minimum(zf, 0.0) - jnp.log(1.0 + e)
        sig_neg = jnp.where(zf >= 0, e * inv, inv)
        a = lbp[0:1, :]
        cc = lbp[1:2, :] + log_sig
        log_f = jnp.maximum(a, cc) + jnp.log(1.0 + jnp.exp(-jnp.abs(a - cc)))
        qs.append(_silu(zq))
        ks.append(lbp[2:3, :] * sig_neg)
        splits.append(_split3(log_f))
    bs = [_dot(c[4], hi) + _dot(c[4], mid) + _dot(c[4], lo) for c, (hi, mid, lo) in zip(chains, splits)]
    b_lasts = [b[last[c[7]]:last[c[7]] + 1] for c, b in zip(chains, bs)]
    a_alls = [None] * len(chains)
    for i in range(len(HG_LEVELS) + 1):
        pairs = []
        for c, q, k, b in zip(chains, qs, ks, bs):
            rel = b - _ref_rows(b, refs[c[7]][i])
            if i < len(HG_LEVELS):
                pairs.append((q * jnp.exp(jnp.minimum(rel, 0.0)), k * jnp.exp(jnp.minimum(-rel, 0.0))))
            else:
                pairs.append((q * jnp.exp(rel), k * jnp.exp(-rel)))
        for n, (c, (qx, kx)) in enumerate(zip(chains, pairs)):
            ai = _dot_nt(qx.astype(BF16), heads_bd(kx)) * c[5][i]
            a_alls[n] = ai if a_alls[n] is None else a_alls[n] + ai
    outs = [_dot(a_all.astype(BF16), heads_bd(c[1])) + _dot_nt((q * jnp.exp(b)).astype(BF16), c[6].astype(BF16))
            for c, q, b, a_all in zip(chains, qs, bs, a_alls)]
    states = []
    for c, k, b, b_last in zip(chains, ks, bs, b_lasts):
        kl = (k * jnp.exp(b_last - b)).astype(BF16)
        upd = lax.dot_general(c[1].astype(BF16), kl, (((0,), (0,)), ((), ())), preferred_element_type=F32)
        states.append(c[6] * jnp.exp(b_last) + upd * hm)
    return outs, states


def _hgrn_kernel(refs, last, *args):
    nb = HG_BATCH
    z_refs = args[:3 * nb]
    s0_ref, lbp_ref, tri_ref, mask_ref, hm_ref = args[3 * nb:3 * nb + 5]
    of_ref, ob_ref, sfin_ref, st = args[3 * nb + 5:]
    i = pl.program_id(1)

    @pl.when(i == 0)
    def _():
        st[...] = s0_ref[...]

    w = HG_WIDTH
    hm = hm_ref[...]
    chains = []
    for bb in range(nb):
        zf_ref, zb_ref, zbg_ref = z_refs[3 * bb:3 * bb + 3]
        chains.append((zf_ref[:, 0:w], zf_ref[:, w:2 * w], zf_ref[:, 2 * w:3 * w], lbp_ref[0], tri_ref[0],
                       mask_ref[0], st[bb, 0], 0))
        chains.append((zb_ref[:, 0:w], zb_ref[:, w:2 * w], zbg_ref[...], lbp_ref[1], tri_ref[1],
                       mask_ref[1], st[bb, 1], 1))
    outs, states = _hgrn_chunks(chains, hm, refs, last)
    for bb in range(nb):
        of_ref[bb] = outs[2 * bb]
        ob_ref[bb] = outs[2 * bb + 1]
        st[bb, 0] = states[2 * bb]
        st[bb, 1] = states[2 * bb + 1]

    @pl.when(i == pl.num_programs(1) - 1)
    def _():
        sfin_ref[...] = st[...]


def _hgrn(z_hg, s0, lbp, tables, n_batch, n_tok, row0):
    (tri, masks, hm), refs, last = tables
    c, w, nb = HG_CHUNK, HG_WIDTH, HG_BATCH
    assert n_batch % nb == 0 and n_tok % c == 0
    nch = n_tok // c
    base = row0 // c
    const3 = lambda b, i: (0, 0, 0)
    in_specs = []
    for bb in range(nb):
        fwd = lambda b, i, bb=bb: (base + (b * nb + bb) * nch + i, 0)
        bwd = lambda b, i, bb=bb: (base + (b * nb + bb) * nch + (nch - 1 - i), 0)
        bwd_gate = lambda b, i, bb=bb: (base + (b * nb + bb) * nch + (nch - 1 - i), 3)
        in_specs += [pl.BlockSpec((c, 3 * w), fwd), pl.BlockSpec((c, 2 * w), bwd), pl.BlockSpec((c, w), bwd_gate)]
    in_specs += [
        pl.BlockSpec((nb, 2, w, w), lambda b, i: (b, 0, 0, 0)),
        pl.BlockSpec(lbp.shape, const3),
        pl.BlockSpec(tri.shape, const3),
        pl.BlockSpec(masks.shape, lambda b, i: (0, 0, 0, 0)),
        pl.BlockSpec(hm.shape, lambda b, i: (0, 0)),
    ]
    o_f, o_b, s_fin = pl.pallas_call(
        functools.partial(_hgrn_kernel, refs, last),
        grid=(n_batch // nb, nch),
        in_specs=in_specs,
        out_specs=[
            pl.BlockSpec((nb, c, w), lambda b, i: (b, i, 0)),
            pl.BlockSpec((nb, c, w), lambda b, i: (b, nch - 1 - i, 0)),
            pl.BlockSpec((nb, 2, w, w), lambda b, i: (b, 0, 0, 0)),
        ],
        out_shape=[
            jax.ShapeDtypeStruct((n_batch, n_tok, w), F32),
            jax.ShapeDtypeStruct((n_batch, n_tok, w), F32),
            jax.ShapeDtypeStruct((n_batch, 2, w, w), F32),
        ],
        scratch_shapes=[pltpu.VMEM((nb, 2, w, w), F32)],
        compiler_params=_cparams("parallel", "arbitrary"),
        name="hgrn",
    )(*([z_hg] * (3 * nb)), s0, lbp, tri, masks, hm)
    return o_f.reshape(n_batch * n_tok, w), o_b.reshape(n_batch * n_tok, w), s_fin


def _hgrn_lb_params(lower_l):
    rows = jnp.stack([jnp.log(lower_l), jnp.log1p(-lower_l), 1.0 - lower_l], axis=1)
    return jnp.pad(rows, ((0, 0), (0, 5), (0, 0)))


def _merge_kernel(x_ref, mod_ref, g1_ref, g2_ref, fn_ref, na_ref, of_ref, ob_ref, hgz_ref, hgain_ref,
                  hmean_ref, wfn_ref, wna_ref, whg_ref, wbg_ref, wout_ref, x1_ref, h2_ref, h2t_ref):
    d = x_ref.shape[1]
    x = x_ref[...]
    h = _norm_mod(x, g1_ref[...], mod_ref[0, 0:1, :], mod_ref[0, 1:2, :]).astype(BF16)
    o = of_ref[...] + ob_ref[...]
    hi, mid, lo = _split3(o * o)
    ms = _dot(hi, hmean_ref[...]) + _dot(mid, hmean_ref[...]) + _dot(lo, hmean_ref[...])
    y_hg_in = o * lax.rsqrt(ms + RMS_EPS) * hgain_ref[...] * _silu(hgz_ref[...])
    y_fn = _dot(jnp.concatenate([fn_ref[0], fn_ref[1]], axis=1).astype(BF16), wfn_ref[...])
    y_na = _dot(na_ref[...], wna_ref[...])
    y_hg = _dot(y_hg_in.astype(BF16), whg_ref[...])
    m = (_sigmoid(_dot(h, wbg_ref[:, 0:d])) * y_fn
         + _sigmoid(_dot(h, wbg_ref[:, d:2 * d])) * y_na
         + _sigmoid(_dot(h, wbg_ref[:, 2 * d:3 * d])) * y_hg)
    x1 = x + mod_ref[0, 2:3, :] * _dot(m.astype(BF16), wout_ref[...])
    x1_ref[...] = x1
    h2 = _norm_mod(x1, g2_ref[...], mod_ref[0, 3:4, :], mod_ref[0, 4:5, :])
    h2_ref[...] = h2
    tm = x.shape[0]
    for s in range(SUBLANES):
        h2t_ref[pl.ds(s, tm, stride=SUBLANES), :] = h2[:, s * LANES:(s + 1) * LANES]


def _merge(x_all, mod, gain1, gain2, fn, na, o_f, o_b, z_hg, hgain, w_fnb, w_nab, w_hgb, w_bg, w_out,
           n_rows, n_lat, seq, tm=512):
    t, d = n_rows, x_all.shape[1]
    n_lat_tiles = n_lat // tm
    per_batch = seq // tm
    ctx_row = mod.shape[0] - 1
    w = HG_WIDTH
    hmean = jnp.asarray(np.kron(np.eye(HG_HEADS), np.full((HG_HEAD_DIM, HG_HEAD_DIM), 1.0 / HG_HEAD_DIM)), BF16)

    def mod_map(i):
        return (jnp.where(i < n_lat_tiles, i // per_batch, ctx_row), 0, 0)

    const = lambda i: (0, 0)
    row = lambda i: (i, 0)
    full = lambda a: pl.BlockSpec(a.shape, const)
    return pl.pallas_call(
        _merge_kernel,
        grid=(t // tm,),
        in_specs=[
            pl.BlockSpec((tm, d), row),
            pl.BlockSpec((1, 6, d), mod_map),
            pl.BlockSpec((1, d), const),
            pl.BlockSpec((1, d), const),
            pl.BlockSpec((2, tm, LANES), lambda i: (0, i, 0)),
            pl.BlockSpec((tm, NA_WIDTH), row),
            pl.BlockSpec((tm, w), row),
            pl.BlockSpec((tm, w), row),
            pl.BlockSpec((tm, w), lambda i: (i, 4)),
            pl.BlockSpec((1, w), const),
            full(hmean), full(w_fnb), full(w_nab), full(w_hgb), full(w_bg), full(w_out),
        ],
        out_specs=[pl.BlockSpec((tm, d), row), pl.BlockSpec((tm, d), row), pl.BlockSpec((tm * SUBLANES, LANES), row)],
        out_shape=[jax.ShapeDtypeStruct((t, d), F32), jax.ShapeDtypeStruct((t, d), F32),
                   jax.ShapeDtypeStruct((t * SUBLANES, LANES), F32)],
        compiler_params=_cparams("parallel"),
        name="merge",
    )(x_all, mod, gain1.reshape(1, d), gain2.reshape(1, d), fn, na, o_f, o_b, z_hg,
      jnp.tile(hgain, HG_HEADS).reshape(1, w), hmean, w_fnb, w_nab, w_hgb, w_bg, w_out)


def _router_kernel(h_ref, whi_ref, wlo_ref, b_ref, idx_ref, gate_ref, cnt_ref):
    h = h_ref[...]
    hi = h.astype(BF16)
    lo = (h - hi.astype(F32)).astype(BF16)
    logits = _dot(hi, whi_ref[...]) + (_dot(hi, wlo_ref[...]) + _dot(lo, whi_ref[...]))
    scores = _sigmoid(logits)
    sel = scores + b_ref[...]
    tm, ne = sel.shape
    col = lax.broadcasted_iota(jnp.int32, (tm, ne), 1)
    out_lane = lax.broadcasted_iota(jnp.int32, (tm, LANES), 1)
    idx_out = jnp.zeros((tm, LANES), jnp.int32)
    gate_out = jnp.zeros((tm, LANES), F32)
    total = jnp.zeros((tm, 1), F32)
    chosen = jnp.zeros((tm, ne), F32)
    for k in range(TOP_K):
        m = sel.max(axis=-1, keepdims=True)
        idx = jnp.where(sel == m, col, ne).min(axis=-1, keepdims=True)
        hit = col == idx
        g = jnp.where(hit, scores, 0.0).sum(axis=-1, keepdims=True)
        sel = jnp.where(hit, -jnp.inf, sel)
        chosen = jnp.where(hit, 1.0, chosen)
        idx_out = jnp.where(out_lane == k, idx, idx_out)
        gate_out = jnp.where(out_lane == k, g, gate_out)
        total = total + g
    idx_ref[...] = idx_out
    gate_ref[...] = gate_out * (ROUTE_SCALE / total)
    for part in range(tm // COUNT_ROWS):
        c = chosen[part * COUNT_ROWS:(part + 1) * COUNT_ROWS].sum(axis=0, keepdims=True)
        cnt_ref[part * SUBLANES:(part + 1) * SUBLANES, :] = jnp.broadcast_to(c, (SUBLANES, ne)).astype(jnp.int32)


def _router(h2, w_router, b_router, tm=512):
    t, d = h2.shape
    ne = w_router.shape[1]
    w_hi = w_router.astype(BF16)
    w_lo = (w_router - w_hi.astype(F32)).astype(BF16)
    const = lambda i: (0, 0)
    row = lambda i: (i, 0)
    return pl.pallas_call(
        _router_kernel,
        grid=(t // tm,),
        in_specs=[pl.BlockSpec((tm, d), row), pl.BlockSpec((d, ne), const), pl.BlockSpec((d, ne), const),
                  pl.BlockSpec((1, ne), const)],
        out_specs=[pl.BlockSpec((tm, LANES), row), pl.BlockSpec((tm, LANES), row),
                   pl.BlockSpec((tm // COUNT_ROWS * SUBLANES, ne), row)],
        out_shape=[jax.ShapeDtypeStruct((t, LANES), jnp.int32), jax.ShapeDtypeStruct((t, LANES), F32),
                   jax.ShapeDtypeStruct((t // COUNT_ROWS * SUBLANES, ne), jnp.int32)],
        compiler_params=_cparams("parallel"),
        name="router",
    )(h2, w_hi, w_lo, b_router.reshape(1, ne).astype(F32))


MOE_TILE = 4352
COUNT_ROWS = 256
MOE_ROWS = 160
MOE_GROUP = 4
MOE_SCATTER_GROUP = 4


def _moe_dispatch(idx, gate, counts, n_tiles, tile):
    t, k = idx.shape
    pad = n_tiles * tile - t
    slots = tile * k
    e = jnp.pad(idx, ((0, pad), (0, 0)), constant_values=N_EXPERTS).reshape(n_tiles, slots)
    g = jnp.pad(gate, ((0, pad), (0, 0))).reshape(n_tiles, slots)
    key = e * slots + jnp.arange(slots, dtype=jnp.int32)[None, :]
    key_sorted, g_sorted = lax.sort((key, g), dimension=1, num_keys=1)
    row8_sorted = (key_sorted % slots) // k * SUBLANES
    group = tile // (t // counts.shape[0])
    c = jnp.pad(counts, ((0, n_tiles * group - counts.shape[0]), (0, 0))).reshape(n_tiles, group, -1).sum(axis=1)
    offs = jnp.concatenate([jnp.zeros((n_tiles, 1), jnp.int32), jnp.cumsum(c, axis=1)], axis=1)
    offs = jnp.pad(offs.astype(jnp.int32).reshape(-1), (0, SUBLANES))
    return row8_sorted.reshape(-1), g_sorted.reshape(-1), offs


def _moe_kernel(offs_ref, x_ref, row_ref, gate_ref, wg_ref, wu_ref, wd_ref, o_ref, g_a, g_b, y_a, y_b):
    i = pl.program_id(0)
    j = pl.program_id(1)
    nj = pl.num_programs(1)
    r, sl = MOE_ROWS, SUBLANES
    n_slots = row_ref.shape[0]
    all_rows = list(range(r))

    def segment(ee):
        start = offs_ref[i * (N_EXPERTS + 1) + ee]
        n = offs_ref[i * (N_EXPERTS + 1) + ee + 1] - start
        n_fast = jnp.where(start <= n_slots - r, jnp.minimum(n, r), 0)
        return start, jnp.minimum(start, n_slots - r), n_fast, n

    def gather(buf, base):
        for row in all_rows:
            buf[pl.ds(row * sl, sl), :] = x_ref[pl.ds(pl.multiple_of(row_ref[base + row], sl), sl), :]

    def experts(k, buf_in, buf_out, n_valid):
        x = jnp.concatenate([buf_in[pl.ds(s, r, stride=sl), :] for s in range(sl)], axis=1).astype(BF16)
        hid = (_silu(_dot(x, wg_ref[k])) * _dot(x, wu_ref[k])).astype(BF16)
        y = _dot(hid, wd_ref[k])
        y = jnp.where(lax.broadcasted_iota(jnp.int32, (r, 1), 0) < n_valid, y, 0.0)
        for s in range(sl):
            buf_out[pl.ds(s, r, stride=sl), :] = y[:, s * LANES:(s + 1) * LANES]

    def scatter(buf, base):
        for g0 in range(0, r, MOE_SCATTER_GROUP):
            dsts, news = [], []
            for row in all_rows[g0:g0 + MOE_SCATTER_GROUP]:
                dst = pl.ds(pl.multiple_of(row_ref[base + row], sl), sl)
                dsts.append(dst)
                news.append(o_ref[dst, :] + gate_ref[base + row] * buf[pl.ds(row * sl, sl), :])
            for dst, new in reversed(list(zip(dsts, news))):
                o_ref[dst, :] = new

    def slow_path(k, start, n_fast, n, g_free, y_free):
        def chunk(c, carry):
            base = start + n_fast + c * r
            n_valid = jnp.minimum(n - n_fast - c * r, r)

            def slot(row):
                return jnp.minimum(base + row, n_slots - 1)

            def gather_group(gi, carry):
                for row in [gi * sl + q for q in range(sl)]:
                    src = pl.ds(pl.multiple_of(row_ref[slot(row)], sl), sl)
                    g_free[pl.ds(pl.multiple_of(row * sl, sl), sl), :] = x_ref[src, :]
                return carry

            def scatter_group(gi, carry):
                dsts, news = [], []
                for row in [gi * sl + q for q in range(sl)]:
                    dst = pl.ds(pl.multiple_of(row_ref[slot(row)], sl), sl)
                    dsts.append(dst)
                    news.append(o_ref[dst, :] + gate_ref[slot(row)] * y_free[pl.ds(pl.multiple_of(row * sl, sl), sl), :])
                for dst, new in reversed(list(zip(dsts, news))):
                    o_ref[dst, :] = new
                return carry

            lax.fori_loop(0, r // sl, gather_group, 0)
            experts(k, g_free, y_free, n_valid)
            lax.fori_loop(0, r // sl, scatter_group, 0)
            return carry

        lax.fori_loop(0, (n - n_fast + r - 1) // r, chunk, 0)

    grp = MOE_GROUP
    e0 = grp * j
    segs = [segment(e0 + k) for k in range(grp)]
    _, base_next, _, _ = segment(jnp.minimum(e0 + grp, N_EXPERTS - 1))
    _, base_prev, _, _ = segment(jnp.maximum(e0 - 1, 0))
    bases = [base_prev] + [s[1] for s in segs] + [base_next]
    bufs = [(g_a, y_a), (g_b, y_b)]

    @pl.when(j == 0)
    def _():
        o_ref[...] = jnp.zeros_like(o_ref)
        y_b[...] = jnp.zeros_like(y_b)
        gather(g_a, bases[1])

    for k in range(grp):
        g_cur, y_cur = bufs[k % 2]
        g_other, y_other = bufs[(k + 1) % 2]
        experts(k, g_cur, y_cur, segs[k][2])
        gather(g_other, bases[k + 2])
        scatter(y_other, bases[k])

    @pl.when(j == nj - 1)
    def _():
        scatter(y_b, bases[grp])

    for k in range(grp):
        start, _, n_fast, n = segs[k]

        @pl.when(n > n_fast)
        def _(k=k, start=start, n_fast=n_fast, n=n):
            slow_path(k, start, n_fast, n, g_b, y_a)


def _moe_routed(x_tiles, tok, gate, offs, w_gate, w_up, w_down, layer, n_tiles, tile):
    _, ne, d, f = w_gate.shape
    grp = MOE_GROUP
    assert grp % 2 == 0 and ne % grp == 0
    rows = tile * SUBLANES
    slots = tile * TOP_K
    grid_spec = pltpu.PrefetchScalarGridSpec(
        num_scalar_prefetch=1,
        grid=(n_tiles, ne // grp),
        in_specs=[
            pl.BlockSpec((rows, LANES), lambda i, j, offs: (i, 0), pipeline_mode=pl.Buffered(1)),
            pl.BlockSpec((slots,), lambda i, j, offs: (i,), memory_space=pltpu.SMEM, pipeline_mode=pl.Buffered(1)),
            pl.BlockSpec((slots,), lambda i, j, offs: (i,), memory_space=pltpu.SMEM, pipeline_mode=pl.Buffered(1)),
            pl.BlockSpec((None, grp, d, f), lambda i, j, offs: (layer, j, 0, 0)),
            pl.BlockSpec((None, grp, d, f), lambda i, j, offs: (layer, j, 0, 0)),
            pl.BlockSpec((None, grp, f, d), lambda i, j, offs: (layer, j, 0, 0)),
        ],
        out_specs=pl.BlockSpec((rows, LANES), lambda i, j, offs: (i, 0), pipeline_mode=pl.Buffered(1)),
        scratch_shapes=[pltpu.VMEM((MOE_ROWS * SUBLANES, LANES), F32)] * 4,
    )
    return pl.pallas_call(
        _moe_kernel,
        grid_spec=grid_spec,
        out_shape=jax.ShapeDtypeStruct(x_tiles.shape, F32),
        compiler_params=_cparams("arbitrary", "arbitrary"),
        name="moe_routed",
    )(offs, x_tiles, tok, gate, w_gate, w_up, w_down)


def _shared_kernel(final, h_ref, routed_ref, x1_ref, mod_ref, wg_ref, wu_ref, wd_ref, fg_ref, o_ref):
    h = h_ref[...].astype(BF16)
    hid = (_silu(_dot(h, wg_ref[...])) * _dot(h, wu_ref[...])).astype(BF16)
    tm = h.shape[0]
    routed = jnp.concatenate([routed_ref[pl.ds(s, tm, stride=SUBLANES), :] for s in range(SUBLANES)], axis=1)
    x2 = x1_ref[...] + mod_ref[0, 5:6, :] * (routed + _dot(hid, wd_ref[...]))
    if final:
        x2 = x2 * lax.rsqrt(jnp.mean(x2 * x2, axis=-1, keepdims=True) + RMS_EPS) * fg_ref[...]
    o_ref[...] = x2


def _shared_residual(h2, routed, x1, mod, ws_gate, ws_up, ws_down, final_gain, final, n_lat, seq, tm=512):
    t, d = x1.shape
    n_lat_tiles = n_lat // tm
    per_batch = seq // tm
    ctx_row = mod.shape[0] - 1

    def mod_map(i):
        return (jnp.where(i < n_lat_tiles, i // per_batch, ctx_row), 0, 0)

    const = lambda i: (0, 0)
    row = lambda i: (i, 0)
    full = lambda a: pl.BlockSpec(a.shape, const)
    return pl.pallas_call(
        functools.partial(_shared_kernel, final),
        grid=(t // tm,),
        in_specs=[pl.BlockSpec((tm, d), row), pl.BlockSpec((tm * SUBLANES, LANES), row), pl.BlockSpec((tm, d), row),
                  pl.BlockSpec((1, 6, d), mod_map), full(ws_gate), full(ws_up), full(ws_down),
                  pl.BlockSpec((1, d), const)],
        out_specs=pl.BlockSpec((tm, d), row),
        out_shape=jax.ShapeDtypeStruct((t, d), F32),
        compiler_params=_cparams("parallel"),
        name="shared_residual",
    )(h2, routed, x1, mod, ws_gate, ws_up, ws_down, final_gain.reshape(1, d))


MOD_ROWS = 16


def kernel(x, c, ctx, c_ctx, w_ada, b_ada, norm1_gain, norm2_gain, w_in, na_rpb, hg_lower_bounds, hg_norm_gain,
           w_fn_branch, w_na_branch, w_hg_branch, w_out, w_router, b_router, w_exp_gate, w_exp_up, w_exp_down,
           w_sh_gate, w_sh_up, w_sh_down, final_gain):
    b, n, d = x.shape
    nc = ctx.shape[1]
    depth = w_ada.shape[0]
    n_lat, n_ctx = b * n, b * nc

    sm = jax.nn.softmax(hg_lower_bounds.astype(F32), axis=0)
    cs = jnp.cumsum(sm, axis=0)
    lower = cs - cs[0]

    cin = jnp.concatenate([c, c_ctx[None, :], jnp.zeros((MOD_ROWS - b - 1, d), F32)], axis=0)
    mod = _ada_mod(cin, w_ada, b_ada)[:, :b + 1].reshape(depth, b + 1, 6, d)
    hg_consts = _hgrn_tables()
    zero_state = jnp.zeros((b, 2, HG_WIDTH, HG_WIDTH), F32)
    wx_gate, wx_up, wx_down = w_exp_gate.astype(BF16), w_exp_up.astype(BF16), w_exp_down.astype(BF16)

    x_all = jnp.concatenate([x.reshape(n_lat, d), ctx.reshape(n_ctx, d)], axis=0)
    for l in range(depth):
        last = l == depth - 1
        w = w_in[l].astype(BF16)
        c0, c1, c2 = FN_WIDTH, FN_WIDTH + 3 * NA_WIDTH, FN_WIDTH + 3 * NA_WIDTH + 5 * HG_WIDTH
        z_fn, z_qkv, z_hg = _proj_in(x_all, mod[l], norm1_gain[l], w[:, :c0], w[:, c0:c1], w[:, c1:c2], n_lat, n)

        lbp = _hgrn_lb_params(lower[l])
        of_c, ob_c, s_ctx = _hgrn(z_hg, zero_state, lbp, hg_consts, b, nc, n_lat)
        of_l, ob_l, _ = _hgrn(z_hg, s_ctx, lbp, hg_consts, b, n, 0)
        fn_l = _fourier_lat(z_fn, b, n)
        na_l = _na_lat(z_qkv, _na_bias_table(na_rpb[l]), b, n, nc)
        if last:
            n_rows = n_lat
            fn_a, na_a, of_a, ob_a = fn_l, na_l, of_l, ob_l
        else:
            n_rows = n_lat + n_ctx
            cat = lambda p, q: jnp.concatenate([p, q], axis=0)
            fn_a = jnp.concatenate([fn_l, _fourier_ctx(z_fn, b, nc, n_lat)], axis=1)
            na_a = cat(na_l, _na_ctx(z_qkv, b, n, nc))
            of_a, ob_a = cat(of_l, of_c), cat(ob_l, ob_c)
        x1, h2, h2_tiles = _merge(x_all, mod[l], norm1_gain[l], norm2_gain[l], fn_a, na_a, of_a, ob_a, z_hg, hg_norm_gain[l],
                        w_fn_branch[l].astype(BF16), w_na_branch[l].astype(BF16), w_hg_branch[l].astype(BF16),
                        w[:, c2:], w_out[l].astype(BF16), n_rows, n_lat, n)

        idx, gate, counts = _router(h2, w_router[l], b_router[l])
        n_tiles = -(-n_rows // MOE_TILE)
        tile = -(-n_rows // (n_tiles * COUNT_ROWS)) * COUNT_ROWS
        tok, gate_sorted, offs = _moe_dispatch(idx[:, :TOP_K], gate[:, :TOP_K], counts[::SUBLANES], n_tiles, tile)
        routed = _moe_routed(h2_tiles, tok, gate_sorted, offs, wx_gate, wx_up, wx_down, l, n_tiles, tile)
        x_all = _shared_residual(h2, routed, x1, mod[l], w_sh_gate[l].astype(BF16),
                                 w_sh_up[l].astype(BF16), w_sh_down[l].astype(BF16), final_gain, last, n_lat, n)
    return x_all.reshape(b, n, d)
```
